```python
import jax, jax.numpy as jnp
from jax import lax
import numpy as np

D_MODEL = 4096
BATCH = 2
SEQ = 8192
DEPTH = 2

MLA_HEADS = 16
MLA_Q_RANK = 768
MLA_KV_RANK = 512
MLA_NOPE = 128
MLA_ROPE = 64
MLA_V = 128
ROPE_THETA = 10000.0
GLA_HEADS = 4
GLA_DK = 128
GLA_DV = 256
GLA_GATE_RANK = 16
GLA_GATE_NORM = 16.0
GLA_CHUNK = 64
DSA_HEADS = 8
DSA_KV_HEADS = 2
DSA_DH = 128
IDX_HEADS = 32
IDX_DH = 64
IDX_TOPK = 256
Q_BLOCK = 128
NORM_EPS = 1e-6
NEG = -1e30

MLA_WIDTH = MLA_HEADS * MLA_V
GLA_WIDTH = GLA_HEADS * GLA_DV
DSA_WIDTH = DSA_HEADS * DSA_DH
N_BRANCH = 3

IN_SPLITS = (
    MLA_Q_RANK, MLA_KV_RANK, MLA_ROPE,
    GLA_HEADS * GLA_DK, GLA_HEADS * GLA_DK, GLA_WIDTH, GLA_GATE_RANK,
    DSA_WIDTH, DSA_KV_HEADS * DSA_DH, DSA_KV_HEADS * DSA_DH,
    IDX_HEADS * IDX_DH, IDX_DH, IDX_HEADS,
    MLA_WIDTH, GLA_WIDTH, DSA_WIDTH,
)
IN_WIDTH = sum(IN_SPLITS)

kernel_name = "hybrid_mla_gla_dsa_gated_merge"


def rms_norm(x, g):
    xf = x.astype(jnp.float32)
    y = xf * lax.rsqrt(jnp.mean(xf * xf, axis=-1, keepdims=True) + NORM_EPS)
    return (y * g.astype(jnp.float32)).astype(x.dtype)


def rope_tables(positions, dim):
    inv = ROPE_THETA ** (-jnp.arange(0, dim, 2, dtype=jnp.float32) / dim)
    ang = positions.astype(jnp.float32)[..., None] * inv
    return jnp.cos(ang), jnp.sin(ang)


def apply_rope(x, cos, sin):
    xf = x.astype(jnp.float32)
    x1, x2 = jnp.split(xf, 2, axis=-1)
    c = cos[:, :, None, :]
    s = sin[:, :, None, :]
    return jnp.concatenate([x1 * c - x2 * s, x1 * s + x2 * c], axis=-1).astype(x.dtype)


def causal_block_attention(q, k, v, scale):
    B, S, H, _ = q.shape
    key_idx = jnp.arange(S)
    qs = q * scale

    def block(i):
        start = i * Q_BLOCK
        qb = lax.dynamic_slice_in_dim(qs, start, Q_BLOCK, axis=1)
        s = jnp.einsum('bqhd,bkhd->bhqk', qb, k, preferred_element_type=jnp.float32)
        q_idx = start + jnp.arange(Q_BLOCK)
        mask = key_idx[None, :] <= q_idx[:, None]
        p = jax.nn.softmax(jnp.where(mask[None, None], s, NEG), axis=-1)
        return jnp.einsum('bhqk,bkhd->bqhd', p.astype(v.dtype), v)

    out = lax.map(block, jnp.arange(S // Q_BLOCK))
    return out.transpose(1, 0, 2, 3, 4).reshape(B, S, H * v.shape[-1])


def mla_mixer(cq, ckv, krope, cos, sin, g_q, w_uq, g_kv, w_ukv):
    B, S, _ = cq.shape
    q = (rms_norm(cq, g_q) @ w_uq).reshape(B, S, MLA_HEADS, MLA_NOPE + MLA_ROPE)
    q = jnp.concatenate([q[..., :MLA_NOPE], apply_rope(q[..., MLA_NOPE:], cos, sin)], axis=-1)
    kv = (rms_norm(ckv, g_kv) @ w_ukv).reshape(B, S, MLA_HEADS, MLA_NOPE + MLA_V)
    k_nope, v = kv[..., :MLA_NOPE], kv[..., MLA_NOPE:]
    k_r = apply_rope(krope[:, :, None, :], cos, sin)
    k = jnp.concatenate([k_nope, jnp.broadcast_to(k_r, (B, S, MLA_HEADS, MLA_ROPE))], axis=-1)
    return causal_block_attention(q, k, v, (MLA_NOPE + MLA_ROPE) ** -0.5)


def gla_mixer(q, k, v, g_low, w_g2, b_g, g_out):
    B, S, _ = q.shape
    H, DK, DV, C = GLA_HEADS, GLA_DK, GLA_DV, GLA_CHUNK
    N = S // C
    f32 = jnp.float32

    def chunks(t, d):
        return t.astype(f32).reshape(B, N, C, H, d).transpose(1, 0, 3, 2, 4)

    qc = chunks(q, DK) * DK ** -0.5
    kc = chunks(k, DK)
    vc = chunks(v, DV)
    glog = jax.nn.log_sigmoid((g_low @ w_g2 + b_g).astype(f32)) / GLA_GATE_NORM
    bc = jnp.cumsum(chunks(glog, DK), axis=3)
    b_last = bc[:, :, :, -1:, :]
    q_dec = qc * jnp.exp(bc)
    k_inv = kc * jnp.exp(-bc)
    k_end = kc * jnp.exp(b_last - bc)
    decay = jnp.exp(b_last[:, :, :, 0, :])
    tril = jnp.arange(C)[:, None] >= jnp.arange(C)[None, :]
    attn = jnp.where(tril, jnp.einsum('nbhcd,nbhsd->nbhcs', q_dec, k_inv), 0.0)
    o_intra = jnp.einsum('nbhcs,nbhse->nbhce', attn, vc)

    def step(state, inp):
        qd, ke, vv, dec = inp
        o = jnp.einsum('bhcd,bhde->bhce', qd, state)
        state = dec[..., None] * state + jnp.einsum('bhcd,bhce->bhde', ke, vv)
        return state, o

    state0 = jnp.zeros((B, H, DK, DV), f32)
    _, o_inter = lax.scan(step, state0, (q_dec, k_end, vc, decay))
    o = (o_intra + o_inter).transpose(1, 0, 3, 2, 4).reshape(B, S, H, DV)
    o = rms_norm(o, g_out)
    return o.reshape(B, S, H * DV).astype(q.dtype)


def dsa_mixer(q, k, v, iq, ik, iw):
    B, S, _ = q.shape
    REP = DSA_HEADS // DSA_KV_HEADS
    top_k = min(IDX_TOPK, S // 4)
    q = q.reshape(B, S, DSA_KV_HEADS, REP, DSA_DH) * DSA_DH ** -0.5
    k = k.reshape(B, S, DSA_KV_HEADS, DSA_DH)
    v = v.reshape(B, S, DSA_KV_HEADS, DSA_DH)
    iq = iq.reshape(B, S, IDX_HEADS, IDX_DH) * IDX_DH ** -0.5
    iw = iw.astype(jnp.float32) * IDX_HEADS ** -0.5
    key_idx = jnp.arange(S)
    gather = jax.vmap(lambda tb, ib: jnp.take(tb, ib, axis=0))

    def block(i):
        start = i * Q_BLOCK
        q_idx = start + jnp.arange(Q_BLOCK)
        iqb = lax.dynamic_slice_in_dim(iq, start, Q_BLOCK, axis=1)
        iwb = lax.dynamic_slice_in_dim(iw, start, Q_BLOCK, axis=1)
        qb = lax.dynamic_slice_in_dim(q, start, Q_BLOCK, axis=1)
        rel = jax.nn.relu(jnp.einsum('bqhd,bkd->bqhk', iqb, ik, preferred_element_type=jnp.float32))
        score = jnp.einsum('bqhk,bqh->bqk', rel, iwb)
        causal = key_idx[None, :] <= q_idx[:, None]
        score = jnp.where(causal[None], score, NEG)
        _, idx = lax.top_k(score, top_k)
        valid = idx <= q_idx[None, :, None]
        ksel = gather(k, idx)
        vsel = gather(v, idx)
        s = jnp.einsum('bqgrd,bqkgd->bqgrk', qb, ksel, preferred_element_type=jnp.float32)
        p = jax.nn.softmax(jnp.where(valid[:, :, None, None, :], s, NEG), axis=-1)
        o = jnp.einsum('bqgrk,bqkgd->bqgrd', p.astype(vsel.dtype), vsel)
        return o.reshape(B, Q_BLOCK, DSA_WIDTH)

    out = lax.map(block, jnp.arange(S // Q_BLOCK))
    return out.transpose(1, 0, 2, 3).reshape(B, S, DSA_WIDTH)


def hybrid_layer(x, c, cos, sin, norm_g, w_ada, b_ada, w_in, mla_gq, mla_wuq, mla_gkv,
                 mla_wukv, gla_wg2, gla_bg, gla_gout, w_mg, b_mg, w_bm, w_bg, w_bd, w_o):
    mod = jax.nn.silu(c) @ w_ada + b_ada
    shift, scale, gate = jnp.split(mod, 3, axis=-1)
    h = rms_norm(x, norm_g) * (1.0 + scale[:, None, :]) + shift[:, None, :]

    proj = h @ w_in
    (cq, ckv, krope, gq, gk, gv, glow, dq, dk, dv, iq, ik, iw,
     z_mla, z_gla, z_dsa) = jnp.split(proj, list(np.cumsum(IN_SPLITS)[:-1]), axis=-1)

    y_mla = mla_mixer(cq, ckv, krope, cos, sin, mla_gq, mla_wuq, mla_gkv, mla_wukv) * jax.nn.silu(z_mla)
    y_gla = gla_mixer(gq, gk, gv, glow, gla_wg2, gla_bg, gla_gout) * jax.nn.silu(z_gla)
    y_dsa = dsa_mixer(dq, dk, dv, iq, ik, iw) * jax.nn.silu(z_dsa)

    g_a, g_b, g_c = jnp.split(jax.nn.sigmoid(h @ w_mg + b_mg), N_BRANCH, axis=-1)
    merged = g_a * (y_mla @ w_bm) + g_b * (y_gla @ w_bg) + g_c * (y_dsa @ w_bd)
    return x + gate[:, None, :] * (merged @ w_o)


def setup_inputs(seed: int = 0) -> dict:
    key = jax.random.key(seed)
    ks = jax.random.split(key, 24)
    f32 = jnp.float32
    D, L = D_MODEL, DEPTH

    def nrm(k, shape, fan_in):
        return jax.random.normal(k, shape, f32) * fan_in ** -0.5

    def gain(k, shape):
        return 1.0 + 0.1 * jax.random.normal(k, shape, f32)

    offset = jax.random.randint(ks[2], (BATCH, 1), 0, 1024, dtype=jnp.int32)
    positions = offset + jnp.arange(SEQ, dtype=jnp.int32)[None, :]
    return {
        "x": jax.random.normal(ks[0], (BATCH, SEQ, D), f32),
        "c": jax.random.normal(ks[1], (BATCH, D), f32),
        "positions": positions,
        "norm_g": gain(ks[3], (L, D)),
        "w_ada": nrm(ks[4], (L, D, 3 * D), D),
        "b_ada": 0.02 * jax.random.normal(ks[5], (L, 3 * D), f32),
        "w_in": nrm(ks[6], (L, D, IN_WIDTH), D),
        "mla_gq": gain(ks[7], (L, MLA_Q_RANK)),
        "mla_wuq": nrm(ks[8], (L, MLA_Q_RANK, MLA_HEADS * (MLA_NOPE + MLA_ROPE)), MLA_Q_RANK),
        "mla_gkv": gain(ks[9], (L, MLA_KV_RANK)),
        "mla_wukv": nrm(ks[10], (L, MLA_KV_RANK, MLA_HEADS * (MLA_NOPE + MLA_V)), MLA_KV_RANK),
        "gla_wg2": nrm(ks[11], (L, GLA_GATE_RANK, GLA_HEADS * GLA_DK), GLA_GATE_RANK),
        "gla_bg": 0.02 * jax.random.normal(ks[12], (L, GLA_HEADS * GLA_DK), f32),
        "gla_gout": gain(ks[13], (L, GLA_DV)),
        "w_mg": nrm(ks[14], (L, D, N_BRANCH * D), D),
        "b_mg": 0.02 * jax.random.normal(ks[15], (L, N_BRANCH * D), f32),
        "w_bm": nrm(ks[16], (L, MLA_WIDTH, D), MLA_WIDTH),
        "w_bg": nrm(ks[17], (L, GLA_WIDTH, D), GLA_WIDTH),
        "w_bd": nrm(ks[18], (L, DSA_WIDTH, D), DSA_WIDTH),
        "w_o": nrm(ks[19], (L, D, D), D),
        "final_g": gain(ks[20], (D,)),
    }


def reference(x, c, positions, norm_g, w_ada, b_ada, w_in, mla_gq, mla_wuq, mla_gkv, mla_wukv,
              gla_wg2, gla_bg, gla_gout, w_mg, b_mg, w_bm, w_bg, w_bd, w_o, final_g):
    cos, sin = rope_tables(positions, MLA_ROPE)
    for l in range(DEPTH):
        x = hybrid_layer(x, c, cos, sin, norm_g[l], w_ada[l], b_ada[l], w_in[l],
                         mla_gq[l], mla_wuq[l], mla_gkv[l], mla_wukv[l],
                         gla_wg2[l], gla_bg[l], gla_gout[l], w_mg[l], b_mg[l],
                         w_bm[l], w_bg[l], w_bd[l], w_o[l])
    return rms_norm(x, final_g)
```

```python
import functools

import jax
import jax.numpy as jnp
import numpy as np
from jax import lax
from jax.experimental import pallas as pl
from jax.experimental.pallas import tpu as pltpu

D_MODEL = 4096
DEPTH = 2
MLA_HEADS = 16
MLA_Q_RANK = 768
MLA_KV_RANK = 512
MLA_NOPE = 128
MLA_ROPE = 64
MLA_V = 128
ROPE_THETA = 10000.0
GLA_HEADS = 4
GLA_DK = 128
GLA_DV = 256
GLA_GATE_RANK = 16
GLA_GATE_NORM = 16.0
GLA_CHUNK = 64
DSA_HEADS = 8
DSA_KV_HEADS = 2
DSA_DH = 128
IDX_HEADS = 32
IDX_DH = 64
IDX_TOPK = 256
NORM_EPS = 1e-6
NEG = -1e30

MLA_WIDTH = MLA_HEADS * MLA_V
GLA_WIDTH = GLA_HEADS * GLA_DV
DSA_WIDTH = DSA_HEADS * DSA_DH
DSA_REP = DSA_HEADS // DSA_KV_HEADS

IN_SPLITS = (
    MLA_Q_RANK, MLA_KV_RANK, MLA_ROPE,
    GLA_HEADS * GLA_DK, GLA_HEADS * GLA_DK, GLA_WIDTH, GLA_GATE_RANK,
    DSA_WIDTH, DSA_KV_HEADS * DSA_DH, DSA_KV_HEADS * DSA_DH,
    IDX_HEADS * IDX_DH, IDX_DH, IDX_HEADS,
    MLA_WIDTH, GLA_WIDTH, DSA_WIDTH,
)
_IN_NAMES = ("cq", "ckv", "krope", "gq", "gk", "gv", "glow", "dq", "dk", "dv",
             "iq", "ik", "iw", "z_mla", "z_gla", "z_dsa")
_IN_OFF = dict(zip(_IN_NAMES, np.concatenate([[0], np.cumsum(IN_SPLITS)[:-1]]).tolist()))
_IN_W = dict(zip(_IN_NAMES, IN_SPLITS))

LANE = 128
MLA_QK_PAD = 256
ATT_TQ = 512
ATT_TK = 512
DSA_TQ = 128
VMEM_LIMIT = 56 * 1024 * 1024

_PM_LAYOUT = (("dq", 1024), ("gv", 1024), ("iq", 2048), ("gq", 512), ("cq", 768), ("dk", 256),
              ("ckv", 512), ("gk", 512), ("dv", 256), ("kr", 128), ("krr", 128), ("glow", 128),
              ("ik", 128), ("iw", 128), ("pad", 128))
_PM_OFF = {}
_o = 0
for _n, _w in _PM_LAYOUT:
    assert _o % _w == 0
    _PM_OFF[_n] = _o
    _o += _w
PM_WIDTH = _o

BF16 = jnp.bfloat16
F32 = jnp.float32
INT_MIN = -2 ** 31


def _cp(n_axes):
    return pltpu.CompilerParams(dimension_semantics=("arbitrary",) * n_axes,
                                vmem_limit_bytes=VMEM_LIMIT)


def _nt(a, b):
    return lax.dot_general(a, b, (((1,), (1,)), ((), ())), preferred_element_type=F32)


def _rms(x, g):
    return x * lax.rsqrt(jnp.mean(x * x, axis=-1, keepdims=True) + NORM_EPS) * g


def _mm_kernel(*refs, a_act, act, has_bias):
    if has_bias:
        a_ref, w_ref, b_ref, o_ref = refs
    else:
        a_ref, w_ref, o_ref = refs
    a = a_ref[...]
    if a_act == "silu":
        a = a.astype(F32)
        a = a * jax.nn.sigmoid(a)
    r = jnp.dot(a.astype(BF16), w_ref[...].astype(BF16), preferred_element_type=F32)
    if has_bias:
        r = r + b_ref[...]
    if act == "sigmoid":
        r = jax.nn.sigmoid(r)
    elif act == "silu":
        r = r * jax.nn.sigmoid(r)
    o_ref[...] = r.astype(o_ref.dtype)


def _mm(a, w, bias, *, tm, tn, out_dtype, a_act=None, act=None, name):
    m, k = a.shape
    n = w.shape[1]
    in_specs = [pl.BlockSpec((tm, k), lambda i, j: (i, 0)),
                pl.BlockSpec((k, tn), lambda i, j: (0, j))]
    args = [a, w]
    if bias is not None:
        in_specs.append(pl.BlockSpec((1, tn), lambda i, j: (0, j)))
        args.append(bias)
    return pl.pallas_call(
        functools.partial(_mm_kernel, a_act=a_act, act=act, has_bias=bias is not None),
        grid=(m // tm, n // tn),
        in_specs=in_specs,
        out_specs=pl.BlockSpec((tm, tn), lambda i, j: (i, j)),
        out_shape=jax.ShapeDtypeStruct((m, n), out_dtype),
        compiler_params=_cp(2),
        name=name,
    )(*args)


def _norm_mod_kernel(x_ref, g_ref, sc_ref, sh_ref, o_ref):
    y = _rms(x_ref[0], g_ref[...])
    o_ref[0] = (y * (1.0 + sc_ref[0]) + sh_ref[0]).astype(o_ref.dtype)


def _norm_mod(x, g, scale, shift, *, tm=256):
    b, s, d = x.shape
    return pl.pallas_call(
        _norm_mod_kernel,
        grid=(b, s // tm),
        in_specs=[pl.BlockSpec((1, tm, d), lambda bi, i: (bi, i, 0)),
                  pl.BlockSpec((1, d), lambda bi, i: (0, 0)),
                  pl.BlockSpec((1, 1, d), lambda bi, i: (bi, 0, 0)),
                  pl.BlockSpec((1, 1, d), lambda bi, i: (bi, 0, 0))],
        out_specs=pl.BlockSpec((1, tm, d), lambda bi, i: (bi, i, 0)),
        out_shape=jax.ShapeDtypeStruct((b, s, d), BF16),
        compiler_params=_cp(2),
        name="norm_mod",
    )(x, g, scale, shift)


def _final_norm_kernel(x_ref, g_ref, o_ref):
    o_ref[0] = _rms(x_ref[0], g_ref[...])


def _final_norm(x, g, *, tm=256):
    b, s, d = x.shape
    return pl.pallas_call(
        _final_norm_kernel,
        grid=(b, s // tm),
        in_specs=[pl.BlockSpec((1, tm, d), lambda bi, i: (bi, i, 0)),
                  pl.BlockSpec((1, d), lambda bi, i: (0, 0))],
        out_specs=pl.BlockSpec((1, tm, d), lambda bi, i: (bi, i, 0)),
        out_shape=jax.ShapeDtypeStruct((b, s, d), F32),
        compiler_params=_cp(2),
        name="final_norm",
    )(x, g)


def _rope_kernel(pos_ref, inv_ref, cos_ref, sin_ref):
    ang = pos_ref[0] * inv_ref[...]
    live = lax.broadcasted_iota(jnp.int32, ang.shape, 1) < MLA_ROPE
    cos_ref[0] = jnp.where(live, jnp.cos(ang), 0.0)
    sin_ref[0] = jnp.where(live, jnp.sin(ang), 0.0)


def _rope_tables(positions, *, tm=512):
    b, s = positions.shape
    inv = ROPE_THETA ** (-jnp.arange(0, MLA_ROPE, 2, dtype=F32) / MLA_ROPE)
    inv128 = jnp.concatenate([inv, inv, jnp.zeros((LANE - MLA_ROPE,), F32)])[None, :]
    pos128 = jnp.broadcast_to(positions.astype(F32)[:, :, None], (b, s, LANE))
    spec = pl.BlockSpec((1, tm, LANE), lambda bi, i: (bi, i, 0))
    return pl.pallas_call(
        _rope_kernel,
        grid=(b, s // tm),
        in_specs=[spec, pl.BlockSpec((1, LANE), lambda bi, i: (0, 0))],
        out_specs=[spec, spec],
        out_shape=[jax.ShapeDtypeStruct((b, s, LANE), F32)] * 2,
        compiler_params=_cp(2),
        name="rope_tables",
    )(pos128, inv128)


def _mla_q_kernel(cq_ref, g_ref, w1_ref, w2_ref, cos_ref, sin_ref, o_ref, an_ref, *, scale):
    @pl.when(pl.program_id(2) == 0)
    def _():
        an_ref[...] = _rms(cq_ref[0].astype(F32), g_ref[...]).astype(BF16)

    a = an_ref[...]
    a1 = jnp.dot(a, w1_ref[...], preferred_element_type=F32)
    a2 = jnp.dot(a, w2_ref[...], preferred_element_type=F32)
    rope = a1[:, MLA_NOPE:] * cos_ref[0] + a2 * sin_ref[0]
    o_ref[0, :, :MLA_NOPE] = (a1[:, :MLA_NOPE] * scale).astype(BF16)
    o_ref[0, :, MLA_NOPE:] = (rope * scale).astype(BF16)


def _mla_q(pm3, g_q, w1, w2, cos128, sin128, *, tm=512):
    b, s, _ = pm3.shape
    cq_blk = _PM_OFF["cq"] // MLA_Q_RANK
    return pl.pallas_call(
        functools.partial(_mla_q_kernel, scale=(MLA_NOPE + MLA_ROPE) ** -0.5),
        grid=(b, s // tm, MLA_HEADS),
        in_specs=[pl.BlockSpec((1, tm, MLA_Q_RANK), lambda bi, i, h: (bi, i, cq_blk)),
                  pl.BlockSpec((1, MLA_Q_RANK), lambda bi, i, h: (0, 0)),
                  pl.BlockSpec((MLA_Q_RANK, MLA_QK_PAD), lambda bi, i, h: (0, h)),
                  pl.BlockSpec((MLA_Q_RANK, LANE), lambda bi, i, h: (0, h)),
                  pl.BlockSpec((1, tm, LANE), lambda bi, i, h: (bi, i, 0)),
                  pl.BlockSpec((1, tm, LANE), lambda bi, i, h: (bi, i, 0))],
        out_specs=pl.BlockSpec((1, tm, MLA_QK_PAD), lambda bi, i, h: (bi, i, h)),
        out_shape=jax.ShapeDtypeStruct((b, s, MLA_HEADS * MLA_QK_PAD), BF16),
        scratch_shapes=[pltpu.VMEM((tm, MLA_Q_RANK), BF16)],
        compiler_params=_cp(3),
        name="mla_q",
    )(pm3, g_q, w1, w2, cos128, sin128)


def _mla_kv_kernel(ckv_ref, g_ref, w_ref, kr_ref, krr_ref, cos_ref, sin_ref, k_ref, vt_ref, an_ref):
    @pl.when(pl.program_id(2) == 0)
    def _():
        an_ref[...] = _rms(ckv_ref[0].astype(F32), g_ref[...]).astype(BF16)

    acc = jnp.dot(an_ref[...], w_ref[...], preferred_element_type=F32)
    kr = kr_ref[0].astype(F32) * cos_ref[0] + krr_ref[0].astype(F32) * sin_ref[0]
    k_ref[0, 0, 0, :, :MLA_NOPE] = acc[:, :MLA_NOPE].astype(BF16)
    k_ref[0, 0, 0, :, MLA_NOPE:] = kr.astype(BF16)
    vt_ref[0, 0, 0] = acc[:, MLA_NOPE:].T.astype(BF16)


def _mla_kv(pm3, g_kv, w_ukv, cos128, sin128):
    b, s, _ = pm3.shape
    tm = ATT_TK
    ckv_blk = _PM_OFF["ckv"] // MLA_KV_RANK
    kr_blk = _PM_OFF["kr"] // LANE
    krr_blk = _PM_OFF["krr"] // LANE
    return pl.pallas_call(
        _mla_kv_kernel,
        grid=(b, s // tm, MLA_HEADS),
        in_specs=[pl.BlockSpec((1, tm, MLA_KV_RANK), lambda bi, i, h: (bi, i, ckv_blk)),
                  pl.BlockSpec((1, MLA_KV_RANK), lambda bi, i, h: (0, 0)),
                  pl.BlockSpec((MLA_KV_RANK, MLA_NOPE + MLA_V), lambda bi, i, h: (0, h)),
                  pl.BlockSpec((1, tm, LANE), lambda bi, i, h: (bi, i, kr_blk)),
                  pl.BlockSpec((1, tm, LANE), lambda bi, i, h: (bi, i, krr_blk)),
                  pl.BlockSpec((1, tm, LANE), lambda bi, i, h: (bi, i, 0)),
                  pl.BlockSpec((1, tm, LANE), lambda bi, i, h: (bi, i, 0))],
        out_specs=[pl.BlockSpec((1, 1, 1, tm, MLA_QK_PAD), lambda bi, i, h: (bi, h, i, 0, 0)),
                   pl.BlockSpec((1, 1, 1, MLA_V, tm), lambda bi, i, h: (bi, h, i, 0, 0))],
        out_shape=[jax.ShapeDtypeStruct((b, MLA_HEADS, s // tm, tm, MLA_QK_PAD), BF16),
                   jax.ShapeDtypeStruct((b, MLA_HEADS, s // tm, MLA_V, tm), BF16)],
        scratch_shapes=[pltpu.VMEM((tm, MLA_KV_RANK), BF16)],
        compiler_params=_cp(3),
        name="mla_kv",
    )(pm3, g_kv, w_ukv, pm3, pm3, cos128, sin128)


def _softmax_step(s, vt, acc_ref, m_ref, l_ref):
    m_old = m_ref[...]
    m_new = jnp.maximum(m_old, jnp.max(s, axis=0, keepdims=True))
    alpha = jnp.exp(m_old - m_new)
    p = jnp.exp(s - m_new)
    l_ref[...] = alpha * l_ref[...] + jnp.sum(p, axis=0, keepdims=True)
    acc_ref[...] = alpha * acc_ref[...] + jnp.dot(vt, p.astype(BF16), preferred_element_type=F32)
    m_ref[...] = m_new


def _softmax_init(acc_ref, m_ref, l_ref):
    m_ref[...] = jnp.full(m_ref.shape, NEG, F32)
    l_ref[...] = jnp.zeros(l_ref.shape, F32)
    acc_ref[...] = jnp.zeros(acc_ref.shape, F32)


def _mla_attn_kernel(q_ref, k_ref, vt_ref, z_ref, o_ref, acc_ref, m_ref, l_ref):
    i = pl.program_id(2)
    q = q_ref[0]
    _softmax_init(acc_ref, m_ref, l_ref)

    def full_block(j, carry):
        _softmax_step(_nt(k_ref[0, 0, j], q), vt_ref[0, 0, j], acc_ref, m_ref, l_ref)
        return carry

    lax.fori_loop(0, i, full_block, 0)
    s = _nt(k_ref[0, 0, i], q)
    kpos = lax.broadcasted_iota(jnp.int32, s.shape, 0)
    qpos = lax.broadcasted_iota(jnp.int32, s.shape, 1)
    _softmax_step(jnp.where(kpos <= qpos, s, NEG), vt_ref[0, 0, i], acc_ref, m_ref, l_ref)

    out = (acc_ref[...] / l_ref[...]).T
    o_ref[0] = (out * z_ref[0].astype(F32)).astype(o_ref.dtype)


def _mla_attn(q, kcat, vt, pz3):
    b, s, _ = q.shape
    assert ATT_TQ == ATT_TK
    nk = s // ATT_TK
    return pl.pallas_call(
        _mla_attn_kernel,
        grid=(b, MLA_HEADS, s // ATT_TQ),
        in_specs=[pl.BlockSpec((1, ATT_TQ, MLA_QK_PAD), lambda bi, h, i: (bi, i, h)),
                  pl.BlockSpec((1, 1, nk, ATT_TK, MLA_QK_PAD), lambda bi, h, i: (bi, h, 0, 0, 0)),
                  pl.BlockSpec((1, 1, nk, MLA_V, ATT_TK), lambda bi, h, i: (bi, h, 0, 0, 0)),
                  pl.BlockSpec((1, ATT_TQ, MLA_V), lambda bi, h, i: (bi, i, h))],
        out_specs=pl.BlockSpec((1, ATT_TQ, MLA_V), lambda bi, h, i: (bi, i, h)),
        out_shape=jax.ShapeDtypeStruct((b, s, MLA_WIDTH), BF16),
        scratch_shapes=[pltpu.VMEM((MLA_V, ATT_TQ), F32),
                        pltpu.VMEM((1, ATT_TQ), F32),
                        pltpu.VMEM((1, ATT_TQ), F32)],
        compiler_params=_cp(3),
        name="mla_attn",
    )(q, kcat, vt, pz3)


def _log_sigmoid(x):
    return jnp.minimum(x, 0.0) - jnp.log(1.0 + jnp.exp(-jnp.abs(x)))


def _gla_kernel(q_ref, k_ref, v_ref, gl_ref, z_ref, w2_ref, bg_ref, go_ref, o_ref, st_ref, *, nb):
    @pl.when(pl.program_id(0) == 0)
    def _():
        st_ref[...] = jnp.zeros(st_ref.shape, F32)

    c = GLA_CHUNK
    row = lax.broadcasted_iota(jnp.int32, (c, c), 0)
    col = lax.broadcasted_iota(jnp.int32, (c, c), 1)
    tril = row >= col
    tril_b = jnp.where(tril, 1.0, 0.0).astype(BF16)
    for b in range(nb):
        gl = gl_ref[b]
        for h in range(GLA_HEADS):
            ks = slice(h * GLA_DK, (h + 1) * GLA_DK)
            vs = slice(h * GLA_DV, (h + 1) * GLA_DV)
            q = q_ref[b, :, ks].astype(F32)
            k = k_ref[b, :, ks].astype(F32)
            v = v_ref[b, :, vs]
            pre = jnp.dot(gl, w2_ref[:, ks], preferred_element_type=F32) + bg_ref[:, ks]
            glog = _log_sigmoid(pre) / GLA_GATE_NORM
            g1 = glog.astype(BF16)
            r1 = glog - g1.astype(F32)
            g2 = r1.astype(BF16)
            g3 = (r1 - g2.astype(F32)).astype(BF16)
            bc = (jnp.dot(tril_b, g1, preferred_element_type=F32)
                  + jnp.dot(tril_b, g2, preferred_element_type=F32)
                  + jnp.dot(tril_b, g3, preferred_element_type=F32))
            b_last = bc[c - 1:c, :]
            q_dec = (q * GLA_DK ** -0.5 * jnp.exp(bc)).astype(BF16)
            k_inv = (k * jnp.exp(-bc)).astype(BF16)
            k_end = (k * jnp.exp(b_last - bc)).astype(BF16)
            decay = jnp.exp(b_last)
            attn = jnp.where(tril, _nt(q_dec, k_inv), 0.0).astype(BF16)
            st = st_ref[b * GLA_HEADS + h]
            o = jnp.dot(attn, v, preferred_element_type=F32) + _nt(q_dec, st.astype(BF16))
            upd = lax.dot_general(v, k_end, (((0,), (0,)), ((), ())), preferred_element_type=F32)
            st_ref[b * GLA_HEADS + h] = st * decay + upd
            on = _rms(o, go_ref[...])
            o_ref[b, :, vs] = (on * z_ref[b, :, vs].astype(F32)).astype(o_ref.dtype)


def _gla(pm3, pz3, w2p, bg, gout):
    b, s, _ = pm3.shape
    c = GLA_CHUNK
    qw = GLA_HEADS * GLA_DK
    q_blk, k_blk = _PM_OFF["gq"] // qw, _PM_OFF["gk"] // qw
    v_blk, gl_blk = _PM_OFF["gv"] // GLA_WIDTH, _PM_OFF["glow"] // LANE
    z_blk = MLA_WIDTH // GLA_WIDTH
    return pl.pallas_call(
        functools.partial(_gla_kernel, nb=b),
        grid=(s // c,),
        in_specs=[pl.BlockSpec((b, c, qw), lambda i: (0, i, q_blk)),
                  pl.BlockSpec((b, c, qw), lambda i: (0, i, k_blk)),
                  pl.BlockSpec((b, c, GLA_WIDTH), lambda i: (0, i, v_blk)),
                  pl.BlockSpec((b, c, LANE), lambda i: (0, i, gl_blk)),
                  pl.BlockSpec((b, c, GLA_WIDTH), lambda i: (0, i, z_blk)),
                  pl.BlockSpec((LANE, qw), lambda i: (0, 0)),
                  pl.BlockSpec((1, qw), lambda i: (0, 0)),
                  pl.BlockSpec((1, GLA_DV), lambda i: (0, 0))],
        out_specs=pl.BlockSpec((b, c, GLA_WIDTH), lambda i: (0, i, 0)),
        out_shape=jax.ShapeDtypeStruct((b, s, GLA_WIDTH), BF16),
        scratch_shapes=[pltpu.VMEM((b * GLA_HEADS, GLA_DV, GLA_DK), F32)],
        compiler_params=_cp(1),
        name="gla",
    )(pm3, pm3, pm3, pm3, pz3, w2p, bg, gout)


def _dsa_kernel(iq_ref, iw_ref, ik_ref, q_ref, k_ref, vt_ref, z_ref, o_ref,
                key_ref, acc_ref, m_ref, l_ref, *, top_k):
    tq, tk = DSA_TQ, ATT_TK
    i = pl.program_id(1)
    nkb = (i * tq + tq - 1) // tk + 1
    w = iw_ref[0] * (IDX_HEADS ** -0.5 * IDX_DH ** -0.5)
    qpos = i * tq + lax.broadcasted_iota(jnp.int32, (tk, tq), 1)
    heads_per_dot = 4

    def score_block(j, carry):
        ikb = ik_ref[0, j]
        sc = jnp.zeros((tk, tq), F32)
        for hg in range(IDX_HEADS // heads_per_dot):
            iq = iq_ref[0, hg * heads_per_dot:(hg + 1) * heads_per_dot].reshape(heads_per_dot * tq, IDX_DH)
            r = _nt(ikb, iq)
            for hh in range(heads_per_dot):
                h = hg * heads_per_dot + hh
                sc = sc + jnp.maximum(r[:, hh * tq:(hh + 1) * tq], 0.0) * w[h:h + 1, :]
        bits = lax.bitcast_convert_type(sc, jnp.int32)
        okey = bits ^ ((bits >> 31) & 0x7FFFFFFF)
        kpos = j * tk + lax.broadcasted_iota(jnp.int32, (tk, tq), 0)
        key_ref[j] = jnp.where(kpos <= qpos, okey, INT_MIN)
        return carry

    lax.fori_loop(0, nkb, score_block, 0)

    def count_ge(cand):
        def body(j, acc):
            ge = jnp.where(key_ref[j] >= cand, 1, 0).astype(jnp.int32)
            return acc + ge.reshape(tk // 8, 8, tq).sum(axis=0)
        acc = lax.fori_loop(0, nkb, body, jnp.zeros((8, tq), jnp.int32))
        return acc.sum(axis=0, keepdims=True)

    zero = jnp.zeros((1, tq), jnp.int32)
    thr = jnp.where(count_ge(zero) >= top_k, zero, INT_MIN)

    def bit_step(it, thr):
        cand = thr | lax.shift_left(jnp.int32(1), 30 - it)
        return jnp.where(count_ge(cand) >= top_k, cand, thr)

    thr = lax.fori_loop(0, 31, bit_step, thr)
    thr = jnp.maximum(thr, INT_MIN + 1)

    scale = DSA_DH ** -0.5
    for g in range(DSA_KV_HEADS):
        qg = jnp.concatenate(
            [q_ref[0, :, (g * DSA_REP + r) * DSA_DH:(g * DSA_REP + r + 1) * DSA_DH] for r in range(DSA_REP)],
            axis=0)
        qg = (qg.astype(F32) * scale).astype(BF16)
        _softmax_init(acc_ref, m_ref, l_ref)

        def attn_block(j, carry):
            s = _nt(k_ref[0, j, :, g * DSA_DH:(g + 1) * DSA_DH], qg)
            sel = key_ref[j] >= thr
            s = jnp.concatenate(
                [jnp.where(sel, s[:, r * tq:(r + 1) * tq], NEG) for r in range(DSA_REP)], axis=1)
            _softmax_step(s, vt_ref[0, g, j], acc_ref, m_ref, l_ref)
            return carry

        lax.fori_loop(0, nkb, attn_block, 0)
        out_t = acc_ref[...] / l_ref[...]
        for r in range(DSA_REP):
            cs = slice((g * DSA_REP + r) * DSA_DH, (g * DSA_REP + r + 1) * DSA_DH)
            o_ref[0, :, cs] = (out_t[:, r * tq:(r + 1) * tq].T * z_ref[0, :, cs].astype(F32)).astype(o_ref.dtype)


def _dsa(pm3, pz3, iq_t, iw_t, ik4, dk4, dvt, top_k):
    b, s, _ = pm3.shape
    tq, tk = DSA_TQ, ATT_TK
    nk = s // tk
    q_blk = _PM_OFF["dq"] // DSA_WIDTH
    z_blk = (MLA_WIDTH + GLA_WIDTH) // DSA_WIDTH
    kvw = DSA_KV_HEADS * DSA_DH
    return pl.pallas_call(
        functools.partial(_dsa_kernel, top_k=top_k),
        grid=(b, s // tq),
        in_specs=[pl.BlockSpec((1, IDX_HEADS, tq, IDX_DH), lambda bi, i: (bi, 0, i, 0)),
                  pl.BlockSpec((1, IDX_HEADS, tq), lambda bi, i: (bi, 0, i)),
                  pl.BlockSpec((1, nk, tk, IDX_DH), lambda bi, i: (bi, 0, 0, 0)),
                  pl.BlockSpec((1, tq, DSA_WIDTH), lambda bi, i: (bi, i, q_blk)),
                  pl.BlockSpec((1, nk, tk, kvw), lambda bi, i: (bi, 0, 0, 0)),
                  pl.BlockSpec((1, DSA_KV_HEADS, nk, DSA_DH, tk), lambda bi, i: (bi, 0, 0, 0, 0)),
                  pl.BlockSpec((1, tq, DSA_WIDTH), lambda bi, i: (bi, i, z_blk))],
        out_specs=pl.BlockSpec((1, tq, DSA_WIDTH), lambda bi, i: (bi, i, 0)),
        out_shape=jax.ShapeDtypeStruct((b, s, DSA_WIDTH), BF16),
        scratch_shapes=[pltpu.VMEM((nk, tk, tq), jnp.int32),
                        pltpu.VMEM((DSA_DH, DSA_REP * tq), F32),
                        pltpu.VMEM((1, DSA_REP * tq), F32),
                        pltpu.VMEM((1, DSA_REP * tq), F32)],
        compiler_params=_cp(2),
        name="dsa",
    )(iq_t, iw_t, ik4, pm3, dk4, dvt, pz3)


def _lift_kernel(ym_ref, yg_ref, yd_ref, wm_ref, wg_ref, wd_ref, ga_ref, gb_ref, gc_ref, o_ref):
    a = jnp.dot(ym_ref[...], wm_ref[...], preferred_element_type=F32)
    b = jnp.dot(yg_ref[...], wg_ref[...], preferred_element_type=F32)
    c = jnp.dot(yd_ref[...], wd_ref[...], preferred_element_type=F32)
    o = ga_ref[...].astype(F32) * a + gb_ref[...].astype(F32) * b + gc_ref[...].astype(F32) * c
    o_ref[...] = o.astype(o_ref.dtype)


def _lift(y_mla, y_gla, y_dsa, w_bm, w_bg, w_bd, gates, *, tm=512, tn=512):
    t = y_mla.shape[0]
    d = w_bm.shape[1]
    nj = d // tn
    row = lambda width: pl.BlockSpec((tm, width), lambda i, j: (i, 0))
    wcol = lambda depth: pl.BlockSpec((depth, tn), lambda i, j: (0, j))
    gate = lambda br: pl.BlockSpec((tm, tn), lambda i, j: (i, br * nj + j))
    return pl.pallas_call(
        _lift_kernel,
        grid=(t // tm, nj),
        in_specs=[row(MLA_WIDTH), row(GLA_WIDTH), row(DSA_WIDTH),
                  wcol(MLA_WIDTH), wcol(GLA_WIDTH), wcol(DSA_WIDTH),
                  gate(0), gate(1), gate(2)],
        out_specs=pl.BlockSpec((tm, tn), lambda i, j: (i, j)),
        out_shape=jax.ShapeDtypeStruct((t, d), BF16),
        compiler_params=_cp(2),
        name="lift_merge",
    )(y_mla, y_gla, y_dsa, w_bm, w_bg, w_bd, gates, gates, gates)


def _out_kernel(m_ref, w_ref, x_ref, gate_ref, o_ref):
    r = jnp.dot(m_ref[0], w_ref[...], preferred_element_type=F32)
    o_ref[0] = x_ref[0] + gate_ref[0] * r


def _out_proj(merged3, w_o, x, gate, *, tm=1024, tn=512):
    b, s, d = x.shape
    tm = min(tm, s)
    return pl.pallas_call(
        _out_kernel,
        grid=(b, s // tm, d // tn),
        in_specs=[pl.BlockSpec((1, tm, d), lambda bi, i, j: (bi, i, 0)),
                  pl.BlockSpec((d, tn), lambda bi, i, j: (0, j)),
                  pl.BlockSpec((1, tm, tn), lambda bi, i, j: (bi, i, j)),
                  pl.BlockSpec((1, 1, tn), lambda bi, i, j: (bi, 0, j))],
        out_specs=pl.BlockSpec((1, tm, tn), lambda bi, i, j: (bi, i, j)),
        out_shape=jax.ShapeDtypeStruct((b, s, d), F32),
        compiler_params=_cp(3),
        name="out_proj",
    )(merged3, w_o, x, gate)


def _main_weight(w_in):
    d = w_in.shape[0]

    def cols(name):
        return w_in[:, _IN_OFF[name]:_IN_OFF[name] + _IN_W[name]]

    def padded(w, width):
        return jnp.concatenate([w, jnp.zeros((d, width - w.shape[1]), w.dtype)], axis=1)

    kr = cols("krope")
    half = MLA_ROPE // 2
    parts = {"kr": padded(kr, LANE),
             "krr": padded(jnp.concatenate([-kr[:, half:], kr[:, :half]], axis=1), LANE),
             "glow": padded(cols("glow"), LANE), "ik": padded(cols("ik"), LANE),
             "iw": padded(cols("iw"), LANE), "pad": jnp.zeros((d, LANE), w_in.dtype)}
    return jnp.concatenate([parts[n] if n in parts else cols(n) for n, _ in _PM_LAYOUT], axis=1).astype(BF16)


def _mla_q_weights(w_uq):
    r = w_uq.shape[0]
    w = w_uq.reshape(r, MLA_HEADS, MLA_NOPE + MLA_ROPE)
    nope, rope = w[..., :MLA_NOPE], w[..., MLA_NOPE:]
    half = MLA_ROPE // 2
    zeros = jnp.zeros((r, MLA_HEADS, LANE - MLA_ROPE), w.dtype)
    w1 = jnp.concatenate([nope, rope, zeros], axis=-1).reshape(r, MLA_HEADS * MLA_QK_PAD)
    rot = jnp.concatenate([-rope[..., half:], rope[..., :half]], axis=-1)
    w2 = jnp.concatenate([rot, zeros], axis=-1).reshape(r, MLA_HEADS * LANE)
    return w1.astype(BF16), w2.astype(BF16)


def _layer(x, c_pad, cos128, sin128, norm_g, w_ada, b_ada, w_in, mla_gq, mla_wuq, mla_gkv, mla_wukv,
           gla_wg2, gla_bg, gla_gout, w_mg, b_mg, w_bm, w_bg, w_bd, w_o):
    b, s, d = x.shape
    t = b * s
    mod = _mm(c_pad, w_ada, b_ada[None, :], tm=c_pad.shape[0], tn=512, out_dtype=F32,
              a_act="silu", name="ada")[:b]
    shift, scale, gate = (mod[:, None, k * d:(k + 1) * d] for k in range(3))
    h = _norm_mod(x, norm_g[None, :], scale, shift)
    h2 = h.reshape(t, d)

    mm_tm = min(1024, t)
    pm = _mm(h2, _main_weight(w_in), None, tm=mm_tm, tn=512, out_dtype=BF16, name="proj_main")
    z0 = _IN_OFF["z_mla"]
    pz = _mm(h2, w_in[:, z0:].astype(BF16), None, tm=mm_tm, tn=512, out_dtype=BF16, act="silu",
             name="proj_gate_paths")
    gates = _mm(h2, w_mg.astype(BF16), b_mg[None, :], tm=mm_tm, tn=512, out_dtype=BF16,
                act="sigmoid", name="merge_gates")
    pm3 = pm.reshape(b, s, PM_WIDTH)
    pz3 = pz.reshape(b, s, d)

    w1, w2 = _mla_q_weights(mla_wuq)
    q = _mla_q(pm3, mla_gq[None, :], w1, w2, cos128, sin128)
    kcat, vt = _mla_kv(pm3, mla_gkv[None, :], mla_wukv.astype(BF16), cos128, sin128)
    y_mla = _mla_attn(q, kcat, vt, pz3)

    w2p = jnp.concatenate([gla_wg2, jnp.zeros((LANE - GLA_GATE_RANK, gla_wg2.shape[1]), gla_wg2.dtype)],
                          axis=0).astype(BF16)
    y_gla = _gla(pm3, pz3, w2p, gla_bg[None, :], gla_gout[None, :])

    nk = s // ATT_TK

    def piece(name, width):
        return pm3[:, :, _PM_OFF[name]:_PM_OFF[name] + width]

    iq_t = piece("iq", IDX_HEADS * IDX_DH).reshape(b, s, IDX_HEADS, IDX_DH).transpose(0, 2, 1, 3)
    iw_t = piece("iw", IDX_HEADS).astype(F32).transpose(0, 2, 1)
    ik4 = piece("ik", IDX_DH).reshape(b, nk, ATT_TK, IDX_DH)
    dk4 = piece("dk", DSA_KV_HEADS * DSA_DH).reshape(b, nk, ATT_TK, DSA_KV_HEADS * DSA_DH)
    dvt = piece("dv", DSA_KV_HEADS * DSA_DH).reshape(b, nk, ATT_TK, DSA_KV_HEADS, DSA_DH).transpose(0, 3, 1, 4, 2)
    y_dsa = _dsa(pm3, pz3, iq_t, iw_t, ik4, dk4, dvt, min(IDX_TOPK, s // 4))

    merged = _lift(y_mla.reshape(t, MLA_WIDTH), y_gla.reshape(t, GLA_WIDTH), y_dsa.reshape(t, DSA_WIDTH),
                   w_bm.astype(BF16), w_bg.astype(BF16), w_bd.astype(BF16), gates)
    return _out_proj(merged.reshape(b, s, d), w_o.astype(BF16), x, gate)


def kernel(x, c, positions, norm_g, w_ada, b_ada, w_in, mla_gq, mla_wuq, mla_gkv, mla_wukv, gla_wg2,
           gla_bg, gla_gout, w_mg, b_mg, w_bm, w_bg, w_bd, w_o, final_g):
    b = x.shape[0]
    cos128, sin128 = _rope_tables(positions)
    c_pad = jnp.concatenate([c, jnp.zeros((8 - b, c.shape[1]), c.dtype)], axis=0)
    for l in range(DEPTH):
        x = _layer(x, c_pad, cos128, sin128, norm_g[l], w_ada[l], b_ada[l], w_in[l], mla_gq[l],
                   mla_wuq[l], mla_gkv[l], mla_wukv[l], gla_wg2[l], gla_bg[l], gla_gout[l],
                   w_mg[l], b_mg[l], w_bm[l], w_bg[l], w_bd[l], w_o[l])
    return _final_norm(x, final_g[None, :])
```

```python
import functools

import jax
import jax.numpy as jnp
import numpy as np
from jax import lax
from jax.experimental import pallas as pl
from jax.experimental.pallas import tpu as pltpu

D_MODEL = 4096
DEPTH = 2
MLA_HEADS = 16
MLA_Q_RANK = 768
MLA_KV_RANK = 512
MLA_NOPE = 128
MLA_ROPE = 64
MLA_V = 128
ROPE_THETA = 10000.0
GLA_HEADS = 4
GLA_DK = 128
GLA_DV = 256
GLA_GATE_RANK = 16
GLA_GATE_NORM = 16.0
GLA_CHUNK = 64
DSA_HEADS = 8
DSA_KV_HEADS = 2
DSA_DH = 128
IDX_HEADS = 32
IDX_DH = 64
IDX_TOPK = 256
NORM_EPS = 1e-6
NEG = -1e30

MLA_WIDTH = MLA_HEADS * MLA_V
GLA_WIDTH = GLA_HEADS * GLA_DV
DSA_WIDTH = DSA_HEADS * DSA_DH
DSA_REP = DSA_HEADS // DSA_KV_HEADS

IN_SPLITS = (
    MLA_Q_RANK, MLA_KV_RANK, MLA_ROPE,
    GLA_HEADS * GLA_DK, GLA_HEADS * GLA_DK, GLA_WIDTH, GLA_GATE_RANK,
    DSA_WIDTH, DSA_KV_HEADS * DSA_DH, DSA_KV_HEADS * DSA_DH,
    IDX_HEADS * IDX_DH, IDX_DH, IDX_HEADS,
    MLA_WIDTH, GLA_WIDTH, DSA_WIDTH,
)
_IN_NAMES = ("cq", "ckv", "krope", "gq", "gk", "gv", "glow", "dq", "dk", "dv",
             "iq", "ik", "iw", "z_mla", "z_gla", "z_dsa")
_IN_OFF = dict(zip(_IN_NAMES, np.concatenate([[0], np.cumsum(IN_SPLITS)[:-1]]).tolist()))
_IN_W = dict(zip(_IN_NAMES, IN_SPLITS))

LANE = 128
MLA_QK_PAD = 256
ATT_TQ = 1024
ATT_TK = 512
LOG2E = 1.4426950408889634
DSA_TQ = 128
VMEM_LIMIT = 56 * 1024 * 1024

_PM_LAYOUT = (("dq", 1024), ("gv", 1024), ("iq", 2048), ("gq", 512), ("cq", 768), ("dk", 256),
              ("ckv", 512), ("gk", 512), ("dv", 256), ("kr", 128), ("krr", 128), ("glow", 128),
              ("ik", 128), ("iw", 128), ("pad", 128))
_PM_OFF = {}
_o = 0
for _n, _w in _PM_LAYOUT:
    assert _o % _w == 0
    _PM_OFF[_n] = _o
    _o += _w
PM_WIDTH = _o

BF16 = jnp.bfloat16
F32 = jnp.float32
INT_MIN = -2 ** 31


def _cp(n_axes, flags=None):
    return pltpu.CompilerParams(dimension_semantics=("arbitrary",) * n_axes,
                                vmem_limit_bytes=VMEM_LIMIT, flags=flags)


def _nt(a, b):
    return lax.dot_general(a, b, (((1,), (1,)), ((), ())), preferred_element_type=F32)


def _rms(x, g):
    return x * lax.rsqrt(jnp.mean(x * x, axis=-1, keepdims=True) + NORM_EPS) * g


def _mm_kernel(*refs, a_act, act, has_bias):
    if has_bias:
        a_ref, w_ref, b_ref, o_ref = refs
    else:
        a_ref, w_ref, o_ref = refs
    a = a_ref[...]
    if a_act == "silu":
        a = a.astype(F32)
        a = a * jax.nn.sigmoid(a)
    r = jnp.dot(a.astype(BF16), w_ref[...].astype(BF16), preferred_element_type=F32)
    if has_bias:
        r = r + b_ref[...]
    if act == "sigmoid":
        r = jax.nn.sigmoid(r)
    elif act == "silu":
        r = r * jax.nn.sigmoid(r)
    o_ref[...] = r.astype(o_ref.dtype)


def _mm(a, w, bias, *, tm, tn, out_dtype, a_act=None, act=None, name):
    m, k = a.shape
    n = w.shape[1]
    in_specs = [pl.BlockSpec((tm, k), lambda i, j: (i, 0)),
                pl.BlockSpec((k, tn), lambda i, j: (0, j))]
    args = [a, w]
    if bias is not None:
        in_specs.append(pl.BlockSpec((1, tn), lambda i, j: (0, j)))
        args.append(bias)
    return pl.pallas_call(
        functools.partial(_mm_kernel, a_act=a_act, act=act, has_bias=bias is not None),
        grid=(m // tm, n // tn),
        in_specs=in_specs,
        out_specs=pl.BlockSpec((tm, tn), lambda i, j: (i, j)),
        out_shape=jax.ShapeDtypeStruct((m, n), out_dtype),
        compiler_params=_cp(2),
        name=name,
    )(*args)


def _norm_mod_kernel(x_ref, g_ref, sc_ref, sh_ref, o_ref):
    y = _rms(x_ref[0], g_ref[...])
    o_ref[0] = (y * (1.0 + sc_ref[0]) + sh_ref[0]).astype(o_ref.dtype)


def _norm_mod(x, g, scale, shift, *, tm=256):
    b, s, d = x.shape
    return pl.pallas_call(
        _norm_mod_kernel,
        grid=(b, s // tm),
        in_specs=[pl.BlockSpec((1, tm, d), lambda bi, i: (bi, i, 0)),
                  pl.BlockSpec((1, d), lambda bi, i: (0, 0)),
                  pl.BlockSpec((1, 1, d), lambda bi, i: (bi, 0, 0)),
                  pl.BlockSpec((1, 1, d), lambda bi, i: (bi, 0, 0))],
        out_specs=pl.BlockSpec((1, tm, d), lambda bi, i: (bi, i, 0)),
        out_shape=jax.ShapeDtypeStruct((b, s, d), BF16),
        compiler_params=_cp(2),
        name="norm_mod",
    )(x, g, scale, shift)


def _final_norm_kernel(x_ref, g_ref, o_ref):
    o_ref[0] = _rms(x_ref[0], g_ref[...])


def _final_norm(x, g, *, tm=256):
    b, s, d = x.shape
    return pl.pallas_call(
        _final_norm_kernel,
        grid=(b, s // tm),
        in_specs=[pl.BlockSpec((1, tm, d), lambda bi, i: (bi, i, 0)),
                  pl.BlockSpec((1, d), lambda bi, i: (0, 0))],
        out_specs=pl.BlockSpec((1, tm, d), lambda bi, i: (bi, i, 0)),
        out_shape=jax.ShapeDtypeStruct((b, s, d), F32),
        compiler_params=_cp(2),
        name="final_norm",
    )(x, g)


def _rope_kernel(pos_ref, inv_ref, cos_ref, sin_ref):
    ang = pos_ref[0] * inv_ref[...]
    live = lax.broadcasted_iota(jnp.int32, ang.shape, 1) < MLA_ROPE
    cos_ref[0] = jnp.where(live, jnp.cos(ang), 0.0)
    sin_ref[0] = jnp.where(live, jnp.sin(ang), 0.0)


def _rope_tables(positions, *, tm=512):
    b, s = positions.shape
    inv = ROPE_THETA ** (-jnp.arange(0, MLA_ROPE, 2, dtype=F32) / MLA_ROPE)
    inv128 = jnp.concatenate([inv, inv, jnp.zeros((LANE - MLA_ROPE,), F32)])[None, :]
    pos128 = jnp.broadcast_to(positions.astype(F32)[:, :, None], (b, s, LANE))
    spec = pl.BlockSpec((1, tm, LANE), lambda bi, i: (bi, i, 0))
    return pl.pallas_call(
        _rope_kernel,
        grid=(b, s // tm),
        in_specs=[spec, pl.BlockSpec((1, LANE), lambda bi, i: (0, 0))],
        out_specs=[spec, spec],
        out_shape=[jax.ShapeDtypeStruct((b, s, LANE), F32)] * 2,
        compiler_params=_cp(2),
        name="rope_tables",
    )(pos128, inv128)


def _mla_q_kernel(cq_ref, g_ref, w1_ref, w2_ref, cos_ref, sin_ref, o_ref, an_ref, *, scale):
    @pl.when(pl.program_id(2) == 0)
    def _():
        an_ref[...] = _rms(cq_ref[0].astype(F32), g_ref[...]).astype(BF16)

    a = an_ref[...]
    a1 = jnp.dot(a, w1_ref[...], preferred_element_type=F32)
    a2 = jnp.dot(a, w2_ref[...], preferred_element_type=F32)
    rope = a1[:, MLA_NOPE:] * cos_ref[0] + a2 * sin_ref[0]
    o_ref[0, :, :MLA_NOPE] = (a1[:, :MLA_NOPE] * scale).astype(BF16)
    o_ref[0, :, MLA_NOPE:] = (rope * scale).astype(BF16)


def _mla_q(pm3, g_q, w1, w2, cos128, sin128, *, tm=512):
    b, s, _ = pm3.shape
    cq_blk = _PM_OFF["cq"] // MLA_Q_RANK
    return pl.pallas_call(
        functools.partial(_mla_q_kernel, scale=(MLA_NOPE + MLA_ROPE) ** -0.5 * LOG2E),
        grid=(b, s // tm, MLA_HEADS),
        in_specs=[pl.BlockSpec((1, tm, MLA_Q_RANK), lambda bi, i, h: (bi, i, cq_blk)),
                  pl.BlockSpec((1, MLA_Q_RANK), lambda bi, i, h: (0, 0)),
                  pl.BlockSpec((MLA_Q_RANK, MLA_QK_PAD), lambda bi, i, h: (0, h)),
                  pl.BlockSpec((MLA_Q_RANK, LANE), lambda bi, i, h: (0, h)),
                  pl.BlockSpec((1, tm, LANE), lambda bi, i, h: (bi, i, 0)),
                  pl.BlockSpec((1, tm, LANE), lambda bi, i, h: (bi, i, 0))],
        out_specs=pl.BlockSpec((1, tm, MLA_QK_PAD), lambda bi, i, h: (bi, i, h)),
        out_shape=jax.ShapeDtypeStruct((b, s, MLA_HEADS * MLA_QK_PAD), BF16),
        scratch_shapes=[pltpu.VMEM((tm, MLA_Q_RANK), BF16)],
        compiler_params=_cp(3),
        name="mla_q",
    )(pm3, g_q, w1, w2, cos128, sin128)


def _mla_kv_kernel(ckv_ref, g_ref, w_ref, kr_ref, krr_ref, cos_ref, sin_ref, k_ref, v_ref, an_ref):
    @pl.when(pl.program_id(2) == 0)
    def _():
        an_ref[...] = _rms(ckv_ref[0].astype(F32), g_ref[...]).astype(BF16)

    acc = jnp.dot(an_ref[...], w_ref[...], preferred_element_type=F32)
    kr = kr_ref[0].astype(F32) * cos_ref[0] + krr_ref[0].astype(F32) * sin_ref[0]
    k_ref[0, 0, 0, :MLA_NOPE, :] = acc[:, :MLA_NOPE].T.astype(BF16)
    k_ref[0, 0, 0, MLA_NOPE:, :] = kr.T.astype(BF16)
    v_ref[0, 0, 0, :, :MLA_V] = acc[:, MLA_NOPE:].astype(BF16)
    v_ref[0, 0, 0, :, MLA_V:] = jnp.ones((acc.shape[0], MLA_V), BF16)


def _mla_kv(pm3, g_kv, w_ukv, cos128, sin128):
    b, s, _ = pm3.shape
    tm = ATT_TK
    ckv_blk = _PM_OFF["ckv"] // MLA_KV_RANK
    kr_blk = _PM_OFF["kr"] // LANE
    krr_blk = _PM_OFF["krr"] // LANE
    return pl.pallas_call(
        _mla_kv_kernel,
        grid=(b, s // tm, MLA_HEADS),
        in_specs=[pl.BlockSpec((1, tm, MLA_KV_RANK), lambda bi, i, h: (bi, i, ckv_blk)),
                  pl.BlockSpec((1, MLA_KV_RANK), lambda bi, i, h: (0, 0)),
                  pl.BlockSpec((MLA_KV_RANK, MLA_NOPE + MLA_V), lambda bi, i, h: (0, h)),
                  pl.BlockSpec((1, tm, LANE), lambda bi, i, h: (bi, i, kr_blk)),
                  pl.BlockSpec((1, tm, LANE), lambda bi, i, h: (bi, i, krr_blk)),
                  pl.BlockSpec((1, tm, LANE), lambda bi, i, h: (bi, i, 0)),
                  pl.BlockSpec((1, tm, LANE), lambda bi, i, h: (bi, i, 0))],
        out_specs=[pl.BlockSpec((1, 1, 1, MLA_QK_PAD, tm), lambda bi, i, h: (bi, h, i, 0, 0)),
                   pl.BlockSpec((1, 1, 1, tm, 2 * MLA_V), lambda bi, i, h: (bi, h, i, 0, 0))],
        out_shape=[jax.ShapeDtypeStruct((b, MLA_HEADS, s // tm, MLA_QK_PAD, tm), BF16),
                   jax.ShapeDtypeStruct((b, MLA_HEADS, s // tm, tm, 2 * MLA_V), BF16)],
        scratch_shapes=[pltpu.VMEM((tm, MLA_KV_RANK), BF16)],
        compiler_params=_cp(3),
        name="mla_kv",
    )(pm3, g_kv, w_ukv, pm3, pm3, cos128, sin128)


def _sm_init(acc_ref, m_ref):
    m_ref[...] = jnp.full(m_ref.shape, NEG, F32)
    acc_ref[...] = jnp.zeros(acc_ref.shape, F32)


def _sm_step(s, v_ext, acc_ref, m_ref):
    m_old = m_ref[...]
    m_new = jnp.maximum(m_old, jnp.max(s, axis=-1, keepdims=True))
    alpha = jnp.exp2(m_old - m_new)
    p = jnp.exp2(s - jnp.tile(m_new, (1, s.shape[1] // LANE)))
    pv = jnp.dot(p.astype(BF16), v_ext, preferred_element_type=F32)
    acc_ref[...] = jnp.tile(alpha, (1, acc_ref.shape[1] // LANE)) * acc_ref[...] + pv
    m_ref[...] = m_new


def _sm_result(acc_ref, d):
    acc = acc_ref[...]
    return acc[:, :d] / acc[:, d:]


def _mla_attn_kernel(q_ref, kt_ref, v_ref, z_ref, o_ref, s_ref, acc_ref, m_ref):
    i = pl.program_id(2)
    t = ATT_TK
    assert ATT_TQ == 2 * t
    chains = range(2)
    qpos = lax.broadcasted_iota(jnp.int32, (t, t), 0)
    kpos = lax.broadcasted_iota(jnp.int32, (t, t), 1)

    def scores(u, j):
        return jnp.dot(q_ref[0, u * t:(u + 1) * t, :], kt_ref[0, 0, j], preferred_element_type=F32)

    def consume(u, buf, j, diagonal):
        s = s_ref[buf, u]
        if diagonal:
            s = jnp.where(kpos <= qpos, s, NEG)
        _sm_step(s, v_ref[0, 0, j], acc_ref.at[u], m_ref.at[u])

    for u in chains:
        _sm_init(acc_ref.at[u], m_ref.at[u])
        s_ref[0, u] = scores(u, 0)

    def block_pair(p, carry):
        j = 2 * p
        for u in chains:
            consume(u, 0, j, False)
            s_ref[1, u] = scores(u, j + 1)
        for u in chains:
            consume(u, 1, j + 1, False)
            s_ref[0, u] = scores(u, j + 2)
        return carry

    lax.fori_loop(0, i, block_pair, 0)
    first = 2 * i
    consume(0, 0, first, True)
    consume(1, 0, first, False)
    s_ref[1, 1] = scores(1, first + 1)
    consume(1, 1, first + 1, True)
    for u in chains:
        rows = slice(u * t, (u + 1) * t)
        o_ref[0, rows, :] = (_sm_result(acc_ref.at[u], MLA_V) * z_ref[0, rows, :].astype(F32)).astype(o_ref.dtype)


def _mla_attn(q, kt, v_ext, pz3):
    b, s, _ = q.shape
    nk = s // ATT_TK
    return pl.pallas_call(
        _mla_attn_kernel,
        grid=(b, MLA_HEADS, s // ATT_TQ),
        in_specs=[pl.BlockSpec((1, ATT_TQ, MLA_QK_PAD), lambda bi, h, i: (bi, i, h)),
                  pl.BlockSpec((1, 1, nk, MLA_QK_PAD, ATT_TK), lambda bi, h, i: (bi, h, 0, 0, 0)),
                  pl.BlockSpec((1, 1, nk, ATT_TK, 2 * MLA_V), lambda bi, h, i: (bi, h, 0, 0, 0)),
                  pl.BlockSpec((1, ATT_TQ, MLA_V), lambda bi, h, i: (bi, i, h))],
        out_specs=pl.BlockSpec((1, ATT_TQ, MLA_V), lambda bi, h, i: (bi, i, h)),
        out_shape=jax.ShapeDtypeStruct((b, s, MLA_WIDTH), BF16),
        scratch_shapes=[pltpu.VMEM((2, 2, ATT_TK, ATT_TK), F32),
                        pltpu.VMEM((2, ATT_TK, 2 * MLA_V), F32),
                        pltpu.VMEM((2, ATT_TK, LANE), F32)],
        compiler_params=_cp(3),
        name="mla_attn",
    )(q, kt, v_ext, pz3)


def _log_sigmoid(x):
    return jnp.minimum(x, 0.0) - jnp.log(1.0 + jnp.exp(-jnp.abs(x)))


def _gla_kernel(q_ref, k_ref, v_ref, gl_ref, z_ref, w2_ref, bg_ref, go_ref, o_ref, st_ref, *, nb):
    @pl.when(pl.program_id(0) == 0)
    def _():
        st_ref[...] = jnp.zeros(st_ref.shape, F32)

    c = GLA_CHUNK
    row = lax.broadcasted_iota(jnp.int32, (c, c), 0)
    col = lax.broadcasted_iota(jnp.int32, (c, c), 1)
    tril = row >= col
    tril_b = jnp.where(tril, 1.0, 0.0).astype(BF16)
    for b in range(nb):
        gl = gl_ref[b]
        for h in range(GLA_HEADS):
            ks = slice(h * GLA_DK, (h + 1) * GLA_DK)
            vs = slice(h * GLA_DV, (h + 1) * GLA_DV)
            q = q_ref[b, :, ks].astype(F32)
            k = k_ref[b, :, ks].astype(F32)
            v = v_ref[b, :, vs]
            pre = jnp.dot(gl, w2_ref[:, ks], preferred_element_type=F32) + bg_ref[:, ks]
            glog = _log_sigmoid(pre) / GLA_GATE_NORM
            g1 = glog.astype(BF16)
            r1 = glog - g1.astype(F32)
            g2 = r1.astype(BF16)
            g3 = (r1 - g2.astype(F32)).astype(BF16)
            bc = (jnp.dot(tril_b, g1, preferred_element_type=F32)
                  + jnp.dot(tril_b, g2, preferred_element_type=F32)
                  + jnp.dot(tril_b, g3, preferred_element_type=F32))
            b_last = bc[c - 1:c, :]
            q_dec = (q * GLA_DK ** -0.5 * jnp.exp(bc)).astype(BF16)
            k_inv = (k * jnp.exp(-bc)).astype(BF16)
            k_end = (k * jnp.exp(b_last - bc)).astype(BF16)
            decay = jnp.exp(b_last)
            attn = jnp.where(tril, _nt(q_dec, k_inv), 0.0).astype(BF16)
            st = st_ref[b * GLA_HEADS + h]
            o = jnp.dot(attn, v, preferred_element_type=F32) + _nt(q_dec, st.astype(BF16))
            upd = lax.dot_general(v, k_end, (((0,), (0,)), ((), ())), preferred_element_type=F32)
            st_ref[b * GLA_HEADS + h] = st * decay + upd
            on = _rms(o, go_ref[...])
            o_ref[b, :, vs] = (on * z_ref[b, :, vs].astype(F32)).astype(o_ref.dtype)


def _gla(pm3, pz3, w2p, bg, gout):
    b, s, _ = pm3.shape
    c = GLA_CHUNK
    qw = GLA_HEADS * GLA_DK
    q_blk, k_blk = _PM_OFF["gq"] // qw, _PM_OFF["gk"] // qw
    v_blk, gl_blk = _PM_OFF["gv"] // GLA_WIDTH, _PM_OFF["glow"] // LANE
    z_blk = MLA_WIDTH // GLA_WIDTH
    return pl.pallas_call(
        functools.partial(_gla_kernel, nb=b),
        grid=(s // c,),
        in_specs=[pl.BlockSpec((b, c, qw), lambda i: (0, i, q_blk)),
                  pl.BlockSpec((b, c, qw), lambda i: (0, i, k_blk)),
                  pl.BlockSpec((b, c, GLA_WIDTH), lambda i: (0, i, v_blk)),
                  pl.BlockSpec((b, c, LANE), lambda i: (0, i, gl_blk)),
                  pl.BlockSpec((b, c, GLA_WIDTH), lambda i: (0, i, z_blk)),
                  pl.BlockSpec((LANE, qw), lambda i: (0, 0)),
                  pl.BlockSpec((1, qw), lambda i: (0, 0)),
                  pl.BlockSpec((1, GLA_DV), lambda i: (0, 0))],
        out_specs=pl.BlockSpec((b, c, GLA_WIDTH), lambda i: (0, i, 0)),
        out_shape=jax.ShapeDtypeStruct((b, s, GLA_WIDTH), BF16),
        scratch_shapes=[pltpu.VMEM((b * GLA_HEADS, GLA_DV, GLA_DK), F32)],
        compiler_params=_cp(1),
        name="gla",
    )(pm3, pm3, pm3, pm3, pz3, w2p, bg, gout)


def _dsa_kernel(iq_ref, iw_ref, onehot_ref, ikt_ref, q_ref, kt_ref, v_ref, z_ref, o_ref,
                key_ref, keyt_ref, wb_ref, qs_ref, s_ref, acc_ref, m_ref, *, top_k):
    tq, tk = DSA_TQ, ATT_TK
    nk = key_ref.shape[0]
    lanes = tk // LANE
    i = pl.program_id(1)
    nkb = (i * tq + tq - 1) // tk + 1
    qpos = i * tq + lax.broadcasted_iota(jnp.int32, (tq, tk), 0)
    heads_per_dot = 4

    wb = jnp.dot(iw_ref[0], onehot_ref[...], preferred_element_type=F32)
    for h in range(IDX_HEADS):
        wb_ref[h] = wb[:, h * LANE:(h + 1) * LANE] * (IDX_HEADS ** -0.5 * IDX_DH ** -0.5)

    def sortable(x):
        bits = lax.bitcast_convert_type(x, jnp.int32)
        return bits ^ ((bits >> 31) & 0x7FFFFFFF)

    qpos_t = i * tq + lax.broadcasted_iota(jnp.int32, (tk, tq), 1)

    def score_block(j, carry):
        ikt = ikt_ref[0, j]
        sc = jnp.zeros((tq, tk), F32)
        for hg in range(IDX_HEADS // heads_per_dot):
            iq = iq_ref[0, hg * heads_per_dot:(hg + 1) * heads_per_dot].reshape(heads_per_dot * tq, IDX_DH)
            r = jnp.dot(iq, ikt, preferred_element_type=F32)
            for hh in range(heads_per_dot):
                wrow = jnp.tile(wb_ref[hg * heads_per_dot + hh], (1, lanes))
                sc = sc + jnp.maximum(r[hh * tq:(hh + 1) * tq, :], 0.0) * wrow
        kpos = j * tk + lax.broadcasted_iota(jnp.int32, (tq, tk), 1)
        key_ref[j] = jnp.where(kpos <= qpos, sortable(sc), INT_MIN)
        kpos_t = j * tk + lax.broadcasted_iota(jnp.int32, (tk, tq), 0)
        keyt_ref[j] = jnp.where(kpos_t <= qpos_t, sortable(sc.T), INT_MIN)
        return carry

    lax.fori_loop(0, nkb, score_block, 0)

    def count_ge(cand):
        def body(j, acc):
            ge = jnp.where(keyt_ref[j] >= cand, 1, 0).astype(jnp.int32)
            return acc + ge.reshape(tk // 8, 8, tq).sum(axis=0)
        acc = lax.fori_loop(0, nkb, body, jnp.zeros((8, tq), jnp.int32))
        return acc.sum(axis=0, keepdims=True)

    zero = jnp.zeros((1, tq), jnp.int32)
    thr = jnp.where(count_ge(zero) >= top_k, zero, INT_MIN)

    def bit_step(it, thr):
        cand = thr | lax.shift_left(jnp.int32(1), 30 - it)
        return jnp.where(count_ge(cand) >= top_k, cand, thr)

    thr = lax.fori_loop(0, 31, bit_step, thr)
    thr = jnp.maximum(thr, INT_MIN + 1)
    thr_w = jnp.tile(jnp.broadcast_to(thr, (tq, tq)).T, (1, lanes))

    scale = DSA_DH ** -0.5 * LOG2E
    groups = range(DSA_KV_HEADS)

    def scores(g, j):
        return jnp.dot(qs_ref[g], kt_ref[0, g, j], preferred_element_type=F32)

    def consume(g, buf, j):
        sel = key_ref[j] >= thr_w
        s = s_ref[buf, g]
        s = jnp.concatenate([jnp.where(sel, s[r * tq:(r + 1) * tq, :], NEG) for r in range(DSA_REP)], axis=0)
        _sm_step(s, v_ref[0, g, j], acc_ref.at[g], m_ref.at[g])

    for g in groups:
        for r in range(DSA_REP):
            cs = slice((g * DSA_REP + r) * DSA_DH, (g * DSA_REP + r + 1) * DSA_DH)
            qs_ref[g, r * tq:(r + 1) * tq, :] = (q_ref[0, :, cs].astype(F32) * scale).astype(BF16)
        _sm_init(acc_ref.at[g], m_ref.at[g])
    for g in groups:
        s_ref[0, g] = scores(g, 0)

    def block_pair(p, carry):
        j = 2 * p
        for g in groups:
            consume(g, 0, j)
            s_ref[1, g] = scores(g, j + 1)
        for g in groups:
            consume(g, 1, j + 1)
            s_ref[0, g] = scores(g, jnp.minimum(j + 2, nk - 1))
        return carry

    lax.fori_loop(0, nkb // 2, block_pair, 0)

    @pl.when(nkb % 2 == 1)
    def _():
        for g in groups:
            consume(g, 0, nkb - 1)

    for g in groups:
        res = _sm_result(acc_ref.at[g], DSA_DH)
        for r in range(DSA_REP):
            cs = slice((g * DSA_REP + r) * DSA_DH, (g * DSA_REP + r + 1) * DSA_DH)
            o_ref[0, :, cs] = (res[r * tq:(r + 1) * tq, :] * z_ref[0, :, cs].astype(F32)).astype(o_ref.dtype)


def _dsa(pm3, pz3, iq_t, iw3, ikt, dkt, dv_ext, top_k):
    b, s, _ = pm3.shape
    tq, tk = DSA_TQ, ATT_TK
    nk = s // tk
    q_blk = _PM_OFF["dq"] // DSA_WIDTH
    z_blk = (MLA_WIDTH + GLA_WIDTH) // DSA_WIDTH
    rows = DSA_REP * tq
    assert tq == LANE
    onehot = jnp.repeat(jnp.eye(IDX_HEADS, dtype=BF16), LANE, axis=1)
    return pl.pallas_call(
        functools.partial(_dsa_kernel, top_k=top_k),
        grid=(b, s // tq),
        in_specs=[pl.BlockSpec((1, IDX_HEADS, tq, IDX_DH), lambda bi, i: (bi, 0, i, 0)),
                  pl.BlockSpec((1, tq, IDX_HEADS), lambda bi, i: (bi, i, 0)),
                  pl.BlockSpec((IDX_HEADS, IDX_HEADS * LANE), lambda bi, i: (0, 0)),
                  pl.BlockSpec((1, nk, IDX_DH, tk), lambda bi, i: (bi, 0, 0, 0)),
                  pl.BlockSpec((1, tq, DSA_WIDTH), lambda bi, i: (bi, i, q_blk)),
                  pl.BlockSpec((1, DSA_KV_HEADS, nk, DSA_DH, tk), lambda bi, i: (bi, 0, 0, 0, 0)),
                  pl.BlockSpec((1, DSA_KV_HEADS, nk, tk, 2 * DSA_DH), lambda bi, i: (bi, 0, 0, 0, 0)),
                  pl.BlockSpec((1, tq, DSA_WIDTH), lambda bi, i: (bi, i, z_blk))],
        out_specs=pl.BlockSpec((1, tq, DSA_WIDTH), lambda bi, i: (bi, i, 0)),
        out_shape=jax.ShapeDtypeStruct((b, s, DSA_WIDTH), BF16),
        scratch_shapes=[pltpu.VMEM((nk, tq, tk), jnp.int32),
                        pltpu.VMEM((nk, tk, tq), jnp.int32),
                        pltpu.VMEM((IDX_HEADS, tq, LANE), F32),
                        pltpu.VMEM((DSA_KV_HEADS, rows, DSA_DH), BF16),
                        pltpu.VMEM((2, DSA_KV_HEADS, rows, tk), F32),
                        pltpu.VMEM((DSA_KV_HEADS, rows, 2 * DSA_DH), F32),
                        pltpu.VMEM((DSA_KV_HEADS, rows, LANE), F32)],
        compiler_params=_cp(2),
        name="dsa",
    )(iq_t, iw3, onehot, ikt, pm3, dkt, dv_ext, pz3)


def _lift_kernel(ym_ref, yg_ref, yd_ref, wm_ref, wg_ref, wd_ref, ga_ref, gb_ref, gc_ref, o_ref):
    a = jnp.dot(ym_ref[...], wm_ref[...], preferred_element_type=F32)
    b = jnp.dot(yg_ref[...], wg_ref[...], preferred_element_type=F32)
    c = jnp.dot(yd_ref[...], wd_ref[...], preferred_element_type=F32)
    o = ga_ref[...].astype(F32) * a + gb_ref[...].astype(F32) * b + gc_ref[...].astype(F32) * c
    o_ref[...] = o.astype(o_ref.dtype)


def _lift(y_mla, y_gla, y_dsa, w_bm, w_bg, w_bd, gates, *, tm=512, tn=512):
    t = y_mla.shape[0]
    d = w_bm.shape[1]
    nj = d // tn
    row = lambda width: pl.BlockSpec((tm, width), lambda i, j: (i, 0))
    wcol = lambda depth: pl.BlockSpec((depth, tn), lambda i, j: (0, j))
    gate = lambda br: pl.BlockSpec((tm, tn), lambda i, j: (i, br * nj + j))
    return pl.pallas_call(
        _lift_kernel,
        grid=(t // tm, nj),
        in_specs=[row(MLA_WIDTH), row(GLA_WIDTH), row(DSA_WIDTH),
                  wcol(MLA_WIDTH), wcol(GLA_WIDTH), wcol(DSA_WIDTH),
                  gate(0), gate(1), gate(2)],
        out_specs=pl.BlockSpec((tm, tn), lambda i, j: (i, j)),
        out_shape=jax.ShapeDtypeStruct((t, d), BF16),
        compiler_params=_cp(2),
        name="lift_merge",
    )(y_mla, y_gla, y_dsa, w_bm, w_bg, w_bd, gates, gates, gates)


def _out_kernel(m_ref, w_ref, x_ref, gate_ref, o_ref):
    r = jnp.dot(m_ref[0], w_ref[...], preferred_element_type=F32)
    o_ref[0] = x_ref[0] + gate_ref[0] * r


def _out_proj(merged3, w_o, x, gate, *, tm=1024, tn=512):
    b, s, d = x.shape
    tm = min(tm, s)
    return pl.pallas_call(
        _out_kernel,
        grid=(b, s // tm, d // tn),
        in_specs=[pl.BlockSpec((1, tm, d), lambda bi, i, j: (bi, i, 0)),
                  pl.BlockSpec((d, tn), lambda bi, i, j: (0, j)),
                  pl.BlockSpec((1, tm, tn), lambda bi, i, j: (bi, i, j)),
                  pl.BlockSpec((1, 1, tn), lambda bi, i, j: (bi, 0, j))],
        out_specs=pl.BlockSpec((1, tm, tn), lambda bi, i, j: (bi, i, j)),
        out_shape=jax.ShapeDtypeStruct((b, s, d), F32),
        compiler_params=_cp(3),
        name="out_proj",
    )(merged3, w_o, x, gate)


def _main_weight(w_in):
    d = w_in.shape[0]

    def cols(name):
        return w_in[:, _IN_OFF[name]:_IN_OFF[name] + _IN_W[name]]

    def padded(w, width):
        return jnp.concatenate([w, jnp.zeros((d, width - w.shape[1]), w.dtype)], axis=1)

    kr = cols("krope")
    half = MLA_ROPE // 2
    parts = {"kr": padded(kr, LANE),
             "krr": padded(jnp.concatenate([-kr[:, half:], kr[:, :half]], axis=1), LANE),
             "glow": padded(cols("glow"), LANE), "ik": padded(cols("ik"), LANE),
             "iw": padded(cols("iw"), LANE), "pad": jnp.zeros((d, LANE), w_in.dtype)}
    return jnp.concatenate([parts[n] if n in parts else cols(n) for n, _ in _PM_LAYOUT], axis=1).astype(BF16)


def _mla_q_weights(w_uq):
    r = w_uq.shape[0]
    w = w_uq.reshape(r, MLA_HEADS, MLA_NOPE + MLA_ROPE)
    nope, rope = w[..., :MLA_NOPE], w[..., MLA_NOPE:]
    half = MLA_ROPE // 2
    zeros = jnp.zeros((r, MLA_HEADS, LANE - MLA_ROPE), w.dtype)
    w1 = jnp.concatenate([nope, rope, zeros], axis=-1).reshape(r, MLA_HEADS * MLA_QK_PAD)
    rot = jnp.concatenate([-rope[..., half:], rope[..., :half]], axis=-1)
    w2 = jnp.concatenate([rot, zeros], axis=-1).reshape(r, MLA_HEADS * LANE)
    return w1.astype(BF16), w2.astype(BF16)


def _layer(x, c_pad, cos128, sin128, norm_g, w_ada, b_ada, w_in, mla_gq, mla_wuq, mla_gkv, mla_wukv,
           gla_wg2, gla_bg, gla_gout, w_mg, b_mg, w_bm, w_bg, w_bd, w_o):
    b, s, d = x.shape
    t = b * s
    mod = _mm(c_pad, w_ada, b_ada[None, :], tm=c_pad.shape[0], tn=512, out_dtype=F32,
              a_act="silu", name="ada")[:b]
    shift, scale, gate = (mod[:, None, k * d:(k + 1) * d] for k in range(3))
    h = _norm_mod(x, norm_g[None, :], scale, shift)
    h2 = h.reshape(t, d)

    mm_tm = min(1024, t)
    pm = _mm(h2, _main_weight(w_in), None, tm=mm_tm, tn=512, out_dtype=BF16, name="proj_main")
    z0 = _IN_OFF["z_mla"]
    pz = _mm(h2, w_in[:, z0:].astype(BF16), None, tm=mm_tm, tn=512, out_dtype=BF16, act="silu",
             name="proj_gate_paths")
    gates = _mm(h2, w_mg.astype(BF16), b_mg[None, :], tm=mm_tm, tn=512, out_dtype=BF16,
                act="sigmoid", name="merge_gates")
    pm3 = pm.reshape(b, s, PM_WIDTH)
    pz3 = pz.reshape(b, s, d)

    w1, w2 = _mla_q_weights(mla_wuq)
    q = _mla_q(pm3, mla_gq[None, :], w1, w2, cos128, sin128)
    kcat, vt = _mla_kv(pm3, mla_gkv[None, :], mla_wukv.astype(BF16), cos128, sin128)
    y_mla = _mla_attn(q, kcat, vt, pz3)

    w2p = jnp.concatenate([gla_wg2, jnp.zeros((LANE - GLA_GATE_RANK, gla_wg2.shape[1]), gla_wg2.dtype)],
                          axis=0).astype(BF16)
    y_gla = _gla(pm3, pz3, w2p, gla_bg[None, :], gla_gout[None, :])

    nk = s // ATT_TK

    def piece(name, width):
        return pm3[:, :, _PM_OFF[name]:_PM_OFF[name] + width]

    iq_t = piece("iq", IDX_HEADS * IDX_DH).reshape(b, s, IDX_HEADS, IDX_DH).transpose(0, 2, 1, 3)
    iw3 = piece("iw", IDX_HEADS)
    ikt = piece("ik", IDX_DH).reshape(b, nk, ATT_TK, IDX_DH).transpose(0, 1, 3, 2)
    kv5 = (b, nk, ATT_TK, DSA_KV_HEADS, DSA_DH)
    dkt = piece("dk", DSA_KV_HEADS * DSA_DH).reshape(kv5).transpose(0, 3, 1, 4, 2)
    dv = piece("dv", DSA_KV_HEADS * DSA_DH).reshape(kv5).transpose(0, 3, 1, 2, 4)
    dv_ext = jnp.concatenate([dv, jnp.ones_like(dv)], axis=-1)
    y_dsa = _dsa(pm3, pz3, iq_t, iw3, ikt, dkt, dv_ext, min(IDX_TOPK, s // 4))

    merged = _lift(y_mla.reshape(t, MLA_WIDTH), y_gla.reshape(t, GLA_WIDTH), y_dsa.reshape(t, DSA_WIDTH),
                   w_bm.astype(BF16), w_bg.astype(BF16), w_bd.astype(BF16), gates)
    return _out_proj(merged.reshape(b, s, d), w_o.astype(BF16), x, gate)


def kernel(x, c, positions, norm_g, w_ada, b_ada, w_in, mla_gq, mla_wuq, mla_gkv, mla_wukv, gla_wg2,
           gla_bg, gla_gout, w_mg, b_mg, w_bm, w_bg, w_bd, w_o, final_g):
    b = x.shape[0]
    cos128, sin128 = _rope_tables(positions)
    c_pad = jnp.concatenate([c, jnp.zeros((8 - b, c.shape[1]), c.dtype)], axis=0)
    for l in range(DEPTH):
        x = _layer(x, c_pad, cos128, sin128, norm_g[l], w_ada[l], b_ada[l], w_in[l], mla_gq[l],
                   mla_wuq[l], mla_gkv[l], mla_wukv[l], gla_wg2[l], gla_bg[l], gla_gout[l],
                   w_mg[l], b_mg[l], w_bm[l], w_bg[l], w_bd[l], w_o[l])
    return _final_norm(x, final_g[None, :])
```

```python
import functools

import jax
import jax.numpy as jnp
import numpy as np
from jax import lax
from jax.experimental import pallas as pl
from jax.experimental.pallas import tpu as pltpu

D_MODEL = 4096
DEPTH = 2
MLA_HEADS = 16
MLA_Q_RANK = 768
MLA_KV_RANK = 512
MLA_NOPE = 128
MLA_ROPE = 64
MLA_V = 128
ROPE_THETA = 10000.0
GLA_HEADS = 4
GLA_DK = 128
GLA_DV = 256
GLA_GATE_RANK = 16
GLA_GATE_NORM = 16.0
GLA_CHUNK = 64
DSA_HEADS = 8
DSA_KV_HEADS = 2
DSA_DH = 128
IDX_HEADS = 32
IDX_DH = 64
IDX_TOPK = 256
NORM_EPS = 1e-6
NEG = -1e30

MLA_WIDTH = MLA_HEADS * MLA_V
GLA_WIDTH = GLA_HEADS * GLA_DV
DSA_WIDTH = DSA_HEADS * DSA_DH
DSA_REP = DSA_HEADS // DSA_KV_HEADS

IN_SPLITS = (
    MLA_Q_RANK, MLA_KV_RANK, MLA_ROPE,
    GLA_HEADS * GLA_DK, GLA_HEADS * GLA_DK, GLA_WIDTH, GLA_GATE_RANK,
    DSA_WIDTH, DSA_KV_HEADS * DSA_DH, DSA_KV_HEADS * DSA_DH,
    IDX_HEADS * IDX_DH, IDX_DH, IDX_HEADS,
    MLA_WIDTH, GLA_WIDTH, DSA_WIDTH,
)
_IN_NAMES = ("cq", "ckv", "krope", "gq", "gk", "gv", "glow", "dq", "dk", "dv",
             "iq", "ik", "iw", "z_mla", "z_gla", "z_dsa")
_IN_OFF = dict(zip(_IN_NAMES, np.concatenate([[0], np.cumsum(IN_SPLITS)[:-1]]).tolist()))
_IN_W = dict(zip(_IN_NAMES, IN_SPLITS))

LANE = 128
MLA_QK_PAD = 256
ATT_TQ = 1024
ATT_TK = 512
MLA_HEADS_PER_STEP = 4
LOG2E = 1.4426950408889634
DSA_TQ = 128
VMEM_LIMIT = 56 * 1024 * 1024

_PM_LAYOUT = (("dq", 1024), ("gv", 1024), ("iq", 2048), ("gq", 512), ("cq", 768), ("dk", 256),
              ("ckv", 512), ("gk", 512), ("dv", 256), ("kr", 128), ("krr", 128), ("glow", 128),
              ("ik", 128), ("iw", 128), ("pad", 128))
_PM_OFF = {}
_o = 0
for _n, _w in _PM_LAYOUT:
    assert _o % _w == 0
    _PM_OFF[_n] = _o
    _o += _w
PM_WIDTH = _o

BF16 = jnp.bfloat16
F32 = jnp.float32
INT_MIN = -2 ** 31
HALF = 2 ** 15


def _cp(n_axes, flags=None):
    return pltpu.CompilerParams(dimension_semantics=("arbitrary",) * n_axes,
                                vmem_limit_bytes=VMEM_LIMIT, flags=flags)


def _nt(a, b):
    return lax.dot_general(a, b, (((1,), (1,)), ((), ())), preferred_element_type=F32)


def _rms(x, g):
    return x * lax.rsqrt(jnp.mean(x * x, axis=-1, keepdims=True) + NORM_EPS) * g


def _mm_kernel(*refs, a_act, act, has_bias):
    if has_bias:
        a_ref, w_ref, b_ref, o_ref = refs
    else:
        a_ref, w_ref, o_ref = refs
    a = a_ref[...]
    if a_act == "silu":
        a = a.astype(F32)
        a = a * jax.nn.sigmoid(a)
    r = jnp.dot(a.astype(BF16), w_ref[...].astype(BF16), preferred_element_type=F32)
    if has_bias:
        r = r + b_ref[...]
    if act == "sigmoid":
        r = jax.nn.sigmoid(r)
    elif act == "silu":
        r = r * jax.nn.sigmoid(r)
    o_ref[...] = r.astype(o_ref.dtype)


def _mm(a, w, bias, *, tm, tn, out_dtype, a_act=None, act=None, name):
    m, k = a.shape
    n = w.shape[1]
    in_specs = [pl.BlockSpec((tm, k), lambda i, j: (i, 0)),
                pl.BlockSpec((k, tn), lambda i, j: (0, j))]
    args = [a, w]
    if bias is not None:
        in_specs.append(pl.BlockSpec((1, tn), lambda i, j: (0, j)))
        args.append(bias)
    return pl.pallas_call(
        functools.partial(_mm_kernel, a_act=a_act, act=act, has_bias=bias is not None),
        grid=(m // tm, n // tn),
        in_specs=in_specs,
        out_specs=pl.BlockSpec((tm, tn), lambda i, j: (i, j)),
        out_shape=jax.ShapeDtypeStruct((m, n), out_dtype),
        compiler_params=_cp(2),
        name=name,
    )(*args)


def _norm_mod_kernel(x_ref, g_ref, sc_ref, sh_ref, o_ref):
    y = _rms(x_ref[0], g_ref[...])
    o_ref[0] = (y * (1.0 + sc_ref[0]) + sh_ref[0]).astype(o_ref.dtype)


def _norm_mod(x, g, scale, shift, *, tm=256):
    b, s, d = x.shape
    return pl.pallas_call(
        _norm_mod_kernel,
        grid=(b, s // tm),
        in_specs=[pl.BlockSpec((1, tm, d), lambda bi, i: (bi, i, 0)),
                  pl.BlockSpec((1, d), lambda bi, i: (0, 0)),
                  pl.BlockSpec((1, 1, d), lambda bi, i: (bi, 0, 0)),
                  pl.BlockSpec((1, 1, d), lambda bi, i: (bi, 0, 0))],
        out_specs=pl.BlockSpec((1, tm, d), lambda bi, i: (bi, i, 0)),
        out_shape=jax.ShapeDtypeStruct((b, s, d), BF16),
        compiler_params=_cp(2),
        name="norm_mod",
    )(x, g, scale, shift)


def _final_norm_kernel(x_ref, g_ref, o_ref):
    o_ref[0] = _rms(x_ref[0], g_ref[...])


def _final_norm(x, g, *, tm=256):
    b, s, d = x.shape
    return pl.pallas_call(
        _final_norm_kernel,
        grid=(b, s // tm),
        in_specs=[pl.BlockSpec((1, tm, d), lambda bi, i: (bi, i, 0)),
                  pl.BlockSpec((1, d), lambda bi, i: (0, 0))],
        out_specs=pl.BlockSpec((1, tm, d), lambda bi, i: (bi, i, 0)),
        out_shape=jax.ShapeDtypeStruct((b, s, d), F32),
        compiler_params=_cp(2),
        name="final_norm",
    )(x, g)


def _rope_kernel(pos_ref, inv_ref, cos_ref, sin_ref):
    ang = pos_ref[0] * inv_ref[...]
    live = lax.broadcasted_iota(jnp.int32, ang.shape, 1) < MLA_ROPE
    cos_ref[0] = jnp.where(live, jnp.cos(ang), 0.0)
    sin_ref[0] = jnp.where(live, jnp.sin(ang), 0.0)


def _rope_tables(positions, *, tm=512):
    b, s = positions.shape
    inv = ROPE_THETA ** (-jnp.arange(0, MLA_ROPE, 2, dtype=F32) / MLA_ROPE)
    inv128 = jnp.concatenate([inv, inv, jnp.zeros((LANE - MLA_ROPE,), F32)])[None, :]
    pos128 = jnp.broadcast_to(positions.astype(F32)[:, :, None], (b, s, LANE))
    spec = pl.BlockSpec((1, tm, LANE), lambda bi, i: (bi, i, 0))
    return pl.pallas_call(
        _rope_kernel,
        grid=(b, s // tm),
        in_specs=[spec, pl.BlockSpec((1, LANE), lambda bi, i: (0, 0))],
        out_specs=[spec, spec],
        out_shape=[jax.ShapeDtypeStruct((b, s, LANE), F32)] * 2,
        compiler_params=_cp(2),
        name="rope_tables",
    )(pos128, inv128)


def _mla_q_kernel(cq_ref, g_ref, w1_ref, w2_ref, cos_ref, sin_ref, o_ref, an_ref, *, scale):
    @pl.when(pl.program_id(2) == 0)
    def _():
        an_ref[...] = _rms(cq_ref[0].astype(F32), g_ref[...]).astype(BF16)

    a = an_ref[...]
    a1 = jnp.dot(a, w1_ref[...], preferred_element_type=F32)
    a2 = jnp.dot(a, w2_ref[...], preferred_element_type=F32)
    cos, sin = cos_ref[0], sin_ref[0]
    for u in range(MLA_HEADS_PER_STEP):
        c0 = u * MLA_QK_PAD
        rope = a1[:, c0 + MLA_NOPE:c0 + MLA_QK_PAD] * cos + a2[:, u * LANE:(u + 1) * LANE] * sin
        o_ref[0, :, c0:c0 + MLA_NOPE] = (a1[:, c0:c0 + MLA_NOPE] * scale).astype(BF16)
        o_ref[0, :, c0 + MLA_NOPE:c0 + MLA_QK_PAD] = (rope * scale).astype(BF16)


def _mla_q(pm3, g_q, w1, w2, cos128, sin128, *, tm=512):
    b, s, _ = pm3.shape
    hp = MLA_HEADS_PER_STEP
    cq_blk = _PM_OFF["cq"] // MLA_Q_RANK
    return pl.pallas_call(
        functools.partial(_mla_q_kernel, scale=(MLA_NOPE + MLA_ROPE) ** -0.5 * LOG2E),
        grid=(b, s // tm, MLA_HEADS // hp),
        in_specs=[pl.BlockSpec((1, tm, MLA_Q_RANK), lambda bi, i, h: (bi, i, cq_blk)),
                  pl.BlockSpec((1, MLA_Q_RANK), lambda bi, i, h: (0, 0)),
                  pl.BlockSpec((MLA_Q_RANK, hp * MLA_QK_PAD), lambda bi, i, h: (0, h)),
                  pl.BlockSpec((MLA_Q_RANK, hp * LANE), lambda bi, i, h: (0, h)),
                  pl.BlockSpec((1, tm, LANE), lambda bi, i, h: (bi, i, 0)),
                  pl.BlockSpec((1, tm, LANE), lambda bi, i, h: (bi, i, 0))],
        out_specs=pl.BlockSpec((1, tm, hp * MLA_QK_PAD), lambda bi, i, h: (bi, i, h)),
        out_shape=jax.ShapeDtypeStruct((b, s, MLA_HEADS * MLA_QK_PAD), BF16),
        scratch_shapes=[pltpu.VMEM((tm, MLA_Q_RANK), BF16)],
        compiler_params=_cp(3),
        name="mla_q",
    )(pm3, g_q, w1, w2, cos128, sin128)


def _mla_kv_kernel(ckv_ref, g_ref, w_ref, kr_ref, krr_ref, cos_ref, sin_ref, k_ref, v_ref, an_ref, krt_ref):
    @pl.when(pl.program_id(2) == 0)
    def _():
        an_ref[...] = _rms(ckv_ref[0].astype(F32), g_ref[...]).astype(BF16)
        kr = kr_ref[0].astype(F32) * cos_ref[0] + krr_ref[0].astype(F32) * sin_ref[0]
        krt_ref[...] = kr.T.astype(BF16)

    acc = jnp.dot(an_ref[...], w_ref[...], preferred_element_type=F32)
    ones = jnp.ones((acc.shape[0], MLA_V), BF16)
    for u in range(MLA_HEADS_PER_STEP):
        c0 = u * (MLA_NOPE + MLA_V)
        k_ref[0, u, 0, :MLA_NOPE, :] = acc[:, c0:c0 + MLA_NOPE].T.astype(BF16)
        k_ref[0, u, 0, MLA_NOPE:, :] = krt_ref[...]
        v_ref[0, u, 0, :, :MLA_V] = acc[:, c0 + MLA_NOPE:c0 + MLA_NOPE + MLA_V].astype(BF16)
        v_ref[0, u, 0, :, MLA_V:] = ones


def _mla_kv(pm3, g_kv, w_ukv, cos128, sin128):
    b, s, _ = pm3.shape
    tm = ATT_TK
    hp = MLA_HEADS_PER_STEP
    ckv_blk = _PM_OFF["ckv"] // MLA_KV_RANK
    kr_blk = _PM_OFF["kr"] // LANE
    krr_blk = _PM_OFF["krr"] // LANE
    return pl.pallas_call(
        _mla_kv_kernel,
        grid=(b, s // tm, MLA_HEADS // hp),
        in_specs=[pl.BlockSpec((1, tm, MLA_KV_RANK), lambda bi, i, h: (bi, i, ckv_blk)),
                  pl.BlockSpec((1, MLA_KV_RANK), lambda bi, i, h: (0, 0)),
                  pl.BlockSpec((MLA_KV_RANK, hp * (MLA_NOPE + MLA_V)), lambda bi, i, h: (0, h)),
                  pl.BlockSpec((1, tm, LANE), lambda bi, i, h: (bi, i, kr_blk)),
                  pl.BlockSpec((1, tm, LANE), lambda bi, i, h: (bi, i, krr_blk)),
                  pl.BlockSpec((1, tm, LANE), lambda bi, i, h: (bi, i, 0)),
                  pl.BlockSpec((1, tm, LANE), lambda bi, i, h: (bi, i, 0))],
        out_specs=[pl.BlockSpec((1, hp, 1, MLA_QK_PAD, tm), lambda bi, i, h: (bi, h, i, 0, 0)),
                   pl.BlockSpec((1, hp, 1, tm, 2 * MLA_V), lambda bi, i, h: (bi, h, i, 0, 0))],
        out_shape=[jax.ShapeDtypeStruct((b, MLA_HEADS, s // tm, MLA_QK_PAD, tm), BF16),
                   jax.ShapeDtypeStruct((b, MLA_HEADS, s // tm, tm, 2 * MLA_V), BF16)],
        scratch_shapes=[pltpu.VMEM((tm, MLA_KV_RANK), BF16), pltpu.VMEM((LANE, tm), BF16)],
        compiler_params=_cp(3),
        name="mla_kv",
    )(pm3, g_kv, w_ukv, pm3, pm3, cos128, sin128)


def _sm_init(acc_ref, m_ref):
    m_ref[...] = jnp.full(m_ref.shape, NEG, F32)
    acc_ref[...] = jnp.zeros(acc_ref.shape, F32)


def _sm_step(s, v_ext, acc_ref, m_ref):
    m_old = m_ref[...]
    m_new = jnp.maximum(m_old, jnp.max(s, axis=-1, keepdims=True))
    alpha = jnp.exp2(m_old - m_new)
    p = jnp.exp2(s - jnp.tile(m_new, (1, s.shape[1] // LANE)))
    pv = jnp.dot(p.astype(BF16), v_ext, preferred_element_type=F32)
    acc_ref[...] = jnp.tile(alpha, (1, acc_ref.shape[1] // LANE)) * acc_ref[...] + pv
    m_ref[...] = m_new


def _sm_result(acc_ref, d):
    acc = acc_ref[...]
    return acc[:, :d] / acc[:, d:]


def _mla_attn_kernel(q_ref, kt_ref, v_ref, z_ref, o_ref, s_ref, acc_ref, m_ref):
    i = pl.program_id(2)
    t = ATT_TK
    assert ATT_TQ == 2 * t
    chains = range(2)
    qpos = lax.broadcasted_iota(jnp.int32, (t, t), 0)
    kpos = lax.broadcasted_iota(jnp.int32, (t, t), 1)

    def scores(u, j):
        return jnp.dot(q_ref[0, u * t:(u + 1) * t, :], kt_ref[0, 0, j], preferred_element_type=F32)

    def consume(u, buf, j, diagonal):
        s = s_ref[buf, u]
        if diagonal:
            s = jnp.where(kpos <= qpos, s, NEG)
        _sm_step(s, v_ref[0, 0, j], acc_ref.at[u], m_ref.at[u])

    for u in chains:
        _sm_init(acc_ref.at[u], m_ref.at[u])
        s_ref[0, u] = scores(u, 0)

    def block_pair(p, carry):
        j = 2 * p
        for u in chains:
            consume(u, 0, j, False)
            s_ref[1, u] = scores(u, j + 1)
        for u in chains:
            consume(u, 1, j + 1, False)
            s_ref[0, u] = scores(u, j + 2)
        return carry

    lax.fori_loop(0, i, block_pair, 0)
    first = 2 * i
    consume(0, 0, first, True)
    consume(1, 0, first, False)
    s_ref[1, 1] = scores(1, first + 1)
    consume(1, 1, first + 1, True)
    for u in chains:
        rows = slice(u * t, (u + 1) * t)
        o_ref[0, rows, :] = (_sm_result(acc_ref.at[u], MLA_V) * z_ref[0, rows, :].astype(F32)).astype(o_ref.dtype)


def _mla_attn(q, kt, v_ext, pz3):
    b, s, _ = q.shape
    nk = s // ATT_TK
    return pl.pallas_call(
        _mla_attn_kernel,
        grid=(b, MLA_HEADS, s // ATT_TQ),
        in_specs=[pl.BlockSpec((1, ATT_TQ, MLA_QK_PAD), lambda bi, h, i: (bi, i, h)),
                  pl.BlockSpec((1, 1, nk, MLA_QK_PAD, ATT_TK), lambda bi, h, i: (bi, h, 0, 0, 0)),
                  pl.BlockSpec((1, 1, nk, ATT_TK, 2 * MLA_V), lambda bi, h, i: (bi, h, 0, 0, 0)),
                  pl.BlockSpec((1, ATT_TQ, MLA_V), lambda bi, h, i: (bi, i, h))],
        out_specs=pl.BlockSpec((1, ATT_TQ, MLA_V), lambda bi, h, i: (bi, i, h)),
        out_shape=jax.ShapeDtypeStruct((b, s, MLA_WIDTH), BF16),
        scratch_shapes=[pltpu.VMEM((2, 2, ATT_TK, ATT_TK), F32),
                        pltpu.VMEM((2, ATT_TK, 2 * MLA_V), F32),
                        pltpu.VMEM((2, ATT_TK, LANE), F32)],
        compiler_params=_cp(3),
        name="mla_attn",
    )(q, kt, v_ext, pz3)


def _log_sigmoid(x):
    return jnp.minimum(x, 0.0) - jnp.log(1.0 + jnp.exp(-jnp.abs(x)))


def _gla_kernel(q_ref, k_ref, v_ref, gl_ref, z_ref, w2_ref, bg_ref, go_ref, o_ref, st_ref, *, nb):
    @pl.when(pl.program_id(0) == 0)
    def _():
        st_ref[...] = jnp.zeros(st_ref.shape, F32)

    c = GLA_CHUNK
    row = lax.broadcasted_iota(jnp.int32, (c, c), 0)
    col = lax.broadcasted_iota(jnp.int32, (c, c), 1)
    tril = row >= col
    tril_b = jnp.where(tril, 1.0, 0.0).astype(BF16)
    for b in range(nb):
        gl = gl_ref[b]
        for h in range(GLA_HEADS):
            ks = slice(h * GLA_DK, (h + 1) * GLA_DK)
            vs = slice(h * GLA_DV, (h + 1) * GLA_DV)
            q = q_ref[b, :, ks].astype(F32)
            k = k_ref[b, :, ks].astype(F32)
            v = v_ref[b, :, vs]
            pre = jnp.dot(gl, w2_ref[:, ks], preferred_element_type=F32) + bg_ref[:, ks]
            glog = _log_sigmoid(pre) / GLA_GATE_NORM
            g1 = glog.astype(BF16)
            r1 = glog - g1.astype(F32)
            g2 = r1.astype(BF16)
            g3 = (r1 - g2.astype(F32)).astype(BF16)
            bc = (jnp.dot(tril_b, g1, preferred_element_type=F32)
                  + jnp.dot(tril_b, g2, preferred_element_type=F32)
                  + jnp.dot(tril_b, g3, preferred_element_type=F32))
            b_last = bc[c - 1:c, :]
            q_dec = (q * GLA_DK ** -0.5 * jnp.exp(bc)).astype(BF16)
            k_inv = (k * jnp.exp(-bc)).astype(BF16)
            k_end = (k * jnp.exp(b_last - bc)).astype(BF16)
            decay = jnp.exp(b_last)
            attn = jnp.where(tril, _nt(q_dec, k_inv), 0.0).astype(BF16)
            st = st_ref[b * GLA_HEADS + h]
            o = jnp.dot(attn, v, preferred_element_type=F32) + _nt(q_dec, st.astype(BF16))
            upd = lax.dot_general(v, k_end, (((0,), (0,)), ((), ())), preferred_element_type=F32)
            st_ref[b * GLA_HEADS + h] = st * decay + upd
            on = _rms(o, go_ref[...])
            o_ref[b, :, vs] = (on * z_ref[b, :, vs].astype(F32)).astype(o_ref.dtype)


def _gla(pm3, pz3, w2p, bg, gout):
    b, s, _ = pm3.shape
    c = GLA_CHUNK
    qw = GLA_HEADS * GLA_DK
    q_blk, k_blk = _PM_OFF["gq"] // qw, _PM_OFF["gk"] // qw
    v_blk, gl_blk = _PM_OFF["gv"] // GLA_WIDTH, _PM_OFF["glow"] // LANE
    z_blk = MLA_WIDTH // GLA_WIDTH
    return pl.pallas_call(
        functools.partial(_gla_kernel, nb=b),
        grid=(s // c,),
        in_specs=[pl.BlockSpec((b, c, qw), lambda i: (0, i, q_blk)),
                  pl.BlockSpec((b, c, qw), lambda i: (0, i, k_blk)),
                  pl.BlockSpec((b, c, GLA_WIDTH), lambda i: (0, i, v_blk)),
                  pl.BlockSpec((b, c, LANE), lambda i: (0, i, gl_blk)),
                  pl.BlockSpec((b, c, GLA_WIDTH), lambda i: (0, i, z_blk)),
                  pl.BlockSpec((LANE, qw), lambda i: (0, 0)),
                  pl.BlockSpec((1, qw), lambda i: (0, 0)),
                  pl.BlockSpec((1, GLA_DV), lambda i: (0, 0))],
        out_specs=pl.BlockSpec((b, c, GLA_WIDTH), lambda i: (0, i, 0)),
        out_shape=jax.ShapeDtypeStruct((b, s, GLA_WIDTH), BF16),
        scratch_shapes=[pltpu.VMEM((b * GLA_HEADS, GLA_DV, GLA_DK), F32)],
        compiler_params=_cp(1),
        name="gla",
    )(pm3, pm3, pm3, pm3, pz3, w2p, bg, gout)


def _dsa_kernel(iq_ref, iw_ref, onehot_ref, ikt_ref, q_ref, kt_ref, v_ref, z_ref, o_ref,
                key_ref, hi_ref, lo_ref, wb_ref, qs_ref, s_ref, acc_ref, m_ref, *, top_k):
    tq, tk = DSA_TQ, ATT_TK
    nk = key_ref.shape[0]
    lanes = tk // LANE
    i = pl.program_id(1)
    nkb = (i * tq + tq - 1) // tk + 1
    qpos = i * tq + lax.broadcasted_iota(jnp.int32, (tq, tk), 0)
    heads_per_dot = 4

    wb = jnp.dot(iw_ref[0], onehot_ref[...], preferred_element_type=F32)
    for h in range(IDX_HEADS):
        wb_ref[h] = wb[:, h * LANE:(h + 1) * LANE] * (IDX_HEADS ** -0.5 * IDX_DH ** -0.5)

    def sortable(x):
        bits = lax.bitcast_convert_type(x, jnp.int32)
        return bits ^ ((bits >> 31) & 0x7FFFFFFF)

    qpos_t = i * tq + lax.broadcasted_iota(jnp.int32, (tk, tq), 1)

    def score_block(j, carry):
        ikt = ikt_ref[0, j]
        sc = jnp.zeros((tq, tk), F32)
        for hg in range(IDX_HEADS // heads_per_dot):
            iq = iq_ref[0, hg * heads_per_dot:(hg + 1) * heads_per_dot].reshape(heads_per_dot * tq, IDX_DH)
            r = jnp.dot(iq, ikt, preferred_element_type=F32)
            for hh in range(heads_per_dot):
                wrow = jnp.tile(wb_ref[hg * heads_per_dot + hh], (1, lanes))
                sc = sc + jnp.maximum(r[hh * tq:(hh + 1) * tq, :], 0.0) * wrow
        kpos = j * tk + lax.broadcasted_iota(jnp.int32, (tq, tk), 1)
        key_ref[j] = jnp.where(kpos <= qpos, sortable(sc), INT_MIN)
        kpos_t = j * tk + lax.broadcasted_iota(jnp.int32, (tk, tq), 0)
        key_t = jnp.where(kpos_t <= qpos_t, sortable(sc.T), INT_MIN)
        hi_ref[j] = (key_t >> 16).astype(jnp.int16)
        lo_ref[j] = ((key_t & 0xFFFF) - HALF).astype(jnp.int16)
        return carry

    lax.fori_loop(0, nkb, score_block, 0)

    def count(ref, pred):
        def body(j, acc):
            hit = jnp.where(pred(ref[j]), jnp.int16(1), jnp.int16(0))
            slabs = [hit[r:r + 16] for r in range(0, tk, 16)]
            while len(slabs) > 1:
                slabs = [a + b for a, b in zip(slabs[::2], slabs[1::2])]
            return acc + slabs[0]
        acc = lax.fori_loop(0, nkb, body, jnp.zeros((16, tq), jnp.int16))
        return acc.astype(jnp.int32).sum(axis=0, keepdims=True)

    def search16(ref, need):
        def ge(cand):
            c16 = cand.astype(jnp.int16)
            return count(ref, lambda x: x >= c16)
        zero = jnp.zeros((1, tq), jnp.int32)
        t = jnp.where(ge(zero) >= need, zero, -HALF)

        def bit_step(it, t):
            cand = t | lax.shift_left(jnp.int32(1), 14 - it)
            return jnp.where(ge(cand) >= need, cand, t)
        return lax.fori_loop(0, 15, bit_step, t)

    thr_hi = search16(hi_ref, top_k)
    hi16 = thr_hi.astype(jnp.int16)
    need_lo = top_k - count(hi_ref, lambda x: x > hi16)

    def keep_equal(j, carry):
        lo_ref[j] = jnp.where(hi_ref[j] == hi16, lo_ref[j], jnp.int16(-HALF))
        return carry

    lax.fori_loop(0, nkb, keep_equal, 0)
    thr_lo = search16(lo_ref, need_lo)
    thr = lax.shift_left(thr_hi, 16) | (thr_lo + HALF)
    thr = jnp.maximum(thr, INT_MIN + 1)
    thr_w = jnp.tile(jnp.broadcast_to(thr, (tq, tq)).T, (1, lanes))

    scale = DSA_DH ** -0.5 * LOG2E
    groups = range(DSA_KV_HEADS)

    def scores(g, j):
        return jnp.dot(qs_ref[g], kt_ref[0, g, j], preferred_element_type=F32)

    def consume(g, buf, j):
        sel = key_ref[j] >= thr_w
        s = s_ref[buf, g]
        s = jnp.concatenate([jnp.where(sel, s[r * tq:(r + 1) * tq, :], NEG) for r in range(DSA_REP)], axis=0)
        _sm_step(s, v_ref[0, g, j], acc_ref.at[g], m_ref.at[g])

    for g in groups:
        for r in range(DSA_REP):
            cs = slice((g * DSA_REP + r) * DSA_DH, (g * DSA_REP + r + 1) * DSA_DH)
            qs_ref[g, r * tq:(r + 1) * tq, :] = (q_ref[0, :, cs].astype(F32) * scale).astype(BF16)
        _sm_init(acc_ref.at[g], m_ref.at[g])
    for g in groups:
        s_ref[0, g] = scores(g, 0)

    def block_pair(p, carry):
        j = 2 * p
        for g in groups:
            consume(g, 0, j)
            s_ref[1, g] = scores(g, j + 1)
        for g in groups:
            consume(g, 1, j + 1)
            s_ref[0, g] = scores(g, jnp.minimum(j + 2, nk - 1))
        return carry

    lax.fori_loop(0, nkb // 2, block_pair, 0)

    @pl.when(nkb % 2 == 1)
    def _():
        for g in groups:
            consume(g, 0, nkb - 1)

    for g in groups:
        res = _sm_result(acc_ref.at[g], DSA_DH)
        for r in range(DSA_REP):
            cs = slice((g * DSA_REP + r) * DSA_DH, (g * DSA_REP + r + 1) * DSA_DH)
            o_ref[0, :, cs] = (res[r * tq:(r + 1) * tq, :] * z_ref[0, :, cs].astype(F32)).astype(o_ref.dtype)


def _dsa(pm3, pz3, iq_t, iw3, ikt, dkt, dv_ext, top_k):
    b, s, _ = pm3.shape
    tq, tk = DSA_TQ, ATT_TK
    nk = s // tk
    q_blk = _PM_OFF["dq"] // DSA_WIDTH
    z_blk = (MLA_WIDTH + GLA_WIDTH) // DSA_WIDTH
    rows = DSA_REP * tq
    assert tq == LANE
    onehot = jnp.repeat(jnp.eye(IDX_HEADS, dtype=BF16), LANE, axis=1)
    return pl.pallas_call(
        functools.partial(_dsa_kernel, top_k=top_k),
        grid=(b, s // tq),
        in_specs=[pl.BlockSpec((1, IDX_HEADS, tq, IDX_DH), lambda bi, i: (bi, 0, i, 0)),
                  pl.BlockSpec((1, tq, IDX_HEADS), lambda bi, i: (bi, i, 0)),
                  pl.BlockSpec((IDX_HEADS, IDX_HEADS * LANE), lambda bi, i: (0, 0)),
                  pl.BlockSpec((1, nk, IDX_DH, tk), lambda bi, i: (bi, 0, 0, 0)),
                  pl.BlockSpec((1, tq, DSA_WIDTH), lambda bi, i: (bi, i, q_blk)),
                  pl.BlockSpec((1, DSA_KV_HEADS, nk, DSA_DH, tk), lambda bi, i: (bi, 0, 0, 0, 0)),
                  pl.BlockSpec((1, DSA_KV_HEADS, nk, tk, 2 * DSA_DH), lambda bi, i: (bi, 0, 0, 0, 0)),
                  pl.BlockSpec((1, tq, DSA_WIDTH), lambda bi, i: (bi, i, z_blk))],
        out_specs=pl.BlockSpec((1, tq, DSA_WIDTH), lambda bi, i: (bi, i, 0)),
        out_shape=jax.ShapeDtypeStruct((b, s, DSA_WIDTH), BF16),
        scratch_shapes=[pltpu.VMEM((nk, tq, tk), jnp.int32),
                        pltpu.VMEM((nk, tk, tq), jnp.int16),
                        pltpu.VMEM((nk, tk, tq), jnp.int16),
                        pltpu.VMEM((IDX_HEADS, tq, LANE), F32),
                        pltpu.VMEM((DSA_KV_HEADS, rows, DSA_DH), BF16),
                        pltpu.VMEM((2, DSA_KV_HEADS, rows, tk), F32),
                        pltpu.VMEM((DSA_KV_HEADS, rows, 2 * DSA_DH), F32),
                        pltpu.VMEM((DSA_KV_HEADS, rows, LANE), F32)],
        compiler_params=_cp(2),
        name="dsa",
    )(iq_t, iw3, onehot, ikt, pm3, dkt, dv_ext, pz3)


def _lift_kernel(ym_ref, yg_ref, yd_ref, wm_ref, wg_ref, wd_ref, ga_ref, gb_ref, gc_ref, o_ref):
    a = jnp.dot(ym_ref[...], wm_ref[...], preferred_element_type=F32)
    b = jnp.dot(yg_ref[...], wg_ref[...], preferred_element_type=F32)
    c = jnp.dot(yd_ref[...], wd_ref[...], preferred_element_type=F32)
    o = ga_ref[...].astype(F32) * a + gb_ref[...].astype(F32) * b + gc_ref[...].astype(F32) * c
    o_ref[...] = o.astype(o_ref.dtype)


def _lift(y_mla, y_gla, y_dsa, w_bm, w_bg, w_bd, gates, *, tm=512, tn=512):
    t = y_mla.shape[0]
    d = w_bm.shape[1]
    nj = d // tn
    row = lambda width: pl.BlockSpec((tm, width), lambda i, j: (i, 0))
    wcol = lambda depth: pl.BlockSpec((depth, tn), lambda i, j: (0, j))
    gate = lambda br: pl.BlockSpec((tm, tn), lambda i, j: (i, br * nj + j))
    return pl.pallas_call(
        _lift_kernel,
        grid=(t // tm, nj),
        in_specs=[row(MLA_WIDTH), row(GLA_WIDTH), row(DSA_WIDTH),
                  wcol(MLA_WIDTH), wcol(GLA_WIDTH), wcol(DSA_WIDTH),
                  gate(0), gate(1), gate(2)],
        out_specs=pl.BlockSpec((tm, tn), lambda i, j: (i, j)),
        out_shape=jax.ShapeDtypeStruct((t, d), BF16),
        compiler_params=_cp(2),
        name="lift_merge",
    )(y_mla, y_gla, y_dsa, w_bm, w_bg, w_bd, gates, gates, gates)


def _out_kernel(m_ref, w_ref, x_ref, gate_ref, o_ref):
    r = jnp.dot(m_ref[0], w_ref[...], preferred_element_type=F32)
    o_ref[0] = x_ref[0] + gate_ref[0] * r


def _out_proj(merged3, w_o, x, gate, *, tm=1024, tn=512):
    b, s, d = x.shape
    tm = min(tm, s)
    return pl.pallas_call(
        _out_kernel,
        grid=(b, s // tm, d // tn),
        in_specs=[pl.BlockSpec((1, tm, d), lambda bi, i, j: (bi, i, 0)),
                  pl.BlockSpec((d, tn), lambda bi, i, j: (0, j)),
                  pl.BlockSpec((1, tm, tn), lambda bi, i, j: (bi, i, j)),
                  pl.BlockSpec((1, 1, tn), lambda bi, i, j: (bi, 0, j))],
        out_specs=pl.BlockSpec((1, tm, tn), lambda bi, i, j: (bi, i, j)),
        out_shape=jax.ShapeDtypeStruct((b, s, d), F32),
        compiler_params=_cp(3),
        name="out_proj",
    )(merged3, w_o, x, gate)


def _main_weight(w_in):
    d = w_in.shape[0]

    def cols(name):
        return w_in[:, _IN_OFF[name]:_IN_OFF[name] + _IN_W[name]]

    def padded(w, width):
        return jnp.concatenate([w, jnp.zeros((d, width - w.shape[1]), w.dtype)], axis=1)

    kr = cols("krope")
    half = MLA_ROPE // 2
    parts = {"kr": padded(kr, LANE),
             "krr": padded(jnp.concatenate([-kr[:, half:], kr[:, :half]], axis=1), LANE),
             "glow": padded(cols("glow"), LANE), "ik": padded(cols("ik"), LANE),
             "iw": padded(cols("iw"), LANE), "pad": jnp.zeros((d, LANE), w_in.dtype)}
    return jnp.concatenate([parts[n] if n in parts else cols(n) for n, _ in _PM_LAYOUT], axis=1).astype(BF16)


def _mla_q_weights(w_uq):
    r = w_uq.shape[0]
    w = w_uq.reshape(r, MLA_HEADS, MLA_NOPE + MLA_ROPE)
    nope, rope = w[..., :MLA_NOPE], w[..., MLA_NOPE:]
    half = MLA_ROPE // 2
    zeros = jnp.zeros((r, MLA_HEADS, LANE - MLA_ROPE), w.dtype)
    w1 = jnp.concatenate([nope, rope, zeros], axis=-1).reshape(r, MLA_HEADS * MLA_QK_PAD)
    rot = jnp.concatenate([-rope[..., half:], rope[..., :half]], axis=-1)
    w2 = jnp.concatenate([rot, zeros], axis=-1).reshape(r, MLA_HEADS * LANE)
    return w1.astype(BF16), w2.astype(BF16)


def _layer(x, c_pad, cos128, sin128, norm_g, w_ada, b_ada, w_in, mla_gq, mla_wuq, mla_gkv, mla_wukv,
           gla_wg2, gla_bg, gla_gout, w_mg, b_mg, w_bm, w_bg, w_bd, w_o):
    b, s, d = x.shape
    t = b * s
    mod = _mm(c_pad, w_ada, b_ada[None, :], tm=c_pad.shape[0], tn=512, out_dtype=F32,
              a_act="silu", name="ada")[:b]
    shift, scale, gate = (mod[:, None, k * d:(k + 1) * d] for k in range(3))
    h = _norm_mod(x, norm_g[None, :], scale, shift)
    h2 = h.reshape(t, d)

    mm_tm = min(1024, t)
    pm = _mm(h2, _main_weight(w_in), None, tm=mm_tm, tn=512, out_dtype=BF16, name="proj_main")
    z0 = _IN_OFF["z_mla"]
    pz = _mm(h2, w_in[:, z0:].astype(BF16), None, tm=mm_tm, tn=512, out_dtype=BF16, act="silu",
             name="proj_gate_paths")
    gates = _mm(h2, w_mg.astype(BF16), b_mg[None, :], tm=mm_tm, tn=512, out_dtype=BF16,
                act="sigmoid", name="merge_gates")
    pm3 = pm.reshape(b, s, PM_WIDTH)
    pz3 = pz.reshape(b, s, d)

    w1, w2 = _mla_q_weights(mla_wuq)
    q = _mla_q(pm3, mla_gq[None, :], w1, w2, cos128, sin128)
    kcat, vt = _mla_kv(pm3, mla_gkv[None, :], mla_wukv.astype(BF16), cos128, sin128)
    y_mla = _mla_attn(q, kcat, vt, pz3)

    w2p = jnp.concatenate([gla_wg2, jnp.zeros((LANE - GLA_GATE_RANK, gla_wg2.shape[1]), gla_wg2.dtype)],
                          axis=0).astype(BF16)
    y_gla = _gla(pm3, pz3, w2p, gla_bg[None, :], gla_gout[None, :])

    nk = s // ATT_TK

    def piece(name, width):
        return pm3[:, :, _PM_OFF[name]:_PM_OFF[name] + width]

    iq_t = piece("iq", IDX_HEADS * IDX_DH).reshape(b, s, IDX_HEADS, IDX_DH).transpose(0, 2, 1, 3)
    iw3 = piece("iw", IDX_HEADS)
    ikt = piece("ik", IDX_DH).reshape(b, nk, ATT_TK, IDX_DH).transpose(0, 1, 3, 2)
    kv5 = (b, nk, ATT_TK, DSA_KV_HEADS, DSA_DH)
    dkt = piece("dk", DSA_KV_HEADS * DSA_DH).reshape(kv5).transpose(0, 3, 1, 4, 2)
    dv = piece("dv", DSA_KV_HEADS * DSA_DH).reshape(kv5).transpose(0, 3, 1, 2, 4)
    dv_ext = jnp.concatenate([dv, jnp.ones_like(dv)], axis=-1)
    y_dsa = _dsa(pm3, pz3, iq_t, iw3, ikt, dkt, dv_ext, min(IDX_TOPK, s // 4))

    merged = _lift(y_mla.reshape(t, MLA_WIDTH), y_gla.reshape(t, GLA_WIDTH), y_dsa.reshape(t, DSA_WIDTH),
                   w_bm.astype(BF16), w_bg.astype(BF16), w_bd.astype(BF16), gates)
    return _out_proj(merged.reshape(b, s, d), w_o.astype(BF16), x, gate)


def kernel(x, c, positions, norm_g, w_ada, b_ada, w_in, mla_gq, mla_wuq, mla_gkv, mla_wukv, gla_wg2,
           gla_bg, gla_gout, w_mg, b_mg, w_bm, w_bg, w_bd, w_o, final_g):
    b = x.shape[0]
    cos128, sin128 = _rope_tables(positions)
    c_pad = jnp.concatenate([c, jnp.zeros((8 - b, c.shape[1]), c.dtype)], axis=0)
    for l in range(DEPTH):
        x = _layer(x, c_pad, cos128, sin128, norm_g[l], w_ada[l], b_ada[l], w_in[l], mla_gq[l],
                   mla_wuq[l], mla_gkv[l], mla_wukv[l], gla_wg2[l], gla_bg[l], gla_gout[l],
                   w_mg[l], b_mg[l], w_bm[l], w_bg[l], w_bd[l], w_o[l])
    return _final_norm(x, final_g[None, :])
```

```python
import functools

import jax
import jax.numpy as jnp
import numpy as np
from jax import lax
from jax.experimental import pallas as pl
from jax.experimental.pallas import tpu as pltpu

D_MODEL = 4096
DEPTH = 2
MLA_HEADS = 16
MLA_Q_RANK = 768
MLA_KV_RANK = 512
MLA_NOPE = 128
MLA_ROPE = 64
MLA_V = 128
ROPE_THETA = 10000.0
GLA_HEADS = 4
GLA_DK = 128
GLA_DV = 256
GLA_GATE_RANK = 16
GLA_GATE_NORM = 16.0
GLA_CHUNK = 64
DSA_HEADS = 8
DSA_KV_HEADS = 2
DSA_DH = 128
IDX_HEADS = 32
IDX_DH = 64
IDX_TOPK = 256
NORM_EPS = 1e-6
NEG = -1e30

MLA_WIDTH = MLA_HEADS * MLA_V
GLA_WIDTH = GLA_HEADS * GLA_DV
DSA_WIDTH = DSA_HEADS * DSA_DH
DSA_REP = DSA_HEADS // DSA_KV_HEADS

IN_SPLITS = (
    MLA_Q_RANK, MLA_KV_RANK, MLA_ROPE,
    GLA_HEADS * GLA_DK, GLA_HEADS * GLA_DK, GLA_WIDTH, GLA_GATE_RANK,
    DSA_WIDTH, DSA_KV_HEADS * DSA_DH, DSA_KV_HEADS * DSA_DH,
    IDX_HEADS * IDX_DH, IDX_DH, IDX_HEADS,
    MLA_WIDTH, GLA_WIDTH, DSA_WIDTH,
)
_IN_NAMES = ("cq", "ckv", "krope", "gq", "gk", "gv", "glow", "dq", "dk", "dv",
             "iq", "ik", "iw", "z_mla", "z_gla", "z_dsa")
_IN_OFF = dict(zip(_IN_NAMES, np.concatenate([[0], np.cumsum(IN_SPLITS)[:-1]]).tolist()))
_IN_W = dict(zip(_IN_NAMES, IN_SPLITS))

LANE = 128
MLA_QK_PAD = 256
ATT_TQ = 1024
ATT_TK = 512
MLA_HEADS_PER_STEP = 4
LOG2E = 1.4426950408889634
DSA_TQ = 128
VMEM_LIMIT = 56 * 1024 * 1024

_PM_LAYOUT = (("dq", 1024), ("gv", 1024), ("iq", 2048), ("gq", 512), ("cq", 768), ("dk", 256),
              ("ckv", 512), ("gk", 512), ("dv", 256), ("kr", 128), ("krr", 128), ("glow", 128),
              ("ik", 128), ("iw", 128), ("pad", 128))
_PM_OFF = {}
_o = 0
for _n, _w in _PM_LAYOUT:
    assert _o % _w == 0
    _PM_OFF[_n] = _o
    _o += _w
PM_WIDTH = _o

BF16 = jnp.bfloat16
F32 = jnp.float32
INT_MIN = -2 ** 31


def _cp(n_axes, flags=None):
    return pltpu.CompilerParams(dimension_semantics=("arbitrary",) * n_axes,
                                vmem_limit_bytes=VMEM_LIMIT, flags=flags)


def _nt(a, b):
    return lax.dot_general(a, b, (((1,), (1,)), ((), ())), preferred_element_type=F32)


def _rms(x, g):
    return x * lax.rsqrt(jnp.mean(x * x, axis=-1, keepdims=True) + NORM_EPS) * g


def _mm_kernel(*refs, a_act, act, has_bias):
    if has_bias:
        a_ref, w_ref, b_ref, o_ref = refs
    else:
        a_ref, w_ref, o_ref = refs
    a = a_ref[...]
    if a_act == "silu":
        a = a.astype(F32)
        a = a * jax.nn.sigmoid(a)
    r = jnp.dot(a.astype(BF16), w_ref[...].astype(BF16), preferred_element_type=F32)
    if has_bias:
        r = r + b_ref[...]
    if act == "sigmoid":
        r = jax.nn.sigmoid(r)
    elif act == "silu":
        r = r * jax.nn.sigmoid(r)
    o_ref[...] = r.astype(o_ref.dtype)


def _mm(a, w, bias, *, tm, tn, out_dtype, a_act=None, act=None, name):
    m, k = a.shape
    n = w.shape[1]
    in_specs = [pl.BlockSpec((tm, k), lambda i, j: (i, 0)),
                pl.BlockSpec((k, tn), lambda i, j: (0, j))]
    args = [a, w]
    if bias is not None:
        in_specs.append(pl.BlockSpec((1, tn), lambda i, j: (0, j)))
        args.append(bias)
    return pl.pallas_call(
        functools.partial(_mm_kernel, a_act=a_act, act=act, has_bias=bias is not None),
        grid=(m // tm, n // tn),
        in_specs=in_specs,
        out_specs=pl.BlockSpec((tm, tn), lambda i, j: (i, j)),
        out_shape=jax.ShapeDtypeStruct((m, n), out_dtype),
        compiler_params=_cp(2),
        name=name,
    )(*args)


def _norm_mod_kernel(x_ref, g_ref, sc_ref, sh_ref, o_ref):
    y = _rms(x_ref[0], g_ref[...])
    o_ref[0] = (y * (1.0 + sc_ref[0]) + sh_ref[0]).astype(o_ref.dtype)


def _norm_mod(x, g, scale, shift, *, tm=256):
    b, s, d = x.shape
    return pl.pallas_call(
        _norm_mod_kernel,
        grid=(b, s // tm),
        in_specs=[pl.BlockSpec((1, tm, d), lambda bi, i: (bi, i, 0)),
                  pl.BlockSpec((1, d), lambda bi, i: (0, 0)),
                  pl.BlockSpec((1, 1, d), lambda bi, i: (bi, 0, 0)),
                  pl.BlockSpec((1, 1, d), lambda bi, i: (bi, 0, 0))],
        out_specs=pl.BlockSpec((1, tm, d), lambda bi, i: (bi, i, 0)),
        out_shape=jax.ShapeDtypeStruct((b, s, d), BF16),
        compiler_params=_cp(2),
        name="norm_mod",
    )(x, g, scale, shift)


def _final_norm_kernel(x_ref, g_ref, o_ref):
    o_ref[0] = _rms(x_ref[0], g_ref[...])


def _final_norm(x, g, *, tm=256):
    b, s, d = x.shape
    return pl.pallas_call(
        _final_norm_kernel,
        grid=(b, s // tm),
        in_specs=[pl.BlockSpec((1, tm, d), lambda bi, i: (bi, i, 0)),
                  pl.BlockSpec((1, d), lambda bi, i: (0, 0))],
        out_specs=pl.BlockSpec((1, tm, d), lambda bi, i: (bi, i, 0)),
        out_shape=jax.ShapeDtypeStruct((b, s, d), F32),
        compiler_params=_cp(2),
        name="final_norm",
    )(x, g)


def _rope_kernel(pos_ref, inv_ref, cos_ref, sin_ref):
    ang = pos_ref[0] * inv_ref[...]
    live = lax.broadcasted_iota(jnp.int32, ang.shape, 1) < MLA_ROPE
    cos_ref[0] = jnp.where(live, jnp.cos(ang), 0.0)
    sin_ref[0] = jnp.where(live, jnp.sin(ang), 0.0)


def _rope_tables(positions, *, tm=512):
    b, s = positions.shape
    inv = ROPE_THETA ** (-jnp.arange(0, MLA_ROPE, 2, dtype=F32) / MLA_ROPE)
    inv128 = jnp.concatenate([inv, inv, jnp.zeros((LANE - MLA_ROPE,), F32)])[None, :]
    pos128 = jnp.broadcast_to(positions.astype(F32)[:, :, None], (b, s, LANE))
    spec = pl.BlockSpec((1, tm, LANE), lambda bi, i: (bi, i, 0))
    return pl.pallas_call(
        _rope_kernel,
        grid=(b, s // tm),
        in_specs=[spec, pl.BlockSpec((1, LANE), lambda bi, i: (0, 0))],
        out_specs=[spec, spec],
        out_shape=[jax.ShapeDtypeStruct((b, s, LANE), F32)] * 2,
        compiler_params=_cp(2),
        name="rope_tables",
    )(pos128, inv128)


def _mla_q_kernel(cq_ref, g_ref, w1_ref, w2_ref, cos_ref, sin_ref, o_ref, an_ref, *, scale):
    @pl.when(pl.program_id(2) == 0)
    def _():
        an_ref[...] = _rms(cq_ref[0].astype(F32), g_ref[...]).astype(BF16)

    a = an_ref[...]
    a1 = jnp.dot(a, w1_ref[...], preferred_element_type=F32)
    a2 = jnp.dot(a, w2_ref[...], preferred_element_type=F32)
    cos, sin = cos_ref[0], sin_ref[0]
    for u in range(MLA_HEADS_PER_STEP):
        c0 = u * MLA_QK_PAD
        rope = a1[:, c0 + MLA_NOPE:c0 + MLA_QK_PAD] * cos + a2[:, u * LANE:(u + 1) * LANE] * sin
        o_ref[0, :, c0:c0 + MLA_NOPE] = (a1[:, c0:c0 + MLA_NOPE] * scale).astype(BF16)
        o_ref[0, :, c0 + MLA_NOPE:c0 + MLA_QK_PAD] = (rope * scale).astype(BF16)


def _mla_q(pm3, g_q, w1, w2, cos128, sin128, *, tm=512):
    b, s, _ = pm3.shape
    hp = MLA_HEADS_PER_STEP
    cq_blk = _PM_OFF["cq"] // MLA_Q_RANK
    return pl.pallas_call(
        functools.partial(_mla_q_kernel, scale=(MLA_NOPE + MLA_ROPE) ** -0.5 * LOG2E),
        grid=(b, s // tm, MLA_HEADS // hp),
        in_specs=[pl.BlockSpec((1, tm, MLA_Q_RANK), lambda bi, i, h: (bi, i, cq_blk)),
                  pl.BlockSpec((1, MLA_Q_RANK), lambda bi, i, h: (0, 0)),
                  pl.BlockSpec((MLA_Q_RANK, hp * MLA_QK_PAD), lambda bi, i, h: (0, h)),
                  pl.BlockSpec((MLA_Q_RANK, hp * LANE), lambda bi, i, h: (0, h)),
                  pl.BlockSpec((1, tm, LANE), lambda bi, i, h: (bi, i, 0)),
                  pl.BlockSpec((1, tm, LANE), lambda bi, i, h: (bi, i, 0))],
        out_specs=pl.BlockSpec((1, tm, hp * MLA_QK_PAD), lambda bi, i, h: (bi, i, h)),
        out_shape=jax.ShapeDtypeStruct((b, s, MLA_HEADS * MLA_QK_PAD), BF16),
        scratch_shapes=[pltpu.VMEM((tm, MLA_Q_RANK), BF16)],
        compiler_params=_cp(3),
        name="mla_q",
    )(pm3, g_q, w1, w2, cos128, sin128)


def _mla_kv_kernel(ckv_ref, g_ref, w_ref, kr_ref, krr_ref, cos_ref, sin_ref, k_ref, v_ref, an_ref, krt_ref):
    @pl.when(pl.program_id(2) == 0)
    def _():
        an_ref[...] = _rms(ckv_ref[0].astype(F32), g_ref[...]).astype(BF16)
        kr = kr_ref[0].astype(F32) * cos_ref[0] + krr_ref[0].astype(F32) * sin_ref[0]
        krt_ref[...] = kr.T.astype(BF16)

    acc = jnp.dot(an_ref[...], w_ref[...], preferred_element_type=F32)
    ones = jnp.ones((acc.shape[0], MLA_V), BF16)
    for u in range(MLA_HEADS_PER_STEP):
        c0 = u * (MLA_NOPE + MLA_V)
        k_ref[0, u, 0, :MLA_NOPE, :] = acc[:, c0:c0 + MLA_NOPE].T.astype(BF16)
        k_ref[0, u, 0, MLA_NOPE:, :] = krt_ref[...]
        v_ref[0, u, 0, :, :MLA_V] = acc[:, c0 + MLA_NOPE:c0 + MLA_NOPE + MLA_V].astype(BF16)
        v_ref[0, u, 0, :, MLA_V:] = ones


def _mla_kv(pm3, g_kv, w_ukv, cos128, sin128):
    b, s, _ = pm3.shape
    tm = ATT_TK
    hp = MLA_HEADS_PER_STEP
    ckv_blk = _PM_OFF["ckv"] // MLA_KV_RANK
    kr_blk = _PM_OFF["kr"] // LANE
    krr_blk = _PM_OFF["krr"] // LANE
    return pl.pallas_call(
        _mla_kv_kernel,
        grid=(b, s // tm, MLA_HEADS // hp),
        in_specs=[pl.BlockSpec((1, tm, MLA_KV_RANK), lambda bi, i, h: (bi, i, ckv_blk)),
                  pl.BlockSpec((1, MLA_KV_RANK), lambda bi, i, h: (0, 0)),
                  pl.BlockSpec((MLA_KV_RANK, hp * (MLA_NOPE + MLA_V)), lambda bi, i, h: (0, h)),
                  pl.BlockSpec((1, tm, LANE), lambda bi, i, h: (bi, i, kr_blk)),
                  pl.BlockSpec((1, tm, LANE), lambda bi, i, h: (bi, i, krr_blk)),
                  pl.BlockSpec((1, tm, LANE), lambda bi, i, h: (bi, i, 0)),
                  pl.BlockSpec((1, tm, LANE), lambda bi, i, h: (bi, i, 0))],
        out_specs=[pl.BlockSpec((1, hp, 1, MLA_QK_PAD, tm), lambda bi, i, h: (bi, h, i, 0, 0)),
                   pl.BlockSpec((1, hp, 1, tm, 2 * MLA_V), lambda bi, i, h: (bi, h, i, 0, 0))],
        out_shape=[jax.ShapeDtypeStruct((b, MLA_HEADS, s // tm, MLA_QK_PAD, tm), BF16),
                   jax.ShapeDtypeStruct((b, MLA_HEADS, s // tm, tm, 2 * MLA_V), BF16)],
        scratch_shapes=[pltpu.VMEM((tm, MLA_KV_RANK), BF16), pltpu.VMEM((LANE, tm), BF16)],
        compiler_params=_cp(3),
        name="mla_kv",
    )(pm3, g_kv, w_ukv, pm3, pm3, cos128, sin128)


def _sm_init(acc_ref, m_ref):
    m_ref[...] = jnp.full(m_ref.shape, NEG, F32)
    acc_ref[...] = jnp.zeros(acc_ref.shape, F32)


def _sm_step(s, v_ext, acc_ref, m_ref):
    m_old = m_ref[...]
    m_new = jnp.maximum(m_old, jnp.max(s, axis=-1, keepdims=True))
    alpha = jnp.exp2(m_old - m_new)
    p = jnp.exp2(s - jnp.tile(m_new, (1, s.shape[1] // LANE)))
    pv = jnp.dot(p.astype(BF16), v_ext, preferred_element_type=F32)
    acc_ref[...] = jnp.tile(alpha, (1, acc_ref.shape[1] // LANE)) * acc_ref[...] + pv
    m_ref[...] = m_new


def _sm_result(acc_ref, d):
    acc = acc_ref[...]
    return acc[:, :d] / acc[:, d:]


def _mla_attn_kernel(q_ref, kt_ref, v_ref, z_ref, o_ref, s_ref, acc_ref, m_ref):
    i = pl.program_id(2)
    t = ATT_TK
    assert ATT_TQ == 2 * t
    chains = range(2)
    qpos = lax.broadcasted_iota(jnp.int32, (t, t), 0)
    kpos = lax.broadcasted_iota(jnp.int32, (t, t), 1)

    def scores(u, j):
        return jnp.dot(q_ref[0, u * t:(u + 1) * t, :], kt_ref[0, 0, j], preferred_element_type=F32)

    def consume(u, buf, j, diagonal):
        s = s_ref[buf, u]
        if diagonal:
            s = jnp.where(kpos <= qpos, s, NEG)
        _sm_step(s, v_ref[0, 0, j], acc_ref.at[u], m_ref.at[u])

    for u in chains:
        _sm_init(acc_ref.at[u], m_ref.at[u])
        s_ref[0, u] = scores(u, 0)

    def block_pair(p, carry):
        j = 2 * p
        for u in chains:
            consume(u, 0, j, False)
            s_ref[1, u] = scores(u, j + 1)
        for u in chains:
            consume(u, 1, j + 1, False)
            s_ref[0, u] = scores(u, j + 2)
        return carry

    lax.fori_loop(0, i, block_pair, 0)
    first = 2 * i
    consume(0, 0, first, True)
    consume(1, 0, first, False)
    s_ref[1, 1] = scores(1, first + 1)
    consume(1, 1, first + 1, True)
    for u in chains:
        rows = slice(u * t, (u + 1) * t)
        o_ref[0, rows, :] = (_sm_result(acc_ref.at[u], MLA_V) * z_ref[0, rows, :].astype(F32)).astype(o_ref.dtype)


def _mla_attn(q, kt, v_ext, pz3):
    b, s, _ = q.shape
    nk = s // ATT_TK
    return pl.pallas_call(
        _mla_attn_kernel,
        grid=(b, MLA_HEADS, s // ATT_TQ),
        in_specs=[pl.BlockSpec((1, ATT_TQ, MLA_QK_PAD), lambda bi, h, i: (bi, i, h)),
                  pl.BlockSpec((1, 1, nk, MLA_QK_PAD, ATT_TK), lambda bi, h, i: (bi, h, 0, 0, 0)),
                  pl.BlockSpec((1, 1, nk, ATT_TK, 2 * MLA_V), lambda bi, h, i: (bi, h, 0, 0, 0)),
                  pl.BlockSpec((1, ATT_TQ, MLA_V), lambda bi, h, i: (bi, i, h))],
        out_specs=pl.BlockSpec((1, ATT_TQ, MLA_V), lambda bi, h, i: (bi, i, h)),
        out_shape=jax.ShapeDtypeStruct((b, s, MLA_WIDTH), BF16),
        scratch_shapes=[pltpu.VMEM((2, 2, ATT_TK, ATT_TK), F32),
                        pltpu.VMEM((2, ATT_TK, 2 * MLA_V), F32),
                        pltpu.VMEM((2, ATT_TK, LANE), F32)],
        compiler_params=_cp(3),
        name="mla_attn",
    )(q, kt, v_ext, pz3)


def _log_sigmoid(x):
    return jnp.minimum(x, 0.0) - jnp.log(1.0 + jnp.exp(-jnp.abs(x)))


def _gla_kernel(q_ref, k_ref, v_ref, gl_ref, z_ref, w2_ref, bg_ref, go_ref, o_ref, st_ref, *, nb):
    @pl.when(pl.program_id(0) == 0)
    def _():
        st_ref[...] = jnp.zeros(st_ref.shape, F32)

    c = GLA_CHUNK
    row = lax.broadcasted_iota(jnp.int32, (c, c), 0)
    col = lax.broadcasted_iota(jnp.int32, (c, c), 1)
    tril = row >= col
    tril_b = jnp.where(tril, 1.0, 0.0).astype(BF16)
    for b in range(nb):
        gl = gl_ref[b]
        for h in range(GLA_HEADS):
            ks = slice(h * GLA_DK, (h + 1) * GLA_DK)
            vs = slice(h * GLA_DV, (h + 1) * GLA_DV)
            q = q_ref[b, :, ks].astype(F32)
            k = k_ref[b, :, ks].astype(F32)
            v = v_ref[b, :, vs]
            pre = jnp.dot(gl, w2_ref[:, ks], preferred_element_type=F32) + bg_ref[:, ks]
            glog = _log_sigmoid(pre) / GLA_GATE_NORM
            g1 = glog.astype(BF16)
            r1 = glog - g1.astype(F32)
            g2 = r1.astype(BF16)
            g3 = (r1 - g2.astype(F32)).astype(BF16)
            bc = (jnp.dot(tril_b, g1, preferred_element_type=F32)
                  + jnp.dot(tril_b, g2, preferred_element_type=F32)
                  + jnp.dot(tril_b, g3, preferred_element_type=F32))
            b_last = bc[c - 1:c, :]
            q_dec = (q * GLA_DK ** -0.5 * jnp.exp(bc)).astype(BF16)
            k_inv = (k * jnp.exp(-bc)).astype(BF16)
            k_end = (k * jnp.exp(b_last - bc)).astype(BF16)
            decay = jnp.exp(b_last)
            attn = jnp.where(tril, _nt(q_dec, k_inv), 0.0).astype(BF16)
            st = st_ref[b * GLA_HEADS + h]
            o = jnp.dot(attn, v, preferred_element_type=F32) + _nt(q_dec, st.astype(BF16))
            upd = lax.dot_general(v, k_end, (((0,), (0,)), ((), ())), preferred_element_type=F32)
            st_ref[b * GLA_HEADS + h] = st * decay + upd
            on = _rms(o, go_ref[...])
            o_ref[b, :, vs] = (on * z_ref[b, :, vs].astype(F32)).astype(o_ref.dtype)


def _gla(pm3, pz3, w2p, bg, gout):
    b, s, _ = pm3.shape
    c = GLA_CHUNK
    qw = GLA_HEADS * GLA_DK
    q_blk, k_blk = _PM_OFF["gq"] // qw, _PM_OFF["gk"] // qw
    v_blk, gl_blk = _PM_OFF["gv"] // GLA_WIDTH, _PM_OFF["glow"] // LANE
    z_blk = MLA_WIDTH // GLA_WIDTH
    return pl.pallas_call(
        functools.partial(_gla_kernel, nb=b),
        grid=(s // c,),
        in_specs=[pl.BlockSpec((b, c, qw), lambda i: (0, i, q_blk)),
                  pl.BlockSpec((b, c, qw), lambda i: (0, i, k_blk)),
                  pl.BlockSpec((b, c, GLA_WIDTH), lambda i: (0, i, v_blk)),
                  pl.BlockSpec((b, c, LANE), lambda i: (0, i, gl_blk)),
                  pl.BlockSpec((b, c, GLA_WIDTH), lambda i: (0, i, z_blk)),
                  pl.BlockSpec((LANE, qw), lambda i: (0, 0)),
                  pl.BlockSpec((1, qw), lambda i: (0, 0)),
                  pl.BlockSpec((1, GLA_DV), lambda i: (0, 0))],
        out_specs=pl.BlockSpec((b, c, GLA_WIDTH), lambda i: (0, i, 0)),
        out_shape=jax.ShapeDtypeStruct((b, s, GLA_WIDTH), BF16),
        scratch_shapes=[pltpu.VMEM((b * GLA_HEADS, GLA_DV, GLA_DK), F32)],
        compiler_params=_cp(1),
        name="gla",
    )(pm3, pm3, pm3, pm3, pz3, w2p, bg, gout)


def _dsa_kernel(iq_ref, iw_ref, onehot_ref, ikt_ref, q_ref, kt_ref, v_ref, z_ref, o_ref,
                key_ref, keyt_ref, wb_ref, qs_ref, s_ref, acc_ref, m_ref, *, top_k):
    tq, tk = DSA_TQ, ATT_TK
    nk = key_ref.shape[0]
    lanes = tk // LANE
    i = pl.program_id(1)
    nkb = (i * tq + tq - 1) // tk + 1
    qpos = i * tq + lax.broadcasted_iota(jnp.int32, (tq, tk), 0)
    heads_per_dot = 4

    wb = jnp.dot(iw_ref[0], onehot_ref[...], preferred_element_type=F32)
    for h in range(IDX_HEADS):
        wb_ref[h] = wb[:, h * LANE:(h + 1) * LANE] * (IDX_HEADS ** -0.5 * IDX_DH ** -0.5)

    def sortable(x):
        bits = lax.bitcast_convert_type(x, jnp.int32)
        return bits ^ ((bits >> 31) & 0x7FFFFFFF)

    qpos_t = i * tq + lax.broadcasted_iota(jnp.int32, (tk, tq), 1)

    def score_block(j, carry):
        ikt = ikt_ref[0, j]
        sc = jnp.zeros((tq, tk), F32)
        for hg in range(IDX_HEADS // heads_per_dot):
            iq = iq_ref[0, hg * heads_per_dot:(hg + 1) * heads_per_dot].reshape(heads_per_dot * tq, IDX_DH)
            r = jnp.dot(iq, ikt, preferred_element_type=F32)
            for hh in range(heads_per_dot):
                wrow = jnp.tile(wb_ref[hg * heads_per_dot + hh], (1, lanes))
                sc = sc + jnp.maximum(r[hh * tq:(hh + 1) * tq, :], 0.0) * wrow
        kpos = j * tk + lax.broadcasted_iota(jnp.int32, (tq, tk), 1)
        key_ref[j] = jnp.where(kpos <= qpos, sortable(sc), INT_MIN)
        kpos_t = j * tk + lax.broadcasted_iota(jnp.int32, (tk, tq), 0)
        keyt_ref[j] = jnp.where(kpos_t <= qpos_t, sortable(sc.T), INT_MIN)
        return carry

    lax.fori_loop(0, nkb, score_block, 0)

    def count(pred):
        def body(j, acc):
            hit = jnp.where(pred(keyt_ref[j], j), 1, 0).astype(jnp.int32)
            return acc + hit.reshape(tk // 8, 8, tq).sum(axis=0)
        acc = lax.fori_loop(0, nkb, body, jnp.zeros((8, tq), jnp.int32))
        return acc.sum(axis=0, keepdims=True)

    def count_ge(cand):
        return count(lambda k, j: k >= cand)

    zero = jnp.zeros((1, tq), jnp.int32)
    c0 = count_ge(zero)
    thr = jnp.where(c0 >= top_k, zero, INT_MIN)
    n_ge = jnp.where(c0 >= top_k, c0, nkb * tk)

    def bit_step(it, carry):
        thr, n_ge = carry
        cand = thr | lax.shift_left(jnp.int32(1), 30 - it)
        c = count_ge(cand)
        return jnp.where(c >= top_k, cand, thr), jnp.where(c >= top_k, c, n_ge)

    thr, n_ge = lax.fori_loop(0, 31, bit_step, (thr, n_ge))

    n_gt = count(lambda k, j: k > thr)
    n_take = top_k - n_gt
    overflow = (thr > INT_MIN) & (n_ge - n_gt > n_take)

    @pl.when(jnp.max(jnp.where(overflow, 1, 0)) > 0)
    def _():
        rows = lax.broadcasted_iota(jnp.int32, (tk, tq), 0)
        bound = jnp.zeros((1, tq), jnp.int32)
        for bit in range((nk * tk - 1).bit_length() - 1, -1, -1):
            cand = bound | (1 << bit)
            below = count(lambda k, j: (k == thr) & (j * tk + rows < cand))
            bound = jnp.where(below < n_take, cand, bound)
        bound = jnp.where(overflow, bound, nk * tk)
        thr_c = jnp.tile(jnp.broadcast_to(thr, (tq, tq)).T, (1, lanes))
        bound_c = jnp.tile(jnp.broadcast_to(bound, (tq, tq)).T, (1, lanes))

        def demote(j, carry):
            k = key_ref[j]
            kpos = j * tk + lax.broadcasted_iota(jnp.int32, (tq, tk), 1)
            key_ref[j] = jnp.where((k == thr_c) & (kpos > bound_c), INT_MIN, k)
            return carry

        lax.fori_loop(0, nkb, demote, 0)

    thr = jnp.maximum(thr, INT_MIN + 1)
    thr_w = jnp.tile(jnp.broadcast_to(thr, (tq, tq)).T, (1, lanes))

    scale = DSA_DH ** -0.5 * LOG2E
    groups = range(DSA_KV_HEADS)

    def scores(g, j):
        return jnp.dot(qs_ref[g], kt_ref[0, g, j], preferred_element_type=F32)

    def consume(g, buf, j):
        sel = key_ref[j] >= thr_w
        s = s_ref[buf, g]
        s = jnp.concatenate([jnp.where(sel, s[r * tq:(r + 1) * tq, :], NEG) for r in range(DSA_REP)], axis=0)
        _sm_step(s, v_ref[0, g, j], acc_ref.at[g], m_ref.at[g])

    for g in groups:
        for r in range(DSA_REP):
            cs = slice((g * DSA_REP + r) * DSA_DH, (g * DSA_REP + r + 1) * DSA_DH)
            qs_ref[g, r * tq:(r + 1) * tq, :] = (q_ref[0, :, cs].astype(F32) * scale).astype(BF16)
        _sm_init(acc_ref.at[g], m_ref.at[g])
    for g in groups:
        s_ref[0, g] = scores(g, 0)

    def block_pair(p, carry):
        j = 2 * p
        for g in groups:
            consume(g, 0, j)
            s_ref[1, g] = scores(g, j + 1)
        for g in groups:
            consume(g, 1, j + 1)
            s_ref[0, g] = scores(g, jnp.minimum(j + 2, nk - 1))
        return carry

    lax.fori_loop(0, nkb // 2, block_pair, 0)

    @pl.when(nkb % 2 == 1)
    def _():
        for g in groups:
            consume(g, 0, nkb - 1)

    for g in groups:
        res = _sm_result(acc_ref.at[g], DSA_DH)
        for r in range(DSA_REP):
            cs = slice((g * DSA_REP + r) * DSA_DH, (g * DSA_REP + r + 1) * DSA_DH)
            o_ref[0, :, cs] = (res[r * tq:(r + 1) * tq, :] * z_ref[0, :, cs].astype(F32)).astype(o_ref.dtype)


def _dsa(pm3, pz3, iq_t, iw3, ikt, dkt, dv_ext, top_k):
    b, s, _ = pm3.shape
    tq, tk = DSA_TQ, ATT_TK
    nk = s // tk
    q_blk = _PM_OFF["dq"] // DSA_WIDTH
    z_blk = (MLA_WIDTH + GLA_WIDTH) // DSA_WIDTH
    rows = DSA_REP * tq
    assert tq == LANE
    onehot = jnp.repeat(jnp.eye(IDX_HEADS, dtype=BF16), LANE, axis=1)
    return pl.pallas_call(
        functools.partial(_dsa_kernel, top_k=top_k),
        grid=(b, s // tq),
        in_specs=[pl.BlockSpec((1, IDX_HEADS, tq, IDX_DH), lambda bi, i: (bi, 0, i, 0)),
                  pl.BlockSpec((1, tq, IDX_HEADS), lambda bi, i: (bi, i, 0)),
                  pl.BlockSpec((IDX_HEADS, IDX_HEADS * LANE), lambda bi, i: (0, 0)),
                  pl.BlockSpec((1, nk, IDX_DH, tk), lambda bi, i: (bi, 0, 0, 0)),
                  pl.BlockSpec((1, tq, DSA_WIDTH), lambda bi, i: (bi, i, q_blk)),
                  pl.BlockSpec((1, DSA_KV_HEADS, nk, DSA_DH, tk), lambda bi, i: (bi, 0, 0, 0, 0)),
                  pl.BlockSpec((1, DSA_KV_HEADS, nk, tk, 2 * DSA_DH), lambda bi, i: (bi, 0, 0, 0, 0)),
                  pl.BlockSpec((1, tq, DSA_WIDTH), lambda bi, i: (bi, i, z_blk))],
        out_specs=pl.BlockSpec((1, tq, DSA_WIDTH), lambda bi, i: (bi, i, 0)),
        out_shape=jax.ShapeDtypeStruct((b, s, DSA_WIDTH), BF16),
        scratch_shapes=[pltpu.VMEM((nk, tq, tk), jnp.int32),
                        pltpu.VMEM((nk, tk, tq), jnp.int32),
                        pltpu.VMEM((IDX_HEADS, tq, LANE), F32),
                        pltpu.VMEM((DSA_KV_HEADS, rows, DSA_DH), BF16),
                        pltpu.VMEM((2, DSA_KV_HEADS, rows, tk), F32),
                        pltpu.VMEM((DSA_KV_HEADS, rows, 2 * DSA_DH), F32),
                        pltpu.VMEM((DSA_KV_HEADS, rows, LANE), F32)],
        compiler_params=_cp(2),
        name="dsa",
    )(iq_t, iw3, onehot, ikt, pm3, dkt, dv_ext, pz3)


def _lift_kernel(ym_ref, yg_ref, yd_ref, wm_ref, wg_ref, wd_ref, ga_ref, gb_ref, gc_ref, o_ref):
    a = jnp.dot(ym_ref[...], wm_ref[...], preferred_element_type=F32)
    b = jnp.dot(yg_ref[...], wg_ref[...], preferred_element_type=F32)
    c = jnp.dot(yd_ref[...], wd_ref[...], preferred_element_type=F32)
    o = ga_ref[...].astype(F32) * a + gb_ref[...].astype(F32) * b + gc_ref[...].astype(F32) * c
    o_ref[...] = o.astype(o_ref.dtype)


def _lift(y_mla, y_gla, y_dsa, w_bm, w_bg, w_bd, gates, *, tm=512, tn=512):
    t = y_mla.shape[0]
    d = w_bm.shape[1]
    nj = d // tn
    row = lambda width: pl.BlockSpec((tm, width), lambda i, j: (i, 0))
    wcol = lambda depth: pl.BlockSpec((depth, tn), lambda i, j: (0, j))
    gate = lambda br: pl.BlockSpec((tm, tn), lambda i, j: (i, br * nj + j))
    return pl.pallas_call(
        _lift_kernel,
        grid=(t // tm, nj),
        in_specs=[row(MLA_WIDTH), row(GLA_WIDTH), row(DSA_WIDTH),
                  wcol(MLA_WIDTH), wcol(GLA_WIDTH), wcol(DSA_WIDTH),
                  gate(0), gate(1), gate(2)],
        out_specs=pl.BlockSpec((tm, tn), lambda i, j: (i, j)),
        out_shape=jax.ShapeDtypeStruct((t, d), BF16),
        compiler_params=_cp(2),
        name="lift_merge",
    )(y_mla, y_gla, y_dsa, w_bm, w_bg, w_bd, gates, gates, gates)


def _out_kernel(m_ref, w_ref, x_ref, gate_ref, o_ref):
    r = jnp.dot(m_ref[0], w_ref[...], preferred_element_type=F32)
    o_ref[0] = x_ref[0] + gate_ref[0] * r


def _out_proj(merged3, w_o, x, gate, *, tm=1024, tn=512):
    b, s, d = x.shape
    tm = min(tm, s)
    return pl.pallas_call(
        _out_kernel,
        grid=(b, s // tm, d // tn),
        in_specs=[pl.BlockSpec((1, tm, d), lambda bi, i, j: (bi, i, 0)),
                  pl.BlockSpec((d, tn), lambda bi, i, j: (0, j)),
                  pl.BlockSpec((1, tm, tn), lambda bi, i, j: (bi, i, j)),
                  pl.BlockSpec((1, 1, tn), lambda bi, i, j: (bi, 0, j))],
        out_specs=pl.BlockSpec((1, tm, tn), lambda bi, i, j: (bi, i, j)),
        out_shape=jax.ShapeDtypeStruct((b, s, d), F32),
        compiler_params=_cp(3),
        name="out_proj",
    )(merged3, w_o, x, gate)


def _main_weight(w_in):
    d = w_in.shape[0]

    def cols(name):
        return w_in[:, _IN_OFF[name]:_IN_OFF[name] + _IN_W[name]]

    def padded(w, width):
        return jnp.concatenate([w, jnp.zeros((d, width - w.shape[1]), w.dtype)], axis=1)

    kr = cols("krope")
    half = MLA_ROPE // 2
    parts = {"kr": padded(kr, LANE),
             "krr": padded(jnp.concatenate([-kr[:, half:], kr[:, :half]], axis=1), LANE),
             "glow": padded(cols("glow"), LANE), "ik": padded(cols("ik"), LANE),
             "iw": padded(cols("iw"), LANE), "pad": jnp.zeros((d, LANE), w_in.dtype)}
    return jnp.concatenate([parts[n] if n in parts else cols(n) for n, _ in _PM_LAYOUT], axis=1).astype(BF16)


def _mla_q_weights(w_uq):
    r = w_uq.shape[0]
    w = w_uq.reshape(r, MLA_HEADS, MLA_NOPE + MLA_ROPE)
    nope, rope = w[..., :MLA_NOPE], w[..., MLA_NOPE:]
    half = MLA_ROPE // 2
    zeros = jnp.zeros((r, MLA_HEADS, LANE - MLA_ROPE), w.dtype)
    w1 = jnp.concatenate([nope, rope, zeros], axis=-1).reshape(r, MLA_HEADS * MLA_QK_PAD)
    rot = jnp.concatenate([-rope[..., half:], rope[..., :half]], axis=-1)
    w2 = jnp.concatenate([rot, zeros], axis=-1).reshape(r, MLA_HEADS * LANE)
    return w1.astype(BF16), w2.astype(BF16)


def _layer(x, c_pad, cos128, sin128, norm_g, w_ada, b_ada, w_in, mla_gq, mla_wuq, mla_gkv, mla_wukv,
           gla_wg2, gla_bg, gla_gout, w_mg, b_mg, w_bm, w_bg, w_bd, w_o):
    b, s, d = x.shape
    t = b * s
    mod = _mm(c_pad, w_ada, b_ada[None, :], tm=c_pad.shape[0], tn=512, out_dtype=F32,
              a_act="silu", name="ada")[:b]
    shift, scale, gate = (mod[:, None, k * d:(k + 1) * d] for k in range(3))
    h = _norm_mod(x, norm_g[None, :], scale, shift)
    h2 = h.reshape(t, d)

    mm_tm = min(1024, t)
    pm = _mm(h2, _main_weight(w_in), None, tm=mm_tm, tn=512, out_dtype=BF16, name="proj_main")
    z0 = _IN_OFF["z_mla"]
    pz = _mm(h2, w_in[:, z0:].astype(BF16), None, tm=mm_tm, tn=512, out_dtype=BF16, act="silu",
             name="proj_gate_paths")
    gates = _mm(h2, w_mg, b_mg[None, :], tm=mm_tm, tn=512, out_dtype=BF16,
                act="sigmoid", name="merge_gates")
    pm3 = pm.reshape(b, s, PM_WIDTH)
    pz3 = pz.reshape(b, s, d)

    w1, w2 = _mla_q_weights(mla_wuq)
    q = _mla_q(pm3, mla_gq[None, :], w1, w2, cos128, sin128)
    kcat, vt = _mla_kv(pm3, mla_gkv[None, :], mla_wukv.astype(BF16), cos128, sin128)
    y_mla = _mla_attn(q, kcat, vt, pz3)

    w2p = jnp.concatenate([gla_wg2, jnp.zeros((LANE - GLA_GATE_RANK, gla_wg2.shape[1]), gla_wg2.dtype)],
                          axis=0).astype(BF16)
    y_gla = _gla(pm3, pz3, w2p, gla_bg[None, :], gla_gout[None, :])

    nk = s // ATT_TK

    def piece(name, width):
        return pm3[:, :, _PM_OFF[name]:_PM_OFF[name] + width]

    iq_t = piece("iq", IDX_HEADS * IDX_DH).reshape(b, s, IDX_HEADS, IDX_DH).transpose(0, 2, 1, 3)
    iw3 = piece("iw", IDX_HEADS)
    ikt = piece("ik", IDX_DH).reshape(b, nk, ATT_TK, IDX_DH).transpose(0, 1, 3, 2)
    kv5 = (b, nk, ATT_TK, DSA_KV_HEADS, DSA_DH)
    dkt = piece("dk", DSA_KV_HEADS * DSA_DH).reshape(kv5).transpose(0, 3, 1, 4, 2)
    dv = piece("dv", DSA_KV_HEADS * DSA_DH).reshape(kv5).transpose(0, 3, 1, 2, 4)
    dv_ext = jnp.concatenate([dv, jnp.ones_like(dv)], axis=-1)
    y_dsa = _dsa(pm3, pz3, iq_t, iw3, ikt, dkt, dv_ext, min(IDX_TOPK, s // 4))

    merged = _lift(y_mla.reshape(t, MLA_WIDTH), y_gla.reshape(t, GLA_WIDTH), y_dsa.reshape(t, DSA_WIDTH),
                   w_bm.astype(BF16), w_bg.astype(BF16), w_bd.astype(BF16), gates)
    return _out_proj(merged.reshape(b, s, d), w_o.astype(BF16), x, gate)


def kernel(x, c, positions, norm_g, w_ada, b_ada, w_in, mla_gq, mla_wuq, mla_gkv, mla_wukv, gla_wg2,
           gla_bg, gla_gout, w_mg, b_mg, w_bm, w_bg, w_bd, w_o, final_g):
    b = x.shape[0]
    cos128, sin128 = _rope_tables(positions)
    c_pad = jnp.concatenate([c, jnp.zeros((8 - b, c.shape[1]), c.dtype)], axis=0)
    for l in range(DEPTH):
        x = _layer(x, c_pad, cos128, sin128, norm_g[l], w_ada[l], b_ada[l], w_in[l], mla_gq[l],
                   mla_wuq[l], mla_gkv[l], mla_wukv[l], gla_wg2[l], gla_bg[l], gla_gout[l],
                   w_mg[l], b_mg[l], w_bm[l], w_bg[l], w_bd[l], w_o[l])
    return _final_norm(x, final_g[None, :])
```

```python
import functools

import jax
import jax.numpy as jnp
import numpy as np
from jax import lax
from jax.experimental import pallas as pl
from jax.experimental.pallas import tpu as pltpu

D_MODEL = 4096
DEPTH = 2
MLA_HEADS = 16
MLA_Q_RANK = 768
MLA_KV_RANK = 512
MLA_NOPE = 128
MLA_ROPE = 64
MLA_V = 128
ROPE_THETA = 10000.0
GLA_HEADS = 4
GLA_DK = 128
GLA_DV = 256
GLA_GATE_RANK = 16
GLA_GATE_NORM = 16.0
GLA_CHUNK = 64
DSA_HEADS = 8
DSA_KV_HEADS = 2
DSA_DH = 128
IDX_HEADS = 32
IDX_DH = 64
IDX_TOPK = 256
NORM_EPS = 1e-6
NEG = -1e30

MLA_WIDTH = MLA_HEADS * MLA_V
GLA_WIDTH = GLA_HEADS * GLA_DV
DSA_WIDTH = DSA_HEADS * DSA_DH
DSA_REP = DSA_HEADS // DSA_KV_HEADS

IN_SPLITS = (
    MLA_Q_RANK, MLA_KV_RANK, MLA_ROPE,
    GLA_HEADS * GLA_DK, GLA_HEADS * GLA_DK, GLA_WIDTH, GLA_GATE_RANK,
    DSA_WIDTH, DSA_KV_HEADS * DSA_DH, DSA_KV_HEADS * DSA_DH,
    IDX_HEADS * IDX_DH, IDX_DH, IDX_HEADS,
    MLA_WIDTH, GLA_WIDTH, DSA_WIDTH,
)
_IN_NAMES = ("cq", "ckv", "krope", "gq", "gk", "gv", "glow", "dq", "dk", "dv",
             "iq", "ik", "iw", "z_mla", "z_gla", "z_dsa")
_IN_OFF = dict(zip(_IN_NAMES, np.concatenate([[0], np.cumsum(IN_SPLITS)[:-1]]).tolist()))
_IN_W = dict(zip(_IN_NAMES, IN_SPLITS))

LANE = 128
MLA_QK_PAD = 256
ATT_TQ = 1024
ATT_TK = 512
MLA_HEADS_PER_STEP = 4
LOG2E = 1.4426950408889634
DSA_TQ = 128
MM_CHUNK = 256
VMEM_LIMIT = 56 * 1024 * 1024

_PM_LAYOUT = (("dq", 1024), ("gv", 1024), ("iq", 2048), ("gq", 512), ("cq", 768), ("dk", 256),
              ("ckv", 512), ("gk", 512), ("dv", 256), ("kr", 128), ("krr", 128), ("glow", 128),
              ("ik", 128), ("iw", 128), ("pad", 128))
_PM_OFF = {}
_o = 0
for _n, _w in _PM_LAYOUT:
    assert _o % _w == 0
    _PM_OFF[_n] = _o
    _o += _w
PM_WIDTH = _o

BF16 = jnp.bfloat16
F32 = jnp.float32
INT_MIN = -2 ** 31


def _cp(n_axes, flags=None):
    return pltpu.CompilerParams(dimension_semantics=("arbitrary",) * n_axes,
                                vmem_limit_bytes=VMEM_LIMIT, flags=flags)


def _nt(a, b):
    return lax.dot_general(a, b, (((1,), (1,)), ((), ())), preferred_element_type=F32)


def _rms(x, g):
    return x * lax.rsqrt(jnp.mean(x * x, axis=-1, keepdims=True) + NORM_EPS) * g


def _mm_kernel(*refs, a_act, act, has_bias):
    if has_bias:
        a_ref, w_ref, b_ref, o_ref = refs
    else:
        a_ref, w_ref, o_ref = refs
    a = a_ref[...]
    if a_act == "silu":
        a = a.astype(F32)
        a = a * jax.nn.sigmoid(a)
    a = a.astype(BF16)
    for c0 in range(0, o_ref.shape[1], MM_CHUNK):
        cs = slice(c0, c0 + MM_CHUNK)
        r = jnp.dot(a, w_ref[:, cs].astype(BF16), preferred_element_type=F32)
        if has_bias:
            r = r + b_ref[:, cs]
        if act == "sigmoid":
            r = jax.nn.sigmoid(r)
        elif act == "silu":
            r = r * jax.nn.sigmoid(r)
        o_ref[:, cs] = r.astype(o_ref.dtype)


def _mm(a, w, bias, *, tm, tn, out_dtype, a_act=None, act=None, name):
    m, k = a.shape
    n = w.shape[1]
    in_specs = [pl.BlockSpec((tm, k), lambda i, j: (i, 0)),
                pl.BlockSpec((k, tn), lambda i, j: (0, j))]
    args = [a, w]
    if bias is not None:
        in_specs.append(pl.BlockSpec((1, tn), lambda i, j: (0, j)))
        args.append(bias)
    return pl.pallas_call(
        functools.partial(_mm_kernel, a_act=a_act, act=act, has_bias=bias is not None),
        grid=(m // tm, n // tn),
        in_specs=in_specs,
        out_specs=pl.BlockSpec((tm, tn), lambda i, j: (i, j)),
        out_shape=jax.ShapeDtypeStruct((m, n), out_dtype),
        compiler_params=_cp(2),
        name=name,
    )(*args)


def _norm_mod_kernel(x_ref, g_ref, sc_ref, sh_ref, o_ref):
    y = _rms(x_ref[0], g_ref[...])
    o_ref[0] = (y * (1.0 + sc_ref[0]) + sh_ref[0]).astype(o_ref.dtype)


def _norm_mod(x, g, scale, shift, *, tm=256):
    b, s, d = x.shape
    return pl.pallas_call(
        _norm_mod_kernel,
        grid=(b, s // tm),
        in_specs=[pl.BlockSpec((1, tm, d), lambda bi, i: (bi, i, 0)),
                  pl.BlockSpec((1, d), lambda bi, i: (0, 0)),
                  pl.BlockSpec((1, 1, d), lambda bi, i: (bi, 0, 0)),
                  pl.BlockSpec((1, 1, d), lambda bi, i: (bi, 0, 0))],
        out_specs=pl.BlockSpec((1, tm, d), lambda bi, i: (bi, i, 0)),
        out_shape=jax.ShapeDtypeStruct((b, s, d), BF16),
        compiler_params=_cp(2),
        name="norm_mod",
    )(x, g, scale, shift)


def _final_norm_kernel(x_ref, g_ref, o_ref):
    o_ref[0] = _rms(x_ref[0], g_ref[...])


def _final_norm(x, g, *, tm=256):
    b, s, d = x.shape
    return pl.pallas_call(
        _final_norm_kernel,
        grid=(b, s // tm),
        in_specs=[pl.BlockSpec((1, tm, d), lambda bi, i: (bi, i, 0)),
                  pl.BlockSpec((1, d), lambda bi, i: (0, 0))],
        out_specs=pl.BlockSpec((1, tm, d), lambda bi, i: (bi, i, 0)),
        out_shape=jax.ShapeDtypeStruct((b, s, d), F32),
        compiler_params=_cp(2),
        name="final_norm",
    )(x, g)


def _rope_kernel(pos_ref, inv_ref, cos_ref, sin_ref):
    ang = pos_ref[0] * inv_ref[...]
    live = lax.broadcasted_iota(jnp.int32, ang.shape, 1) < MLA_ROPE
    cos_ref[0] = jnp.where(live, jnp.cos(ang), 0.0)
    sin_ref[0] = jnp.where(live, jnp.sin(ang), 0.0)


def _rope_tables(positions, *, tm=512):
    b, s = positions.shape
    inv = ROPE_THETA ** (-jnp.arange(0, MLA_ROPE, 2, dtype=F32) / MLA_ROPE)
    inv128 = jnp.concatenate([inv, inv, jnp.zeros((LANE - MLA_ROPE,), F32)])[None, :]
    pos128 = jnp.broadcast_to(positions.astype(F32)[:, :, None], (b, s, LANE))
    spec = pl.BlockSpec((1, tm, LANE), lambda bi, i: (bi, i, 0))
    return pl.pallas_call(
        _rope_kernel,
        grid=(b, s // tm),
        in_specs=[spec, pl.BlockSpec((1, LANE), lambda bi, i: (0, 0))],
        out_specs=[spec, spec],
        out_shape=[jax.ShapeDtypeStruct((b, s, LANE), F32)] * 2,
        compiler_params=_cp(2),
        name="rope_tables",
    )(pos128, inv128)


def _mla_q_kernel(cq_ref, g_ref, w1_ref, w2_ref, cos_ref, sin_ref, o_ref, an_ref, *, scale):
    @pl.when(pl.program_id(2) == 0)
    def _():
        an_ref[...] = _rms(cq_ref[0].astype(F32), g_ref[...]).astype(BF16)

    a = an_ref[...]
    a1 = jnp.dot(a, w1_ref[...], preferred_element_type=F32)
    a2 = jnp.dot(a, w2_ref[...], preferred_element_type=F32)
    cos, sin = cos_ref[0], sin_ref[0]
    for u in range(MLA_HEADS_PER_STEP):
        c0 = u * MLA_QK_PAD
        rope = a1[:, c0 + MLA_NOPE:c0 + MLA_QK_PAD] * cos + a2[:, u * LANE:(u + 1) * LANE] * sin
        o_ref[0, :, c0:c0 + MLA_NOPE] = (a1[:, c0:c0 + MLA_NOPE] * scale).astype(BF16)
        o_ref[0, :, c0 + MLA_NOPE:c0 + MLA_QK_PAD] = (rope * scale).astype(BF16)


def _mla_q(pm3, g_q, w1, w2, cos128, sin128, *, tm=512):
    b, s, _ = pm3.shape
    hp = MLA_HEADS_PER_STEP
    cq_blk = _PM_OFF["cq"] // MLA_Q_RANK
    return pl.pallas_call(
        functools.partial(_mla_q_kernel, scale=(MLA_NOPE + MLA_ROPE) ** -0.5 * LOG2E),
        grid=(b, s // tm, MLA_HEADS // hp),
        in_specs=[pl.BlockSpec((1, tm, MLA_Q_RANK), lambda bi, i, h: (bi, i, cq_blk)),
                  pl.BlockSpec((1, MLA_Q_RANK), lambda bi, i, h: (0, 0)),
                  pl.BlockSpec((MLA_Q_RANK, hp * MLA_QK_PAD), lambda bi, i, h: (0, h)),
                  pl.BlockSpec((MLA_Q_RANK, hp * LANE), lambda bi, i, h: (0, h)),
                  pl.BlockSpec((1, tm, LANE), lambda bi, i, h: (bi, i, 0)),
                  pl.BlockSpec((1, tm, LANE), lambda bi, i, h: (bi, i, 0))],
        out_specs=pl.BlockSpec((1, tm, hp * MLA_QK_PAD), lambda bi, i, h: (bi, i, h)),
        out_shape=jax.ShapeDtypeStruct((b, s, MLA_HEADS * MLA_QK_PAD), BF16),
        scratch_shapes=[pltpu.VMEM((tm, MLA_Q_RANK), BF16)],
        compiler_params=_cp(3),
        name="mla_q",
    )(pm3, g_q, w1, w2, cos128, sin128)


def _mla_kv_kernel(ckv_ref, g_ref, w_ref, kr_ref, krr_ref, cos_ref, sin_ref, k_ref, v_ref, an_ref, krt_ref):
    @pl.when(pl.program_id(2) == 0)
    def _():
        an_ref[...] = _rms(ckv_ref[0].astype(F32), g_ref[...]).astype(BF16)
        kr = kr_ref[0].astype(F32) * cos_ref[0] + krr_ref[0].astype(F32) * sin_ref[0]
        krt_ref[...] = kr.T.astype(BF16)

    acc = jnp.dot(an_ref[...], w_ref[...], preferred_element_type=F32)
    ones = jnp.ones((acc.shape[0], MLA_V), BF16)
    for u in range(MLA_HEADS_PER_STEP):
        c0 = u * (MLA_NOPE + MLA_V)
        k_ref[0, u, 0, :MLA_NOPE, :] = acc[:, c0:c0 + MLA_NOPE].T.astype(BF16)
        k_ref[0, u, 0, MLA_NOPE:, :] = krt_ref[...]
        v_ref[0, u, 0, :, :MLA_V] = acc[:, c0 + MLA_NOPE:c0 + MLA_NOPE + MLA_V].astype(BF16)
        v_ref[0, u, 0, :, MLA_V:] = ones


def _mla_kv(pm3, g_kv, w_ukv, cos128, sin128):
    b, s, _ = pm3.shape
    tm = ATT_TK
    hp = MLA_HEADS_PER_STEP
    ckv_blk = _PM_OFF["ckv"] // MLA_KV_RANK
    kr_blk = _PM_OFF["kr"] // LANE
    krr_blk = _PM_OFF["krr"] // LANE
    return pl.pallas_call(
        _mla_kv_kernel,
        grid=(b, s // tm, MLA_HEADS // hp),
        in_specs=[pl.BlockSpec((1, tm, MLA_KV_RANK), lambda bi, i, h: (bi, i, ckv_blk)),
                  pl.BlockSpec((1, MLA_KV_RANK), lambda bi, i, h: (0, 0)),
                  pl.BlockSpec((MLA_KV_RANK, hp * (MLA_NOPE + MLA_V)), lambda bi, i, h: (0, h)),
                  pl.BlockSpec((1, tm, LANE), lambda bi, i, h: (bi, i, kr_blk)),
                  pl.BlockSpec((1, tm, LANE), lambda bi, i, h: (bi, i, krr_blk)),
                  pl.BlockSpec((1, tm, LANE), lambda bi, i, h: (bi, i, 0)),
                  pl.BlockSpec((1, tm, LANE), lambda bi, i, h: (bi, i, 0))],
        out_specs=[pl.BlockSpec((1, hp, 1, MLA_QK_PAD, tm), lambda bi, i, h: (bi, h, i, 0, 0)),
                   pl.BlockSpec((1, hp, 1, tm, 2 * MLA_V), lambda bi, i, h: (bi, h, i, 0, 0))],
        out_shape=[jax.ShapeDtypeStruct((b, MLA_HEADS, s // tm, MLA_QK_PAD, tm), BF16),
                   jax.ShapeDtypeStruct((b, MLA_HEADS, s // tm, tm, 2 * MLA_V), BF16)],
        scratch_shapes=[pltpu.VMEM((tm, MLA_KV_RANK), BF16), pltpu.VMEM((LANE, tm), BF16)],
        compiler_params=_cp(3),
        name="mla_kv",
    )(pm3, g_kv, w_ukv, pm3, pm3, cos128, sin128)


def _sm_init(acc_ref, m_ref):
    m_ref[...] = jnp.full(m_ref.shape, NEG, F32)
    acc_ref[...] = jnp.zeros(acc_ref.shape, F32)


def _sm_step(s, v_ext, acc_ref, m_ref):
    m_old = m_ref[...]
    m_new = jnp.maximum(m_old, jnp.max(s, axis=-1, keepdims=True))
    alpha = jnp.exp2(m_old - m_new)
    p = jnp.exp2(s - jnp.tile(m_new, (1, s.shape[1] // LANE)))
    pv = jnp.dot(p.astype(BF16), v_ext, preferred_element_type=F32)
    acc_ref[...] = jnp.tile(alpha, (1, acc_ref.shape[1] // LANE)) * acc_ref[...] + pv
    m_ref[...] = m_new


def _sm_result(acc_ref, d):
    acc = acc_ref[...]
    return acc[:, :d] / acc[:, d:]


def _mla_attn_kernel(q_ref, kt_ref, v_ref, z_ref, o_ref, s_ref, acc_ref, m_ref):
    i = pl.program_id(2)
    t = ATT_TK
    assert ATT_TQ == 2 * t
    chains = range(2)
    qpos = lax.broadcasted_iota(jnp.int32, (t, t), 0)
    kpos = lax.broadcasted_iota(jnp.int32, (t, t), 1)

    def scores(u, j):
        return jnp.dot(q_ref[0, u * t:(u + 1) * t, :], kt_ref[0, 0, j], preferred_element_type=F32)

    def consume(u, buf, j, diagonal):
        s = s_ref[buf, u]
        if diagonal:
            s = jnp.where(kpos <= qpos, s, NEG)
        _sm_step(s, v_ref[0, 0, j], acc_ref.at[u], m_ref.at[u])

    for u in chains:
        _sm_init(acc_ref.at[u], m_ref.at[u])
        s_ref[0, u] = scores(u, 0)

    def block_pair(p, carry):
        j = 2 * p
        for u in chains:
            consume(u, 0, j, False)
            s_ref[1, u] = scores(u, j + 1)
        for u in chains:
            consume(u, 1, j + 1, False)
            s_ref[0, u] = scores(u, j + 2)
        return carry

    lax.fori_loop(0, i, block_pair, 0)
    first = 2 * i
    consume(0, 0, first, True)
    consume(1, 0, first, False)
    s_ref[1, 1] = scores(1, first + 1)
    consume(1, 1, first + 1, True)
    for u in chains:
        rows = slice(u * t, (u + 1) * t)
        o_ref[0, rows, :] = (_sm_result(acc_ref.at[u], MLA_V) * z_ref[0, rows, :].astype(F32)).astype(o_ref.dtype)


def _mla_attn(q, kt, v_ext, pz3):
    b, s, _ = q.shape
    nk = s // ATT_TK
    return pl.pallas_call(
        _mla_attn_kernel,
        grid=(b, MLA_HEADS, s // ATT_TQ),
        in_specs=[pl.BlockSpec((1, ATT_TQ, MLA_QK_PAD), lambda bi, h, i: (bi, i, h)),
                  pl.BlockSpec((1, 1, nk, MLA_QK_PAD, ATT_TK), lambda bi, h, i: (bi, h, 0, 0, 0)),
                  pl.BlockSpec((1, 1, nk, ATT_TK, 2 * MLA_V), lambda bi, h, i: (bi, h, 0, 0, 0)),
                  pl.BlockSpec((1, ATT_TQ, MLA_V), lambda bi, h, i: (bi, i, h))],
        out_specs=pl.BlockSpec((1, ATT_TQ, MLA_V), lambda bi, h, i: (bi, i, h)),
        out_shape=jax.ShapeDtypeStruct((b, s, MLA_WIDTH), BF16),
        scratch_shapes=[pltpu.VMEM((2, 2, ATT_TK, ATT_TK), F32),
                        pltpu.VMEM((2, ATT_TK, 2 * MLA_V), F32),
                        pltpu.VMEM((2, ATT_TK, LANE), F32)],
        compiler_params=_cp(3),
        name="mla_attn",
    )(q, kt, v_ext, pz3)


def _log_sigmoid(x):
    return jnp.minimum(x, 0.0) - jnp.log(1.0 + jnp.exp(-jnp.abs(x)))


def _gla_kernel(q_ref, k_ref, v_ref, gl_ref, z_ref, w2_ref, bg_ref, go_ref, o_ref, st_ref, *, nb):
    @pl.when(pl.program_id(0) == 0)
    def _():
        st_ref[...] = jnp.zeros(st_ref.shape, F32)

    c = GLA_CHUNK
    row = lax.broadcasted_iota(jnp.int32, (c, c), 0)
    col = lax.broadcasted_iota(jnp.int32, (c, c), 1)
    tril = row >= col
    tril_b = jnp.where(tril, 1.0, 0.0).astype(BF16)
    qw = GLA_HEADS * GLA_DK
    for b in range(nb):
        pre = jnp.dot(gl_ref[b], w2_ref[...], preferred_element_type=F32) + bg_ref[...]
        glog = _log_sigmoid(pre) / GLA_GATE_NORM
        g1 = glog.astype(BF16)
        r1 = glog - g1.astype(F32)
        g2 = r1.astype(BF16)
        g3 = (r1 - g2.astype(F32)).astype(BF16)
        cs = jnp.dot(tril_b, jnp.concatenate([g1, g2, g3], axis=1), preferred_element_type=F32)
        bc_all = cs[:, :qw] + cs[:, qw:2 * qw] + cs[:, 2 * qw:]
        for h in range(GLA_HEADS):
            ks = slice(h * GLA_DK, (h + 1) * GLA_DK)
            vs = slice(h * GLA_DV, (h + 1) * GLA_DV)
            q = q_ref[b, :, ks].astype(F32)
            k = k_ref[b, :, ks].astype(F32)
            v = v_ref[b, :, vs]
            bc = bc_all[:, ks]
            b_last = bc[c - 1:c, :]
            q_dec = (q * GLA_DK ** -0.5 * jnp.exp(bc)).astype(BF16)
            k_inv = (k * jnp.exp(-bc)).astype(BF16)
            k_end = (k * jnp.exp(b_last - bc)).astype(BF16)
            decay = jnp.exp(b_last)
            attn = jnp.where(tril, _nt(q_dec, k_inv), 0.0).astype(BF16)
            st = st_ref[b * GLA_HEADS + h]
            o = jnp.dot(jnp.concatenate([q_dec, attn], axis=1),
                        jnp.concatenate([st.astype(BF16), v], axis=0), preferred_element_type=F32)
            upd = lax.dot_general(k_end, v, (((0,), (0,)), ((), ())), preferred_element_type=F32)
            decay_col = jnp.tile(jnp.broadcast_to(decay, (GLA_DK, GLA_DK)).T, (1, GLA_DV // GLA_DK))
            st_ref[b * GLA_HEADS + h] = st * decay_col + upd
            on = _rms(o, go_ref[...])
            o_ref[b, :, vs] = (on * z_ref[b, :, vs].astype(F32)).astype(o_ref.dtype)


def _gla(pm3, pz3, w2p, bg, gout):
    b, s, _ = pm3.shape
    c = GLA_CHUNK
    qw = GLA_HEADS * GLA_DK
    q_blk, k_blk = _PM_OFF["gq"] // qw, _PM_OFF["gk"] // qw
    v_blk, gl_blk = _PM_OFF["gv"] // GLA_WIDTH, _PM_OFF["glow"] // LANE
    z_blk = MLA_WIDTH // GLA_WIDTH
    return pl.pallas_call(
        functools.partial(_gla_kernel, nb=b),
        grid=(s // c,),
        in_specs=[pl.BlockSpec((b, c, qw), lambda i: (0, i, q_blk)),
                  pl.BlockSpec((b, c, qw), lambda i: (0, i, k_blk)),
                  pl.BlockSpec((b, c, GLA_WIDTH), lambda i: (0, i, v_blk)),
                  pl.BlockSpec((b, c, LANE), lambda i: (0, i, gl_blk)),
                  pl.BlockSpec((b, c, GLA_WIDTH), lambda i: (0, i, z_blk)),
                  pl.BlockSpec((LANE, qw), lambda i: (0, 0)),
                  pl.BlockSpec((1, qw), lambda i: (0, 0)),
                  pl.BlockSpec((1, GLA_DV), lambda i: (0, 0))],
        out_specs=pl.BlockSpec((b, c, GLA_WIDTH), lambda i: (0, i, 0)),
        out_shape=jax.ShapeDtypeStruct((b, s, GLA_WIDTH), BF16),
        scratch_shapes=[pltpu.VMEM((b * GLA_HEADS, GLA_DK, GLA_DV), F32)],
        compiler_params=_cp(1),
        name="gla",
    )(pm3, pm3, pm3, pm3, pz3, w2p, bg, gout)


def _dsa_kernel(iq_ref, iw_ref, onehot_ref, ikt_ref, q_ref, kt_ref, v_ref, z_ref, o_ref,
                key_ref, keyt_ref, wb_ref, qs_ref, s_ref, acc_ref, m_ref, *, top_k):
    tq, tk = DSA_TQ, ATT_TK
    nk = key_ref.shape[0]
    lanes = tk // LANE
    i = pl.program_id(1)
    nkb = (i * tq + tq - 1) // tk + 1
    qpos = i * tq + lax.broadcasted_iota(jnp.int32, (tq, tk), 0)
    heads_per_dot = 4

    wb = jnp.dot(iw_ref[0], onehot_ref[...], preferred_element_type=F32)
    for h in range(IDX_HEADS):
        wb_ref[h] = wb[:, h * LANE:(h + 1) * LANE] * (IDX_HEADS ** -0.5 * IDX_DH ** -0.5)

    def sortable(x):
        bits = lax.bitcast_convert_type(x, jnp.int32)
        return bits ^ ((bits >> 31) & 0x7FFFFFFF)

    qpos_t = i * tq + lax.broadcasted_iota(jnp.int32, (tk, tq), 1)

    def score_block(j, carry):
        ikt = ikt_ref[0, j]
        sc = jnp.zeros((tq, tk), F32)
        for hg in range(IDX_HEADS // heads_per_dot):
            iq = iq_ref[0, hg * heads_per_dot:(hg + 1) * heads_per_dot].reshape(heads_per_dot * tq, IDX_DH)
            r = jnp.dot(iq, ikt, preferred_element_type=F32)
            for hh in range(heads_per_dot):
                wrow = jnp.tile(wb_ref[hg * heads_per_dot + hh], (1, lanes))
                sc = sc + jnp.maximum(r[hh * tq:(hh + 1) * tq, :], 0.0) * wrow
        kpos = j * tk + lax.broadcasted_iota(jnp.int32, (tq, tk), 1)
        key_ref[j] = jnp.where(kpos <= qpos, sortable(sc), INT_MIN)
        kpos_t = j * tk + lax.broadcasted_iota(jnp.int32, (tk, tq), 0)
        keyt_ref[j] = jnp.where(kpos_t <= qpos_t, sortable(sc.T), INT_MIN)
        return carry

    lax.fori_loop(0, nkb, score_block, 0)

    def count(pred):
        def body(j, acc):
            hit = jnp.where(pred(keyt_ref[j], j), 1, 0).astype(jnp.int32)
            return acc + hit.reshape(tk // 8, 8, tq).sum(axis=0)
        acc = lax.fori_loop(0, nkb, body, jnp.zeros((8, tq), jnp.int32))
        return acc.sum(axis=0, keepdims=True)

    def count_ge(cand):
        return count(lambda k, j: k >= cand)

    zero = jnp.zeros((1, tq), jnp.int32)
    c0 = count_ge(zero)
    thr = jnp.where(c0 >= top_k, zero, INT_MIN)
    n_ge = jnp.where(c0 >= top_k, c0, nkb * tk)

    def bit_step(it, carry):
        thr, n_ge = carry
        cand = thr | lax.shift_left(jnp.int32(1), 30 - it)
        c = count_ge(cand)
        return jnp.where(c >= top_k, cand, thr), jnp.where(c >= top_k, c, n_ge)

    thr, n_ge = lax.fori_loop(0, 31, bit_step, (thr, n_ge))

    n_gt = count(lambda k, j: k > thr)
    n_take = top_k - n_gt
    overflow = (thr > INT_MIN) & (n_ge - n_gt > n_take)

    @pl.when(jnp.max(jnp.where(overflow, 1, 0)) > 0)
    def _():
        rows = lax.broadcasted_iota(jnp.int32, (tk, tq), 0)
        bound = jnp.zeros((1, tq), jnp.int32)
        for bit in range((nk * tk - 1).bit_length() - 1, -1, -1):
            cand = bound | (1 << bit)
            below = count(lambda k, j: (k == thr) & (j * tk + rows < cand))
            bound = jnp.where(below < n_take, cand, bound)
        bound = jnp.where(overflow, bound, nk * tk)
        thr_c = jnp.tile(jnp.broadcast_to(thr, (tq, tq)).T, (1, lanes))
        bound_c = jnp.tile(jnp.broadcast_to(bound, (tq, tq)).T, (1, lanes))

        def demote(j, carry):
            k = key_ref[j]
            kpos = j * tk + lax.broadcasted_iota(jnp.int32, (tq, tk), 1)
            key_ref[j] = jnp.where((k == thr_c) & (kpos > bound_c), INT_MIN, k)
            return carry

        lax.fori_loop(0, nkb, demote, 0)

    thr = jnp.maximum(thr, INT_MIN + 1)
    thr_w = jnp.tile(jnp.broadcast_to(thr, (tq, tq)).T, (1, lanes))

    scale = DSA_DH ** -0.5 * LOG2E
    groups = range(DSA_KV_HEADS)

    def scores(g, j):
        return jnp.dot(qs_ref[g], kt_ref[0, g, j], preferred_element_type=F32)

    def consume(g, buf, j):
        sel = key_ref[j] >= thr_w
        s = s_ref[buf, g]
        s = jnp.concatenate([jnp.where(sel, s[r * tq:(r + 1) * tq, :], NEG) for r in range(DSA_REP)], axis=0)
        _sm_step(s, v_ref[0, g, j], acc_ref.at[g], m_ref.at[g])

    for g in groups:
        for r in range(DSA_REP):
            cs = slice((g * DSA_REP + r) * DSA_DH, (g * DSA_REP + r + 1) * DSA_DH)
            qs_ref[g, r * tq:(r + 1) * tq, :] = (q_ref[0, :, cs].astype(F32) * scale).astype(BF16)
        _sm_init(acc_ref.at[g], m_ref.at[g])
    for g in groups:
        s_ref[0, g] = scores(g, 0)

    def block_pair(p, carry):
        j = 2 * p
        for g in groups:
            consume(g, 0, j)
            s_ref[1, g] = scores(g, j + 1)
        for g in groups:
            consume(g, 1, j + 1)
            s_ref[0, g] = scores(g, jnp.minimum(j + 2, nk - 1))
        return carry

    lax.fori_loop(0, nkb // 2, block_pair, 0)

    @pl.when(nkb % 2 == 1)
    def _():
        for g in groups:
            consume(g, 0, nkb - 1)

    for g in groups:
        res = _sm_result(acc_ref.at[g], DSA_DH)
        for r in range(DSA_REP):
            cs = slice((g * DSA_REP + r) * DSA_DH, (g * DSA_REP + r + 1) * DSA_DH)
            o_ref[0, :, cs] = (res[r * tq:(r + 1) * tq, :] * z_ref[0, :, cs].astype(F32)).astype(o_ref.dtype)


def _dsa(pm3, pz3, iq_t, iw3, ikt, dkt, dv_ext, top_k):
    b, s, _ = pm3.shape
    tq, tk = DSA_TQ, ATT_TK
    nk = s // tk
    q_blk = _PM_OFF["dq"] // DSA_WIDTH
    z_blk = (MLA_WIDTH + GLA_WIDTH) // DSA_WIDTH
    rows = DSA_REP * tq
    assert tq == LANE
    onehot = jnp.repeat(jnp.eye(IDX_HEADS, dtype=BF16), LANE, axis=1)
    return pl.pallas_call(
        functools.partial(_dsa_kernel, top_k=top_k),
        grid=(b, s // tq),
        in_specs=[pl.BlockSpec((1, IDX_HEADS, tq, IDX_DH), lambda bi, i: (bi, 0, i, 0)),
                  pl.BlockSpec((1, tq, IDX_HEADS), lambda bi, i: (bi, i, 0)),
                  pl.BlockSpec((IDX_HEADS, IDX_HEADS * LANE), lambda bi, i: (0, 0)),
                  pl.BlockSpec((1, nk, IDX_DH, tk), lambda bi, i: (bi, 0, 0, 0)),
                  pl.BlockSpec((1, tq, DSA_WIDTH), lambda bi, i: (bi, i, q_blk)),
                  pl.BlockSpec((1, DSA_KV_HEADS, nk, DSA_DH, tk), lambda bi, i: (bi, 0, 0, 0, 0)),
                  pl.BlockSpec((1, DSA_KV_HEADS, nk, tk, 2 * DSA_DH), lambda bi, i: (bi, 0, 0, 0, 0)),
                  pl.BlockSpec((1, tq, DSA_WIDTH), lambda bi, i: (bi, i, z_blk))],
        out_specs=pl.BlockSpec((1, tq, DSA_WIDTH), lambda bi, i: (bi, i, 0)),
        out_shape=jax.ShapeDtypeStruct((b, s, DSA_WIDTH), BF16),
        scratch_shapes=[pltpu.VMEM((nk, tq, tk), jnp.int32),
                        pltpu.VMEM((nk, tk, tq), jnp.int32),
                        pltpu.VMEM((IDX_HEADS, tq, LANE), F32),
                        pltpu.VMEM((DSA_KV_HEADS, rows, DSA_DH), BF16),
                        pltpu.VMEM((2, DSA_KV_HEADS, rows, tk), F32),
                        pltpu.VMEM((DSA_KV_HEADS, rows, 2 * DSA_DH), F32),
                        pltpu.VMEM((DSA_KV_HEADS, rows, LANE), F32)],
        compiler_params=_cp(2),
        name="dsa",
    )(iq_t, iw3, onehot, ikt, pm3, dkt, dv_ext, pz3)


def _lift_kernel(ym_ref, yg_ref, yd_ref, wm_ref, wg_ref, wd_ref, ga_ref, gb_ref, gc_ref, o_ref):
    a = jnp.dot(ym_ref[...], wm_ref[...], preferred_element_type=F32)
    b = jnp.dot(yg_ref[...], wg_ref[...], preferred_element_type=F32)
    c = jnp.dot(yd_ref[...], wd_ref[...], preferred_element_type=F32)
    o = ga_ref[...].astype(F32) * a + gb_ref[...].astype(F32) * b + gc_ref[...].astype(F32) * c
    o_ref[...] = o.astype(o_ref.dtype)


def _lift(y_mla, y_gla, y_dsa, w_bm, w_bg, w_bd, gates, *, tm=512, tn=1024):
    t = y_mla.shape[0]
    d = w_bm.shape[1]
    nj = d // tn
    row = lambda width: pl.BlockSpec((tm, width), lambda i, j: (i, 0))
    wcol = lambda depth: pl.BlockSpec((depth, tn), lambda i, j: (0, j))
    gate = lambda br: pl.BlockSpec((tm, tn), lambda i, j: (i, br * nj + j))
    return pl.pallas_call(
        _lift_kernel,
        grid=(t // tm, nj),
        in_specs=[row(MLA_WIDTH), row(GLA_WIDTH), row(DSA_WIDTH),
                  wcol(MLA_WIDTH), wcol(GLA_WIDTH), wcol(DSA_WIDTH),
                  gate(0), gate(1), gate(2)],
        out_specs=pl.BlockSpec((tm, tn), lambda i, j: (i, j)),
        out_shape=jax.ShapeDtypeStruct((t, d), BF16),
        compiler_params=_cp(2),
        name="lift_merge",
    )(y_mla, y_gla, y_dsa, w_bm, w_bg, w_bd, gates, gates, gates)


def _out_kernel(m_ref, w_ref, x_ref, gate_ref, o_ref):
    r = jnp.dot(m_ref[0], w_ref[...], preferred_element_type=F32)
    o_ref[0] = x_ref[0] + gate_ref[0] * r


def _out_proj(merged3, w_o, x, gate, *, tm=1024, tn=1024):
    b, s, d = x.shape
    tm = min(tm, s)
    return pl.pallas_call(
        _out_kernel,
        grid=(b, s // tm, d // tn),
        in_specs=[pl.BlockSpec((1, tm, d), lambda bi, i, j: (bi, i, 0)),
                  pl.BlockSpec((d, tn), lambda bi, i, j: (0, j)),
                  pl.BlockSpec((1, tm, tn), lambda bi, i, j: (bi, i, j)),
                  pl.BlockSpec((1, 1, tn), lambda bi, i, j: (bi, 0, j))],
        out_specs=pl.BlockSpec((1, tm, tn), lambda bi, i, j: (bi, i, j)),
        out_shape=jax.ShapeDtypeStruct((b, s, d), F32),
        compiler_params=_cp(3),
        name="out_proj",
    )(merged3, w_o, x, gate)


def _main_weight(w_in):
    d = w_in.shape[0]

    def cols(name):
        return w_in[:, _IN_OFF[name]:_IN_OFF[name] + _IN_W[name]]

    def padded(w, width):
        return jnp.concatenate([w, jnp.zeros((d, width - w.shape[1]), w.dtype)], axis=1)

    kr = cols("krope")
    half = MLA_ROPE // 2
    parts = {"kr": padded(kr, LANE),
             "krr": padded(jnp.concatenate([-kr[:, half:], kr[:, :half]], axis=1), LANE),
             "glow": padded(cols("glow"), LANE), "ik": padded(cols("ik"), LANE),
             "iw": padded(cols("iw"), LANE), "pad": jnp.zeros((d, LANE), w_in.dtype)}
    return jnp.concatenate([parts[n] if n in parts else cols(n) for n, _ in _PM_LAYOUT], axis=1).astype(BF16)


def _mla_q_weights(w_uq):
    r = w_uq.shape[0]
    w = w_uq.reshape(r, MLA_HEADS, MLA_NOPE + MLA_ROPE)
    nope, rope = w[..., :MLA_NOPE], w[..., MLA_NOPE:]
    half = MLA_ROPE // 2
    zeros = jnp.zeros((r, MLA_HEADS, LANE - MLA_ROPE), w.dtype)
    w1 = jnp.concatenate([nope, rope, zeros], axis=-1).reshape(r, MLA_HEADS * MLA_QK_PAD)
    rot = jnp.concatenate([-rope[..., half:], rope[..., :half]], axis=-1)
    w2 = jnp.concatenate([rot, zeros], axis=-1).reshape(r, MLA_HEADS * LANE)
    return w1.astype(BF16), w2.astype(BF16)


def _layer(x, c_pad, cos128, sin128, norm_g, w_ada, b_ada, w_in, mla_gq, mla_wuq, mla_gkv, mla_wukv,
           gla_wg2, gla_bg, gla_gout, w_mg, b_mg, w_bm, w_bg, w_bd, w_o):
    b, s, d = x.shape
    t = b * s
    mod = _mm(c_pad, w_ada, b_ada[None, :], tm=c_pad.shape[0], tn=512, out_dtype=F32,
              a_act="silu", name="ada")[:b]
    shift, scale, gate = (mod[:, None, k * d:(k + 1) * d] for k in range(3))
    h = _norm_mod(x, norm_g[None, :], scale, shift)
    h2 = h.reshape(t, d)

    mm_tm = min(1024, t)
    pm = _mm(h2, _main_weight(w_in), None, tm=mm_tm, tn=1536, out_dtype=BF16, name="proj_main")
    z0 = _IN_OFF["z_mla"]
    pz = _mm(h2, w_in[:, z0:].astype(BF16), None, tm=mm_tm, tn=1024, out_dtype=BF16, act="silu",
             name="proj_gate_paths")
    gates = _mm(h2, w_mg, b_mg[None, :], tm=mm_tm, tn=512, out_dtype=BF16,
                act="sigmoid", name="merge_gates")
    pm3 = pm.reshape(b, s, PM_WIDTH)
    pz3 = pz.reshape(b, s, d)

    w1, w2 = _mla_q_weights(mla_wuq)
    q = _mla_q(pm3, mla_gq[None, :], w1, w2, cos128, sin128)
    kcat, vt = _mla_kv(pm3, mla_gkv[None, :], mla_wukv.astype(BF16), cos128, sin128)
    y_mla = _mla_attn(q, kcat, vt, pz3)

    w2p = jnp.concatenate([gla_wg2, jnp.zeros((LANE - GLA_GATE_RANK, gla_wg2.shape[1]), gla_wg2.dtype)],
                          axis=0).astype(BF16)
    y_gla = _gla(pm3, pz3, w2p, gla_bg[None, :], gla_gout[None, :])

    nk = s // ATT_TK

    def piece(name, width):
        return pm3[:, :, _PM_OFF[name]:_PM_OFF[name] + width]

    iq_t = piece("iq", IDX_HEADS * IDX_DH).reshape(b, s, IDX_HEADS, IDX_DH).transpose(0, 2, 1, 3)
    iw3 = piece("iw", IDX_HEADS)
    ikt = piece("ik", IDX_DH).reshape(b, nk, ATT_TK, IDX_DH).transpose(0, 1, 3, 2)
    kv5 = (b, nk, ATT_TK, DSA_KV_HEADS, DSA_DH)
    dkt = piece("dk", DSA_KV_HEADS * DSA_DH).reshape(kv5).transpose(0, 3, 1, 4, 2)
    dv = piece("dv", DSA_KV_HEADS * DSA_DH).reshape(kv5).transpose(0, 3, 1, 2, 4)
    dv_ext = jnp.concatenate([dv, jnp.ones_like(dv)], axis=-1)
    y_dsa = _dsa(pm3, pz3, iq_t, iw3, ikt, dkt, dv_ext, min(IDX_TOPK, s // 4))

    merged = _lift(y_mla.reshape(t, MLA_WIDTH), y_gla.reshape(t, GLA_WIDTH), y_dsa.reshape(t, DSA_WIDTH),
                   w_bm.astype(BF16), w_bg.astype(BF16), w_bd.astype(BF16), gates)
    return _out_proj(merged.reshape(b, s, d), w_o.astype(BF16), x, gate)


def kernel(x, c, positions, norm_g, w_ada, b_ada, w_in, mla_gq, mla_wuq, mla_gkv, mla_wukv, gla_wg2,
           gla_bg, gla_gout, w_mg, b_mg, w_bm, w_bg, w_bd, w_o, final_g):
    b = x.shape[0]
    cos128, sin128 = _rope_tables(positions)
    c_pad = jnp.concatenate([c, jnp.zeros((8 - b, c.shape[1]), c.dtype)], axis=0)
    for l in range(DEPTH):
        x = _layer(x, c_pad, cos128, sin128, norm_g[l], w_ada[l], b_ada[l], w_in[l], mla_gq[l],
                   mla_wuq[l], mla_gkv[l], mla_wukv[l], gla_wg2[l], gla_bg[l], gla_gout[l],
                   w_mg[l], b_mg[l], w_bm[l], w_bg[l], w_bd[l], w_o[l])
    return _final_norm(x, final_g[None, :])
```

```python
import functools

import jax
import jax.numpy as jnp
import numpy as np
from jax import lax
from jax.experimental import pallas as pl
from jax.experimental.pallas import tpu as pltpu

D_MODEL = 4096
DEPTH = 2
MLA_HEADS = 16
MLA_Q_RANK = 768
MLA_KV_RANK = 512
MLA_NOPE = 128
MLA_ROPE = 64
MLA_V = 128
ROPE_THETA = 10000.0
GLA_HEADS = 4
GLA_DK = 128
GLA_DV = 256
GLA_GATE_RANK = 16
GLA_GATE_NORM = 16.0
GLA_CHUNK = 64
DSA_HEADS = 8
DSA_KV_HEADS = 2
DSA_DH = 128
IDX_HEADS = 32
IDX_DH = 64
IDX_TOPK = 256
NORM_EPS = 1e-6
NEG = -1e30

MLA_WIDTH = MLA_HEADS * MLA_V
GLA_WIDTH = GLA_HEADS * GLA_DV
DSA_WIDTH = DSA_HEADS * DSA_DH
DSA_REP = DSA_HEADS // DSA_KV_HEADS

IN_SPLITS = (
    MLA_Q_RANK, MLA_KV_RANK, MLA_ROPE,
    GLA_HEADS * GLA_DK, GLA_HEADS * GLA_DK, GLA_WIDTH, GLA_GATE_RANK,
    DSA_WIDTH, DSA_KV_HEADS * DSA_DH, DSA_KV_HEADS * DSA_DH,
    IDX_HEADS * IDX_DH, IDX_DH, IDX_HEADS,
    MLA_WIDTH, GLA_WIDTH, DSA_WIDTH,
)
_IN_NAMES = ("cq", "ckv", "krope", "gq", "gk", "gv", "glow", "dq", "dk", "dv",
             "iq", "ik", "iw", "z_mla", "z_gla", "z_dsa")
_IN_OFF = dict(zip(_IN_NAMES, np.concatenate([[0], np.cumsum(IN_SPLITS)[:-1]]).tolist()))
_IN_W = dict(zip(_IN_NAMES, IN_SPLITS))

LANE = 128
MLA_QK_PAD = 256
ATT_TQ = 2048
ATT_TK = 512
MLA_HEADS_PER_STEP = 4
LOG2E = 1.4426950408889634
DSA_TQ = 128
MM_CHUNK = 256
VMEM_LIMIT = 56 * 1024 * 1024

_PM_LAYOUT = (("dq", 1024), ("gv", 1024), ("iq", 2048), ("gq", 512), ("cq", 768), ("dk", 256),
              ("ckv", 512), ("gk", 512), ("dv", 256), ("kr", 128), ("krr", 128), ("glow", 128),
              ("ik", 128), ("iw", 128), ("pad", 128))
_PM_OFF = {}
_o = 0
for _n, _w in _PM_LAYOUT:
    assert _o % _w == 0
    _PM_OFF[_n] = _o
    _o += _w
PM_WIDTH = _o

BF16 = jnp.bfloat16
F32 = jnp.float32
INT_MIN = -2 ** 31


def _cp(n_axes, flags=None):
    return pltpu.CompilerParams(dimension_semantics=("arbitrary",) * n_axes,
                                vmem_limit_bytes=VMEM_LIMIT, flags=flags)


def _nt(a, b):
    return lax.dot_general(a, b, (((1,), (1,)), ((), ())), preferred_element_type=F32)


def _rms(x, g):
    return x * lax.rsqrt(jnp.mean(x * x, axis=-1, keepdims=True) + NORM_EPS) * g


def _mm_kernel(*refs, a_act, act, has_bias):
    if has_bias:
        a_ref, w_ref, b_ref, o_ref = refs
    else:
        a_ref, w_ref, o_ref = refs
    a = a_ref[...]
    if a_act == "silu":
        a = a.astype(F32)
        a = a * jax.nn.sigmoid(a)
    a = a.astype(BF16)
    for c0 in range(0, o_ref.shape[1], MM_CHUNK):
        cs = slice(c0, c0 + MM_CHUNK)
        r = jnp.dot(a, w_ref[:, cs].astype(BF16), preferred_element_type=F32)
        if has_bias:
            r = r + b_ref[:, cs]
        if act == "sigmoid":
            r = jax.nn.sigmoid(r)
        elif act == "silu":
            r = r * jax.nn.sigmoid(r)
        o_ref[:, cs] = r.astype(o_ref.dtype)


def _mm(a, w, bias, *, tm, tn, out_dtype, a_act=None, act=None, name):
    m, k = a.shape
    n = w.shape[1]
    in_specs = [pl.BlockSpec((tm, k), lambda i, j: (i, 0)),
                pl.BlockSpec((k, tn), lambda i, j: (0, j))]
    args = [a, w]
    if bias is not None:
        in_specs.append(pl.BlockSpec((1, tn), lambda i, j: (0, j)))
        args.append(bias)
    return pl.pallas_call(
        functools.partial(_mm_kernel, a_act=a_act, act=act, has_bias=bias is not None),
        grid=(m // tm, n // tn),
        in_specs=in_specs,
        out_specs=pl.BlockSpec((tm, tn), lambda i, j: (i, j)),
        out_shape=jax.ShapeDtypeStruct((m, n), out_dtype),
        compiler_params=_cp(2),
        name=name,
    )(*args)


def _norm_mod_kernel(x_ref, g_ref, sc_ref, sh_ref, o_ref):
    y = _rms(x_ref[0], g_ref[...])
    o_ref[0] = (y * (1.0 + sc_ref[0]) + sh_ref[0]).astype(o_ref.dtype)


def _norm_mod(x, g, scale, shift, *, tm=256):
    b, s, d = x.shape
    return pl.pallas_call(
        _norm_mod_kernel,
        grid=(b, s // tm),
        in_specs=[pl.BlockSpec((1, tm, d), lambda bi, i: (bi, i, 0)),
                  pl.BlockSpec((1, d), lambda bi, i: (0, 0)),
                  pl.BlockSpec((1, 1, d), lambda bi, i: (bi, 0, 0)),
                  pl.BlockSpec((1, 1, d), lambda bi, i: (bi, 0, 0))],
        out_specs=pl.BlockSpec((1, tm, d), lambda bi, i: (bi, i, 0)),
        out_shape=jax.ShapeDtypeStruct((b, s, d), BF16),
        compiler_params=_cp(2),
        name="norm_mod",
    )(x, g, scale, shift)


def _final_norm_kernel(x_ref, g_ref, o_ref):
    o_ref[0] = _rms(x_ref[0], g_ref[...])


def _final_norm(x, g, *, tm=256):
    b, s, d = x.shape
    return pl.pallas_call(
        _final_norm_kernel,
        grid=(b, s // tm),
        in_specs=[pl.BlockSpec((1, tm, d), lambda bi, i: (bi, i, 0)),
                  pl.BlockSpec((1, d), lambda bi, i: (0, 0))],
        out_specs=pl.BlockSpec((1, tm, d), lambda bi, i: (bi, i, 0)),
        out_shape=jax.ShapeDtypeStruct((b, s, d), F32),
        compiler_params=_cp(2),
        name="final_norm",
    )(x, g)


def _rope_kernel(pos_ref, inv_ref, cos_ref, sin_ref):
    ang = pos_ref[0] * inv_ref[...]
    live = lax.broadcasted_iota(jnp.int32, ang.shape, 1) < MLA_ROPE
    cos_ref[0] = jnp.where(live, jnp.cos(ang), 0.0)
    sin_ref[0] = jnp.where(live, jnp.sin(ang), 0.0)


def _rope_tables(positions, *, tm=512):
    b, s = positions.shape
    inv = ROPE_THETA ** (-jnp.arange(0, MLA_ROPE, 2, dtype=F32) / MLA_ROPE)
    inv128 = jnp.concatenate([inv, inv, jnp.zeros((LANE - MLA_ROPE,), F32)])[None, :]
    pos128 = jnp.broadcast_to(positions.astype(F32)[:, :, None], (b, s, LANE))
    spec = pl.BlockSpec((1, tm, LANE), lambda bi, i: (bi, i, 0))
    return pl.pallas_call(
        _rope_kernel,
        grid=(b, s // tm),
        in_specs=[spec, pl.BlockSpec((1, LANE), lambda bi, i: (0, 0))],
        out_specs=[spec, spec],
        out_shape=[jax.ShapeDtypeStruct((b, s, LANE), F32)] * 2,
        compiler_params=_cp(2),
        name="rope_tables",
    )(pos128, inv128)


def _mla_q_kernel(cq_ref, g_ref, w1_ref, w2_ref, cos_ref, sin_ref, o_ref, an_ref, *, scale):
    @pl.when(pl.program_id(2) == 0)
    def _():
        an_ref[...] = _rms(cq_ref[0].astype(F32), g_ref[...]).astype(BF16)

    a = an_ref[...]
    a1 = jnp.dot(a, w1_ref[...], preferred_element_type=F32)
    a2 = jnp.dot(a, w2_ref[...], preferred_element_type=F32)
    cos, sin = cos_ref[0], sin_ref[0]
    for u in range(MLA_HEADS_PER_STEP):
        c0 = u * MLA_QK_PAD
        rope = a1[:, c0 + MLA_NOPE:c0 + MLA_QK_PAD] * cos + a2[:, u * LANE:(u + 1) * LANE] * sin
        o_ref[0, :, c0:c0 + MLA_NOPE] = (a1[:, c0:c0 + MLA_NOPE] * scale).astype(BF16)
        o_ref[0, :, c0 + MLA_NOPE:c0 + MLA_QK_PAD] = (rope * scale).astype(BF16)


def _mla_q(pm3, g_q, w1, w2, cos128, sin128, *, tm=512):
    b, s, _ = pm3.shape
    hp = MLA_HEADS_PER_STEP
    cq_blk = _PM_OFF["cq"] // MLA_Q_RANK
    return pl.pallas_call(
        functools.partial(_mla_q_kernel, scale=(MLA_NOPE + MLA_ROPE) ** -0.5 * LOG2E),
        grid=(b, s // tm, MLA_HEADS // hp),
        in_specs=[pl.BlockSpec((1, tm, MLA_Q_RANK), lambda bi, i, h: (bi, i, cq_blk)),
                  pl.BlockSpec((1, MLA_Q_RANK), lambda bi, i, h: (0, 0)),
                  pl.BlockSpec((MLA_Q_RANK, hp * MLA_QK_PAD), lambda bi, i, h: (0, h)),
                  pl.BlockSpec((MLA_Q_RANK, hp * LANE), lambda bi, i, h: (0, h)),
                  pl.BlockSpec((1, tm, LANE), lambda bi, i, h: (bi, i, 0)),
                  pl.BlockSpec((1, tm, LANE), lambda bi, i, h: (bi, i, 0))],
        out_specs=pl.BlockSpec((1, tm, hp * MLA_QK_PAD), lambda bi, i, h: (bi, i, h)),
        out_shape=jax.ShapeDtypeStruct((b, s, MLA_HEADS * MLA_QK_PAD), BF16),
        scratch_shapes=[pltpu.VMEM((tm, MLA_Q_RANK), BF16)],
        compiler_params=_cp(3),
        name="mla_q",
    )(pm3, g_q, w1, w2, cos128, sin128)


def _mla_kv_kernel(ckv_ref, g_ref, w_ref, kr_ref, krr_ref, cos_ref, sin_ref, k_ref, v_ref, an_ref, krt_ref):
    @pl.when(pl.program_id(2) == 0)
    def _():
        an_ref[...] = _rms(ckv_ref[0].astype(F32), g_ref[...]).astype(BF16)
        kr = kr_ref[0].astype(F32) * cos_ref[0] + krr_ref[0].astype(F32) * sin_ref[0]
        krt_ref[...] = kr.T.astype(BF16)

    acc = jnp.dot(an_ref[...], w_ref[...], preferred_element_type=F32)
    ones = jnp.ones((acc.shape[0], MLA_V), BF16)
    for u in range(MLA_HEADS_PER_STEP):
        c0 = u * (MLA_NOPE + MLA_V)
        k_ref[0, u, 0, :MLA_NOPE, :] = acc[:, c0:c0 + MLA_NOPE].T.astype(BF16)
        k_ref[0, u, 0, MLA_NOPE:, :] = krt_ref[...]
        v_ref[0, u, 0, :, :MLA_V] = acc[:, c0 + MLA_NOPE:c0 + MLA_NOPE + MLA_V].astype(BF16)
        v_ref[0, u, 0, :, MLA_V:] = ones


def _mla_kv(pm3, g_kv, w_ukv, cos128, sin128):
    b, s, _ = pm3.shape
    tm = ATT_TK
    hp = MLA_HEADS_PER_STEP
    ckv_blk = _PM_OFF["ckv"] // MLA_KV_RANK
    kr_blk = _PM_OFF["kr"] // LANE
    krr_blk = _PM_OFF["krr"] // LANE
    return pl.pallas_call(
        _mla_kv_kernel,
        grid=(b, s // tm, MLA_HEADS // hp),
        in_specs=[pl.BlockSpec((1, tm, MLA_KV_RANK), lambda bi, i, h: (bi, i, ckv_blk)),
                  pl.BlockSpec((1, MLA_KV_RANK), lambda bi, i, h: (0, 0)),
                  pl.BlockSpec((MLA_KV_RANK, hp * (MLA_NOPE + MLA_V)), lambda bi, i, h: (0, h)),
                  pl.BlockSpec((1, tm, LANE), lambda bi, i, h: (bi, i, kr_blk)),
                  pl.BlockSpec((1, tm, LANE), lambda bi, i, h: (bi, i, krr_blk)),
                  pl.BlockSpec((1, tm, LANE), lambda bi, i, h: (bi, i, 0)),
                  pl.BlockSpec((1, tm, LANE), lambda bi, i, h: (bi, i, 0))],
        out_specs=[pl.BlockSpec((1, hp, 1, MLA_QK_PAD, tm), lambda bi, i, h: (bi, h, i, 0, 0)),
                   pl.BlockSpec((1, hp, 1, tm, 2 * MLA_V), lambda bi, i, h: (bi, h, i, 0, 0))],
        out_shape=[jax.ShapeDtypeStruct((b, MLA_HEADS, s // tm, MLA_QK_PAD, tm), BF16),
                   jax.ShapeDtypeStruct((b, MLA_HEADS, s // tm, tm, 2 * MLA_V), BF16)],
        scratch_shapes=[pltpu.VMEM((tm, MLA_KV_RANK), BF16), pltpu.VMEM((LANE, tm), BF16)],
        compiler_params=_cp(3),
        name="mla_kv",
    )(pm3, g_kv, w_ukv, pm3, pm3, cos128, sin128)


def _sm_init(acc_ref, m_ref):
    m_ref[...] = jnp.full(m_ref.shape, NEG, F32)
    acc_ref[...] = jnp.zeros(acc_ref.shape, F32)


def _sm_step(s, v_ext, acc_ref, m_ref):
    m_old = m_ref[...]
    m_new = jnp.maximum(m_old, jnp.max(s, axis=-1, keepdims=True))
    alpha = jnp.exp2(m_old - m_new)
    p = jnp.exp2(s - jnp.tile(m_new, (1, s.shape[1] // LANE)))
    pv = jnp.dot(p.astype(BF16), v_ext, preferred_element_type=F32)
    acc_ref[...] = jnp.tile(alpha, (1, acc_ref.shape[1] // LANE)) * acc_ref[...] + pv
    m_ref[...] = m_new


def _sm_result(acc_ref, d):
    acc = acc_ref[...]
    return acc[:, :d] / acc[:, d:]


def _mla_attn_kernel(q_ref, kt_ref, v_ref, z_ref, o_ref, s_ref, acc_ref, m_ref):
    i = pl.program_id(2)
    t = ATT_TK
    n_chains = ATT_TQ // t
    assert n_chains % 2 == 0
    chains = range(n_chains)
    qpos = lax.broadcasted_iota(jnp.int32, (t, t), 0)
    kpos = lax.broadcasted_iota(jnp.int32, (t, t), 1)

    def scores(u, j):
        return jnp.dot(q_ref[0, u * t:(u + 1) * t, :], kt_ref[0, 0, j], preferred_element_type=F32)

    def update(u, s, j, diagonal):
        if diagonal:
            s = jnp.where(kpos <= qpos, s, NEG)
        _sm_step(s, v_ref[0, 0, j], acc_ref.at[u], m_ref.at[u])

    def consume(u, buf, j, diagonal):
        update(u, s_ref[buf, u], j, diagonal)

    for u in chains:
        _sm_init(acc_ref.at[u], m_ref.at[u])
        s_ref[0, u] = scores(u, 0)

    def block_pair(p, carry):
        j = 2 * p
        for u in chains:
            consume(u, 0, j, False)
            s_ref[1, u] = scores(u, j + 1)
        for u in chains:
            consume(u, 1, j + 1, False)
            s_ref[0, u] = scores(u, j + 2)
        return carry

    first = n_chains * i
    lax.fori_loop(0, first // 2, block_pair, 0)
    for u in chains:
        consume(u, 0, first, u == 0)
        for d in range(1, u + 1):
            update(u, scores(u, first + d), first + d, d == u)
    for u in chains:
        rows = slice(u * t, (u + 1) * t)
        o_ref[0, rows, :] = (_sm_result(acc_ref.at[u], MLA_V) * z_ref[0, rows, :].astype(F32)).astype(o_ref.dtype)


def _mla_attn(q, kt, v_ext, pz3):
    b, s, _ = q.shape
    nk = s // ATT_TK
    return pl.pallas_call(
        _mla_attn_kernel,
        grid=(b, MLA_HEADS, s // ATT_TQ),
        in_specs=[pl.BlockSpec((1, ATT_TQ, MLA_QK_PAD), lambda bi, h, i: (bi, i, h)),
                  pl.BlockSpec((1, 1, nk, MLA_QK_PAD, ATT_TK), lambda bi, h, i: (bi, h, 0, 0, 0)),
                  pl.BlockSpec((1, 1, nk, ATT_TK, 2 * MLA_V), lambda bi, h, i: (bi, h, 0, 0, 0)),
                  pl.BlockSpec((1, ATT_TQ, MLA_V), lambda bi, h, i: (bi, i, h))],
        out_specs=pl.BlockSpec((1, ATT_TQ, MLA_V), lambda bi, h, i: (bi, i, h)),
        out_shape=jax.ShapeDtypeStruct((b, s, MLA_WIDTH), BF16),
        scratch_shapes=[pltpu.VMEM((2, ATT_TQ // ATT_TK, ATT_TK, ATT_TK), F32),
                        pltpu.VMEM((ATT_TQ // ATT_TK, ATT_TK, 2 * MLA_V), F32),
                        pltpu.VMEM((ATT_TQ // ATT_TK, ATT_TK, LANE), F32)],
        compiler_params=_cp(3),
        name="mla_attn",
    )(q, kt, v_ext, pz3)


def _log_sigmoid(x):
    return jnp.minimum(x, 0.0) - jnp.log(1.0 + jnp.exp(-jnp.abs(x)))


def _gla_kernel(q_ref, k_ref, v_ref, gl_ref, z_ref, w2_ref, bg_ref, go_ref, o_ref, st_ref, *, nb):
    @pl.when(pl.program_id(0) == 0)
    def _():
        st_ref[...] = jnp.zeros(st_ref.shape, F32)

    c = GLA_CHUNK
    row = lax.broadcasted_iota(jnp.int32, (c, c), 0)
    col = lax.broadcasted_iota(jnp.int32, (c, c), 1)
    tril = row >= col
    tril_b = jnp.where(tril, 1.0, 0.0).astype(BF16)
    qw = GLA_HEADS * GLA_DK
    for b in range(nb):
        pre = jnp.dot(gl_ref[b], w2_ref[...], preferred_element_type=F32) + bg_ref[...]
        glog = _log_sigmoid(pre) / GLA_GATE_NORM
        g1 = glog.astype(BF16)
        r1 = glog - g1.astype(F32)
        g2 = r1.astype(BF16)
        g3 = (r1 - g2.astype(F32)).astype(BF16)
        cs = jnp.dot(tril_b, jnp.concatenate([g1, g2, g3], axis=1), preferred_element_type=F32)
        bc_all = cs[:, :qw] + cs[:, qw:2 * qw] + cs[:, 2 * qw:]
        for h in range(GLA_HEADS):
            ks = slice(h * GLA_DK, (h + 1) * GLA_DK)
            vs = slice(h * GLA_DV, (h + 1) * GLA_DV)
            q = q_ref[b, :, ks].astype(F32)
            k = k_ref[b, :, ks].astype(F32)
            v = v_ref[b, :, vs]
            bc = bc_all[:, ks]
            b_last = bc[c - 1:c, :]
            q_dec = (q * GLA_DK ** -0.5 * jnp.exp(bc)).astype(BF16)
            k_inv = (k * jnp.exp(-bc)).astype(BF16)
            k_end = (k * jnp.exp(b_last - bc)).astype(BF16)
            decay = jnp.exp(b_last)
            attn = jnp.where(tril, _nt(q_dec, k_inv), 0.0).astype(BF16)
            st = st_ref[b * GLA_HEADS + h]
            o = jnp.dot(jnp.concatenate([q_dec, attn], axis=1),
                        jnp.concatenate([st.astype(BF16), v], axis=0), preferred_element_type=F32)
            upd = lax.dot_general(k_end, v, (((0,), (0,)), ((), ())), preferred_element_type=F32)
            decay_col = jnp.tile(jnp.broadcast_to(decay, (GLA_DK, GLA_DK)).T, (1, GLA_DV // GLA_DK))
            st_ref[b * GLA_HEADS + h] = st * decay_col + upd
            on = _rms(o, go_ref[...])
            o_ref[b, :, vs] = (on * z_ref[b, :, vs].astype(F32)).astype(o_ref.dtype)


def _gla(pm3, pz3, w2p, bg, gout):
    b, s, _ = pm3.shape
    c = GLA_CHUNK
    qw = GLA_HEADS * GLA_DK
    q_blk, k_blk = _PM_OFF["gq"] // qw, _PM_OFF["gk"] // qw
    v_blk, gl_blk = _PM_OFF["gv"] // GLA_WIDTH, _PM_OFF["glow"] // LANE
    z_blk = MLA_WIDTH // GLA_WIDTH
    return pl.pallas_call(
        functools.partial(_gla_kernel, nb=b),
        grid=(s // c,),
        in_specs=[pl.BlockSpec((b, c, qw), lambda i: (0, i, q_blk)),
                  pl.BlockSpec((b, c, qw), lambda i: (0, i, k_blk)),
                  pl.BlockSpec((b, c, GLA_WIDTH), lambda i: (0, i, v_blk)),
                  pl.BlockSpec((b, c, LANE), lambda i: (0, i, gl_blk)),
                  pl.BlockSpec((b, c, GLA_WIDTH), lambda i: (0, i, z_blk)),
                  pl.BlockSpec((LANE, qw), lambda i: (0, 0)),
                  pl.BlockSpec((1, qw), lambda i: (0, 0)),
                  pl.BlockSpec((1, GLA_DV), lambda i: (0, 0))],
        out_specs=pl.BlockSpec((b, c, GLA_WIDTH), lambda i: (0, i, 0)),
        out_shape=jax.ShapeDtypeStruct((b, s, GLA_WIDTH), BF16),
        scratch_shapes=[pltpu.VMEM((b * GLA_HEADS, GLA_DK, GLA_DV), F32)],
        compiler_params=_cp(1),
        name="gla",
    )(pm3, pm3, pm3, pm3, pz3, w2p, bg, gout)


def _dsa_kernel(iq_ref, iw_ref, onehot_ref, ikt_ref, q_ref, kt_ref, v_ref, z_ref, o_ref,
                key_ref, keyt_ref, iqs_ref, wb_ref, qs_ref, s_ref, acc_ref, m_ref, *, top_k):
    tq, tk = DSA_TQ, ATT_TK
    nk = key_ref.shape[0]
    lanes = tk // LANE
    i = pl.program_id(1)
    nkb = (i * tq + tq - 1) // tk + 1
    qpos = i * tq + lax.broadcasted_iota(jnp.int32, (tq, tk), 0)
    heads_per_dot = 4

    for h in range(IDX_HEADS):
        iqs_ref[h] = iq_ref[0, :, h * IDX_DH:(h + 1) * IDX_DH]

    wb = jnp.dot(iw_ref[0], onehot_ref[...], preferred_element_type=F32)
    for h in range(IDX_HEADS):
        wb_ref[h] = wb[:, h * LANE:(h + 1) * LANE] * (IDX_HEADS ** -0.5 * IDX_DH ** -0.5)

    def sortable(x):
        bits = lax.bitcast_convert_type(x, jnp.int32)
        return bits ^ ((bits >> 31) & 0x7FFFFFFF)

    qpos_t = i * tq + lax.broadcasted_iota(jnp.int32, (tk, tq), 1)

    def score_block(j, carry):
        ikt = ikt_ref[0, j]
        sc = jnp.zeros((tq, tk), F32)
        for hg in range(IDX_HEADS // heads_per_dot):
            iq = iqs_ref[hg * heads_per_dot:(hg + 1) * heads_per_dot].reshape(heads_per_dot * tq, IDX_DH)
            r = jnp.dot(iq, ikt, preferred_element_type=F32)
            for hh in range(heads_per_dot):
                wrow = jnp.tile(wb_ref[hg * heads_per_dot + hh], (1, lanes))
                sc = sc + jnp.maximum(r[hh * tq:(hh + 1) * tq, :], 0.0) * wrow
        kpos = j * tk + lax.broadcasted_iota(jnp.int32, (tq, tk), 1)
        key_ref[j] = jnp.where(kpos <= qpos, sortable(sc), INT_MIN)
        kpos_t = j * tk + lax.broadcasted_iota(jnp.int32, (tk, tq), 0)
        keyt_ref[j] = jnp.where(kpos_t <= qpos_t, sortable(sc.T), INT_MIN)
        return carry

    lax.fori_loop(0, nkb, score_block, 0)

    def count(pred):
        def body(j, acc):
            hit = jnp.where(pred(keyt_ref[j], j), 1, 0).astype(jnp.int32)
            return acc + hit.reshape(tk // 8, 8, tq).sum(axis=0)
        acc = lax.fori_loop(0, nkb, body, jnp.zeros((8, tq), jnp.int32))
        return acc.sum(axis=0, keepdims=True)

    def count_ge(cand):
        return count(lambda k, j: k >= cand)

    zero = jnp.zeros((1, tq), jnp.int32)
    c0 = count_ge(zero)
    thr = jnp.where(c0 >= top_k, zero, INT_MIN)
    n_ge = jnp.where(c0 >= top_k, c0, nkb * tk)

    def bit_step(it, carry):
        thr, n_ge = carry
        cand = thr | lax.shift_left(jnp.int32(1), 30 - it)
        c = count_ge(cand)
        return jnp.where(c >= top_k, cand, thr), jnp.where(c >= top_k, c, n_ge)

    thr, n_ge = lax.fori_loop(0, 31, bit_step, (thr, n_ge))

    n_gt = count(lambda k, j: k > thr)
    n_take = top_k - n_gt
    overflow = (thr > INT_MIN) & (n_ge - n_gt > n_take)

    @pl.when(jnp.max(jnp.where(overflow, 1, 0)) > 0)
    def _():
        rows = lax.broadcasted_iota(jnp.int32, (tk, tq), 0)
        bound = jnp.zeros((1, tq), jnp.int32)
        for bit in range((nk * tk - 1).bit_length() - 1, -1, -1):
            cand = bound | (1 << bit)
            below = count(lambda k, j: (k == thr) & (j * tk + rows < cand))
            bound = jnp.where(below < n_take, cand, bound)
        bound = jnp.where(overflow, bound, nk * tk)
        thr_c = jnp.tile(jnp.broadcast_to(thr, (tq, tq)).T, (1, lanes))
        bound_c = jnp.tile(jnp.broadcast_to(bound, (tq, tq)).T, (1, lanes))

        def demote(j, carry):
            k = key_ref[j]
            kpos = j * tk + lax.broadcasted_iota(jnp.int32, (tq, tk), 1)
            key_ref[j] = jnp.where((k == thr_c) & (kpos > bound_c), INT_MIN, k)
            return carry

        lax.fori_loop(0, nkb, demote, 0)

    thr = jnp.maximum(thr, INT_MIN + 1)
    thr_w = jnp.tile(jnp.broadcast_to(thr, (tq, tq)).T, (1, lanes))

    scale = DSA_DH ** -0.5 * LOG2E
    groups = range(DSA_KV_HEADS)

    def scores(g, j):
        return jnp.dot(qs_ref[g], kt_ref[0, g, j], preferred_element_type=F32)

    def consume(g, buf, j):
        sel = key_ref[j] >= thr_w
        s = s_ref[buf, g]
        s = jnp.concatenate([jnp.where(sel, s[r * tq:(r + 1) * tq, :], NEG) for r in range(DSA_REP)], axis=0)
        _sm_step(s, v_ref[0, g, j], acc_ref.at[g], m_ref.at[g])

    for g in groups:
        for r in range(DSA_REP):
            cs = slice((g * DSA_REP + r) * DSA_DH, (g * DSA_REP + r + 1) * DSA_DH)
            qs_ref[g, r * tq:(r + 1) * tq, :] = (q_ref[0, :, cs].astype(F32) * scale).astype(BF16)
        _sm_init(acc_ref.at[g], m_ref.at[g])
    for g in groups:
        s_ref[0, g] = scores(g, 0)

    def block_pair(p, carry):
        j = 2 * p
        for g in groups:
            consume(g, 0, j)
            s_ref[1, g] = scores(g, j + 1)
        for g in groups:
            consume(g, 1, j + 1)
            s_ref[0, g] = scores(g, jnp.minimum(j + 2, nk - 1))
        return carry

    lax.fori_loop(0, nkb // 2, block_pair, 0)

    @pl.when(nkb % 2 == 1)
    def _():
        for g in groups:
            consume(g, 0, nkb - 1)

    for g in groups:
        res = _sm_result(acc_ref.at[g], DSA_DH)
        for r in range(DSA_REP):
            cs = slice((g * DSA_REP + r) * DSA_DH, (g * DSA_REP + r + 1) * DSA_DH)
            o_ref[0, :, cs] = (res[r * tq:(r + 1) * tq, :] * z_ref[0, :, cs].astype(F32)).astype(o_ref.dtype)


def _dsa(pm3, pz3, iw3, ikt, dkt, dv_ext, top_k):
    b, s, _ = pm3.shape
    tq, tk = DSA_TQ, ATT_TK
    nk = s // tk
    q_blk = _PM_OFF["dq"] // DSA_WIDTH
    z_blk = (MLA_WIDTH + GLA_WIDTH) // DSA_WIDTH
    rows = DSA_REP * tq
    iq_blk = _PM_OFF["iq"] // (IDX_HEADS * IDX_DH)
    assert tq == LANE
    onehot = jnp.repeat(jnp.eye(IDX_HEADS, dtype=BF16), LANE, axis=1)
    return pl.pallas_call(
        functools.partial(_dsa_kernel, top_k=top_k),
        grid=(b, s // tq),
        in_specs=[pl.BlockSpec((1, tq, IDX_HEADS * IDX_DH), lambda bi, i: (bi, i, iq_blk)),
                  pl.BlockSpec((1, tq, IDX_HEADS), lambda bi, i: (bi, i, 0)),
                  pl.BlockSpec((IDX_HEADS, IDX_HEADS * LANE), lambda bi, i: (0, 0)),
                  pl.BlockSpec((1, nk, IDX_DH, tk), lambda bi, i: (bi, 0, 0, 0)),
                  pl.BlockSpec((1, tq, DSA_WIDTH), lambda bi, i: (bi, i, q_blk)),
                  pl.BlockSpec((1, DSA_KV_HEADS, nk, DSA_DH, tk), lambda bi, i: (bi, 0, 0, 0, 0)),
                  pl.BlockSpec((1, DSA_KV_HEADS, nk, tk, 2 * DSA_DH), lambda bi, i: (bi, 0, 0, 0, 0)),
                  pl.BlockSpec((1, tq, DSA_WIDTH), lambda bi, i: (bi, i, z_blk))],
        out_specs=pl.BlockSpec((1, tq, DSA_WIDTH), lambda bi, i: (bi, i, 0)),
        out_shape=jax.ShapeDtypeStruct((b, s, DSA_WIDTH), BF16),
        scratch_shapes=[pltpu.VMEM((nk, tq, tk), jnp.int32),
                        pltpu.VMEM((nk, tk, tq), jnp.int32),
                        pltpu.VMEM((IDX_HEADS, tq, IDX_DH), BF16),
                        pltpu.VMEM((IDX_HEADS, tq, LANE), F32),
                        pltpu.VMEM((DSA_KV_HEADS, rows, DSA_DH), BF16),
                        pltpu.VMEM((2, DSA_KV_HEADS, rows, tk), F32),
                        pltpu.VMEM((DSA_KV_HEADS, rows, 2 * DSA_DH), F32),
                        pltpu.VMEM((DSA_KV_HEADS, rows, LANE), F32)],
        compiler_params=_cp(2),
        name="dsa",
    )(pm3, iw3, onehot, ikt, pm3, dkt, dv_ext, pz3)


def _lift_kernel(ym_ref, yg_ref, yd_ref, wm_ref, wg_ref, wd_ref, ga_ref, gb_ref, gc_ref, o_ref):
    a = jnp.dot(ym_ref[...], wm_ref[...], preferred_element_type=F32)
    b = jnp.dot(yg_ref[...], wg_ref[...], preferred_element_type=F32)
    c = jnp.dot(yd_ref[...], wd_ref[...], preferred_element_type=F32)
    o = ga_ref[...].astype(F32) * a + gb_ref[...].astype(F32) * b + gc_ref[...].astype(F32) * c
    o_ref[...] = o.astype(o_ref.dtype)


def _lift(y_mla, y_gla, y_dsa, w_bm, w_bg, w_bd, gates, *, tm=512, tn=1024):
    t = y_mla.shape[0]
    d = w_bm.shape[1]
    nj = d // tn
    row = lambda width: pl.BlockSpec((tm, width), lambda i, j: (i, 0))
    wcol = lambda depth: pl.BlockSpec((depth, tn), lambda i, j: (0, j))
    gate = lambda br: pl.BlockSpec((tm, tn), lambda i, j: (i, br * nj + j))
    return pl.pallas_call(
        _lift_kernel,
        grid=(t // tm, nj),
        in_specs=[row(MLA_WIDTH), row(GLA_WIDTH), row(DSA_WIDTH),
                  wcol(MLA_WIDTH), wcol(GLA_WIDTH), wcol(DSA_WIDTH),
                  gate(0), gate(1), gate(2)],
        out_specs=pl.BlockSpec((tm, tn), lambda i, j: (i, j)),
        out_shape=jax.ShapeDtypeStruct((t, d), BF16),
        compiler_params=_cp(2),
        name="lift_merge",
    )(y_mla, y_gla, y_dsa, w_bm, w_bg, w_bd, gates, gates, gates)


def _out_kernel(m_ref, w_ref, x_ref, gate_ref, o_ref):
    r = jnp.dot(m_ref[0], w_ref[...], preferred_element_type=F32)
    o_ref[0] = x_ref[0] + gate_ref[0] * r


def _out_proj(merged3, w_o, x, gate, *, tm=1024, tn=1024):
    b, s, d = x.shape
    tm = min(tm, s)
    return pl.pallas_call(
        _out_kernel,
        grid=(b, s // tm, d // tn),
        in_specs=[pl.BlockSpec((1, tm, d), lambda bi, i, j: (bi, i, 0)),
                  pl.BlockSpec((d, tn), lambda bi, i, j: (0, j)),
                  pl.BlockSpec((1, tm, tn), lambda bi, i, j: (bi, i, j)),
                  pl.BlockSpec((1, 1, tn), lambda bi, i, j: (bi, 0, j))],
        out_specs=pl.BlockSpec((1, tm, tn), lambda bi, i, j: (bi, i, j)),
        out_shape=jax.ShapeDtypeStruct((b, s, d), F32),
        compiler_params=_cp(3),
        name="out_proj",
    )(merged3, w_o, x, gate)


def _main_weight(w_in):
    d = w_in.shape[0]

    def cols(name):
        return w_in[:, _IN_OFF[name]:_IN_OFF[name] + _IN_W[name]]

    def padded(w, width):
        return jnp.concatenate([w, jnp.zeros((d, width - w.shape[1]), w.dtype)], axis=1)

    kr = cols("krope")
    half = MLA_ROPE // 2
    parts = {"kr": padded(kr, LANE),
             "krr": padded(jnp.concatenate([-kr[:, half:], kr[:, :half]], axis=1), LANE),
             "glow": padded(cols("glow"), LANE), "ik": padded(cols("ik"), LANE),
             "iw": padded(cols("iw"), LANE), "pad": jnp.zeros((d, LANE), w_in.dtype)}
    return jnp.concatenate([parts[n] if n in parts else cols(n) for n, _ in _PM_LAYOUT], axis=1).astype(BF16)


def _mla_q_weights(w_uq):
    r = w_uq.shape[0]
    w = w_uq.reshape(r, MLA_HEADS, MLA_NOPE + MLA_ROPE)
    nope, rope = w[..., :MLA_NOPE], w[..., MLA_NOPE:]
    half = MLA_ROPE // 2
    zeros = jnp.zeros((r, MLA_HEADS, LANE - MLA_ROPE), w.dtype)
    w1 = jnp.concatenate([nope, rope, zeros], axis=-1).reshape(r, MLA_HEADS * MLA_QK_PAD)
    rot = jnp.concatenate([-rope[..., half:], rope[..., :half]], axis=-1)
    w2 = jnp.concatenate([rot, zeros], axis=-1).reshape(r, MLA_HEADS * LANE)
    return w1.astype(BF16), w2.astype(BF16)


def _layer(x, c_pad, cos128, sin128, norm_g, w_ada, b_ada, w_in, mla_gq, mla_wuq, mla_gkv, mla_wukv,
           gla_wg2, gla_bg, gla_gout, w_mg, b_mg, w_bm, w_bg, w_bd, w_o):
    b, s, d = x.shape
    t = b * s
    mod = _mm(c_pad, w_ada, b_ada[None, :], tm=c_pad.shape[0], tn=512, out_dtype=F32,
              a_act="silu", name="ada")[:b]
    shift, scale, gate = (mod[:, None, k * d:(k + 1) * d] for k in range(3))
    h = _norm_mod(x, norm_g[None, :], scale, shift)
    h2 = h.reshape(t, d)

    mm_tm = min(1024, t)
    pm = _mm(h2, _main_weight(w_in), None, tm=mm_tm, tn=1536, out_dtype=BF16, name="proj_main")
    z0 = _IN_OFF["z_mla"]
    pz = _mm(h2, w_in[:, z0:].astype(BF16), None, tm=mm_tm, tn=1024, out_dtype=BF16, act="silu",
             name="proj_gate_paths")
    gates = _mm(h2, w_mg, b_mg[None, :], tm=mm_tm, tn=512, out_dtype=BF16,
                act="sigmoid", name="merge_gates")
    pm3 = pm.reshape(b, s, PM_WIDTH)
    pz3 = pz.reshape(b, s, d)

    w1, w2 = _mla_q_weights(mla_wuq)
    q = _mla_q(pm3, mla_gq[None, :], w1, w2, cos128, sin128)
    kcat, vt = _mla_kv(pm3, mla_gkv[None, :], mla_wukv.astype(BF16), cos128, sin128)
    y_mla = _mla_attn(q, kcat, vt, pz3)

    w2p = jnp.concatenate([gla_wg2, jnp.zeros((LANE - GLA_GATE_RANK, gla_wg2.shape[1]), gla_wg2.dtype)],
                          axis=0).astype(BF16)
    y_gla = _gla(pm3, pz3, w2p, gla_bg[None, :], gla_gout[None, :])

    nk = s // ATT_TK

    def piece(name, width):
        return pm3[:, :, _PM_OFF[name]:_PM_OFF[name] + width]

    iw3 = piece("iw", IDX_HEADS)
    ikt = piece("ik", IDX_DH).reshape(b, nk, ATT_TK, IDX_DH).transpose(0, 1, 3, 2)
    kv5 = (b, nk, ATT_TK, DSA_KV_HEADS, DSA_DH)
    dkt = piece("dk", DSA_KV_HEADS * DSA_DH).reshape(kv5).transpose(0, 3, 1, 4, 2)
    dv = piece("dv", DSA_KV_HEADS * DSA_DH).reshape(kv5).transpose(0, 3, 1, 2, 4)
    dv_ext = jnp.concatenate([dv, jnp.ones_like(dv)], axis=-1)
    y_dsa = _dsa(pm3, pz3, iw3, ikt, dkt, dv_ext, min(IDX_TOPK, s // 4))

    merged = _lift(y_mla.reshape(t, MLA_WIDTH), y_gla.reshape(t, GLA_WIDTH), y_dsa.reshape(t, DSA_WIDTH),
                   w_bm.astype(BF16), w_bg.astype(BF16), w_bd.astype(BF16), gates)
    return _out_proj(merged.reshape(b, s, d), w_o.astype(BF16), x, gate)


def kernel(x, c, positions, norm_g, w_ada, b_ada, w_in, mla_gq, mla_wuq, mla_gkv, mla_wukv, gla_wg2,
           gla_bg, gla_gout, w_mg, b_mg, w_bm, w_bg, w_bd, w_o, final_g):
    b = x.shape[0]
    cos128, sin128 = _rope_tables(positions)
    c_pad = jnp.concatenate([c, jnp.zeros((8 - b, c.shape[1]), c.dtype)], axis=0)
    for l in range(DEPTH):
        x = _layer(x, c_pad, cos128, sin128, norm_g[l], w_ada[l], b_ada[l], w_in[l], mla_gq[l],
                   mla_wuq[l], mla_gkv[l], mla_wukv[l], gla_wg2[l], gla_bg[l], gla_gout[l],
                   w_mg[l], b_mg[l], w_bm[l], w_bg[l], w_bd[l], w_o[l])
    return _final_norm(x, final_g[None, :])
```

```python
import functools

import jax
import jax.numpy as jnp
import numpy as np
from jax import lax
from jax.experimental import pallas as pl
from jax.experimental.pallas import tpu as pltpu

D_MODEL = 4096
DEPTH = 2
MLA_HEADS = 16
MLA_Q_RANK = 768
MLA_KV_RANK = 512
MLA_NOPE = 128
MLA_ROPE = 64
MLA_V = 128
ROPE_THETA = 10000.0
GLA_HEADS = 4
GLA_DK = 128
GLA_DV = 256
GLA_GATE_RANK = 16
GLA_GATE_NORM = 16.0
GLA_CHUNK = 64
DSA_HEADS = 8
DSA_KV_HEADS = 2
DSA_DH = 128
IDX_HEADS = 32
IDX_DH = 64
IDX_TOPK = 256
NORM_EPS = 1e-6
NEG = -1e30

MLA_WIDTH = MLA_HEADS * MLA_V
GLA_WIDTH = GLA_HEADS * GLA_DV
DSA_WIDTH = DSA_HEADS * DSA_DH
DSA_REP = DSA_HEADS // DSA_KV_HEADS

IN_SPLITS = (
    MLA_Q_RANK, MLA_KV_RANK, MLA_ROPE,
    GLA_HEADS * GLA_DK, GLA_HEADS * GLA_DK, GLA_WIDTH, GLA_GATE_RANK,
    DSA_WIDTH, DSA_KV_HEADS * DSA_DH, DSA_KV_HEADS * DSA_DH,
    IDX_HEADS * IDX_DH, IDX_DH, IDX_HEADS,
    MLA_WIDTH, GLA_WIDTH, DSA_WIDTH,
)
_IN_NAMES = ("cq", "ckv", "krope", "gq", "gk", "gv", "glow", "dq", "dk", "dv",
             "iq", "ik", "iw", "z_mla", "z_gla", "z_dsa")
_IN_OFF = dict(zip(_IN_NAMES, np.concatenate([[0], np.cumsum(IN_SPLITS)[:-1]]).tolist()))
_IN_W = dict(zip(_IN_NAMES, IN_SPLITS))

LANE = 128
MLA_QK_PAD = 256
ATT_TQ = 2048
ATT_TK = 512
MLA_HEADS_PER_STEP = 4
LOG2E = 1.4426950408889634
DSA_TQ = 128
MM_CHUNK = 256
VMEM_LIMIT = 56 * 1024 * 1024

_PM_LAYOUT = (("dq", 1024), ("gv", 1024), ("iq", 2048), ("gq", 512), ("cq", 768), ("dk", 256),
              ("ckv", 512), ("gk", 512), ("dv", 256), ("kr", 128), ("krr", 128), ("glow", 128),
              ("ik", 128), ("iw", 128), ("pad", 128))
_PM_OFF = {}
_o = 0
for _n, _w in _PM_LAYOUT:
    assert _o % _w == 0
    _PM_OFF[_n] = _o
    _o += _w
PM_WIDTH = _o

BF16 = jnp.bfloat16
F32 = jnp.float32
INT_MIN = -2 ** 31


def _cp(n_axes, flags=None):
    return pltpu.CompilerParams(dimension_semantics=("arbitrary",) * n_axes,
                                vmem_limit_bytes=VMEM_LIMIT, flags=flags)


def _nt(a, b):
    return lax.dot_general(a, b, (((1,), (1,)), ((), ())), preferred_element_type=F32)


def _rms(x, g):
    return x * lax.rsqrt(jnp.mean(x * x, axis=-1, keepdims=True) + NORM_EPS) * g


def _mm_kernel(*refs, a_act, act, has_bias):
    if has_bias:
        a_ref, w_ref, b_ref, o_ref = refs
    else:
        a_ref, w_ref, o_ref = refs
    a = a_ref[...]
    if a_act == "silu":
        a = a.astype(F32)
        a = a * jax.nn.sigmoid(a)
    a = a.astype(BF16)
    for c0 in range(0, o_ref.shape[1], MM_CHUNK):
        cs = slice(c0, c0 + MM_CHUNK)
        r = jnp.dot(a, w_ref[:, cs].astype(BF16), preferred_element_type=F32)
        if has_bias:
            r = r + b_ref[:, cs]
        if act == "sigmoid":
            r = jax.nn.sigmoid(r)
        elif act == "silu":
            r = r * jax.nn.sigmoid(r)
        o_ref[:, cs] = r.astype(o_ref.dtype)


def _mm(a, w, bias, *, tm, tn, out_dtype, a_act=None, act=None, name):
    m, k = a.shape
    n = w.shape[1]
    in_specs = [pl.BlockSpec((tm, k), lambda i, j: (i, 0)),
                pl.BlockSpec((k, tn), lambda i, j: (0, j))]
    args = [a, w]
    if bias is not None:
        in_specs.append(pl.BlockSpec((1, tn), lambda i, j: (0, j)))
        args.append(bias)
    return pl.pallas_call(
        functools.partial(_mm_kernel, a_act=a_act, act=act, has_bias=bias is not None),
        grid=(m // tm, n // tn),
        in_specs=in_specs,
        out_specs=pl.BlockSpec((tm, tn), lambda i, j: (i, j)),
        out_shape=jax.ShapeDtypeStruct((m, n), out_dtype),
        compiler_params=_cp(2),
        name=name,
    )(*args)


def _norm_mod_kernel(x_ref, g_ref, sc_ref, sh_ref, o_ref):
    y = _rms(x_ref[0], g_ref[...])
    o_ref[0] = (y * (1.0 + sc_ref[0]) + sh_ref[0]).astype(o_ref.dtype)


def _norm_mod(x, g, scale, shift, *, tm=256):
    b, s, d = x.shape
    return pl.pallas_call(
        _norm_mod_kernel,
        grid=(b, s // tm),
        in_specs=[pl.BlockSpec((1, tm, d), lambda bi, i: (bi, i, 0)),
                  pl.BlockSpec((1, d), lambda bi, i: (0, 0)),
                  pl.BlockSpec((1, 1, d), lambda bi, i: (bi, 0, 0)),
                  pl.BlockSpec((1, 1, d), lambda bi, i: (bi, 0, 0))],
        out_specs=pl.BlockSpec((1, tm, d), lambda bi, i: (bi, i, 0)),
        out_shape=jax.ShapeDtypeStruct((b, s, d), BF16),
        compiler_params=_cp(2),
        name="norm_mod",
    )(x, g, scale, shift)


def _final_norm_kernel(x_ref, g_ref, o_ref):
    o_ref[0] = _rms(x_ref[0], g_ref[...])


def _final_norm(x, g, *, tm=256):
    b, s, d = x.shape
    return pl.pallas_call(
        _final_norm_kernel,
        grid=(b, s // tm),
        in_specs=[pl.BlockSpec((1, tm, d), lambda bi, i: (bi, i, 0)),
                  pl.BlockSpec((1, d), lambda bi, i: (0, 0))],
        out_specs=pl.BlockSpec((1, tm, d), lambda bi, i: (bi, i, 0)),
        out_shape=jax.ShapeDtypeStruct((b, s, d), F32),
        compiler_params=_cp(2),
        name="final_norm",
    )(x, g)


def _rope_kernel(pos_ref, inv_ref, cos_ref, sin_ref):
    ang = pos_ref[0] * inv_ref[...]
    live = lax.broadcasted_iota(jnp.int32, ang.shape, 1) < MLA_ROPE
    cos_ref[0] = jnp.where(live, jnp.cos(ang), 0.0)
    sin_ref[0] = jnp.where(live, jnp.sin(ang), 0.0)


def _rope_tables(positions, *, tm=512):
    b, s = positions.shape
    inv = ROPE_THETA ** (-jnp.arange(0, MLA_ROPE, 2, dtype=F32) / MLA_ROPE)
    inv128 = jnp.concatenate([inv, inv, jnp.zeros((LANE - MLA_ROPE,), F32)])[None, :]
    pos128 = jnp.broadcast_to(positions.astype(F32)[:, :, None], (b, s, LANE))
    spec = pl.BlockSpec((1, tm, LANE), lambda bi, i: (bi, i, 0))
    return pl.pallas_call(
        _rope_kernel,
        grid=(b, s // tm),
        in_specs=[spec, pl.BlockSpec((1, LANE), lambda bi, i: (0, 0))],
        out_specs=[spec, spec],
        out_shape=[jax.ShapeDtypeStruct((b, s, LANE), F32)] * 2,
        compiler_params=_cp(2),
        name="rope_tables",
    )(pos128, inv128)


def _mla_q_kernel(cq_ref, g_ref, w1_ref, w2_ref, cos_ref, sin_ref, o_ref, an_ref, *, scale):
    @pl.when(pl.program_id(2) == 0)
    def _():
        an_ref[...] = _rms(cq_ref[0].astype(F32), g_ref[...]).astype(BF16)

    a = an_ref[...]
    a1 = jnp.dot(a, w1_ref[...], preferred_element_type=F32)
    a2 = jnp.dot(a, w2_ref[...], preferred_element_type=F32)
    cos, sin = cos_ref[0], sin_ref[0]
    for u in range(MLA_HEADS_PER_STEP):
        c0 = u * MLA_QK_PAD
        rope = a1[:, c0 + MLA_NOPE:c0 + MLA_QK_PAD] * cos + a2[:, u * LANE:(u + 1) * LANE] * sin
        o_ref[0, :, c0:c0 + MLA_NOPE] = (a1[:, c0:c0 + MLA_NOPE] * scale).astype(BF16)
        o_ref[0, :, c0 + MLA_NOPE:c0 + MLA_QK_PAD] = (rope * scale).astype(BF16)


def _mla_q(pm3, g_q, w1, w2, cos128, sin128, *, tm=512):
    b, s, _ = pm3.shape
    hp = MLA_HEADS_PER_STEP
    cq_blk = _PM_OFF["cq"] // MLA_Q_RANK
    return pl.pallas_call(
        functools.partial(_mla_q_kernel, scale=(MLA_NOPE + MLA_ROPE) ** -0.5 * LOG2E),
        grid=(b, s // tm, MLA_HEADS // hp),
        in_specs=[pl.BlockSpec((1, tm, MLA_Q_RANK), lambda bi, i, h: (bi, i, cq_blk)),
                  pl.BlockSpec((1, MLA_Q_RANK), lambda bi, i, h: (0, 0)),
                  pl.BlockSpec((MLA_Q_RANK, hp * MLA_QK_PAD), lambda bi, i, h: (0, h)),
                  pl.BlockSpec((MLA_Q_RANK, hp * LANE), lambda bi, i, h: (0, h)),
                  pl.BlockSpec((1, tm, LANE), lambda bi, i, h: (bi, i, 0)),
                  pl.BlockSpec((1, tm, LANE), lambda bi, i, h: (bi, i, 0))],
        out_specs=pl.BlockSpec((1, tm, hp * MLA_QK_PAD), lambda bi, i, h: (bi, i, h)),
        out_shape=jax.ShapeDtypeStruct((b, s, MLA_HEADS * MLA_QK_PAD), BF16),
        scratch_shapes=[pltpu.VMEM((tm, MLA_Q_RANK), BF16)],
        compiler_params=_cp(3),
        name="mla_q",
    )(pm3, g_q, w1, w2, cos128, sin128)


def _mla_kv_kernel(ckv_ref, g_ref, w_ref, kr_ref, krr_ref, cos_ref, sin_ref, k_ref, v_ref, an_ref, krt_ref):
    @pl.when(pl.program_id(2) == 0)
    def _():
        an_ref[...] = _rms(ckv_ref[0].astype(F32), g_ref[...]).astype(BF16)
        kr = kr_ref[0].astype(F32) * cos_ref[0] + krr_ref[0].astype(F32) * sin_ref[0]
        krt_ref[...] = kr.T.astype(BF16)

    acc = jnp.dot(an_ref[...], w_ref[...], preferred_element_type=F32)
    ones = jnp.ones((acc.shape[0], MLA_V), BF16)
    for u in range(MLA_HEADS_PER_STEP):
        c0 = u * (MLA_NOPE + MLA_V)
        k_ref[0, u, 0, :MLA_NOPE, :] = acc[:, c0:c0 + MLA_NOPE].T.astype(BF16)
        k_ref[0, u, 0, MLA_NOPE:, :] = krt_ref[...]
        v_ref[0, u, 0, :, :MLA_V] = acc[:, c0 + MLA_NOPE:c0 + MLA_NOPE + MLA_V].astype(BF16)
        v_ref[0, u, 0, :, MLA_V:] = ones


def _mla_kv(pm3, g_kv, w_ukv, cos128, sin128):
    b, s, _ = pm3.shape
    tm = ATT_TK
    hp = MLA_HEADS_PER_STEP
    ckv_blk = _PM_OFF["ckv"] // MLA_KV_RANK
    kr_blk = _PM_OFF["kr"] // LANE
    krr_blk = _PM_OFF["krr"] // LANE
    return pl.pallas_call(
        _mla_kv_kernel,
        grid=(b, s // tm, MLA_HEADS // hp),
        in_specs=[pl.BlockSpec((1, tm, MLA_KV_RANK), lambda bi, i, h: (bi, i, ckv_blk)),
                  pl.BlockSpec((1, MLA_KV_RANK), lambda bi, i, h: (0, 0)),
                  pl.BlockSpec((MLA_KV_RANK, hp * (MLA_NOPE + MLA_V)), lambda bi, i, h: (0, h)),
                  pl.BlockSpec((1, tm, LANE), lambda bi, i, h: (bi, i, kr_blk)),
                  pl.BlockSpec((1, tm, LANE), lambda bi, i, h: (bi, i, krr_blk)),
                  pl.BlockSpec((1, tm, LANE), lambda bi, i, h: (bi, i, 0)),
                  pl.BlockSpec((1, tm, LANE), lambda bi, i, h: (bi, i, 0))],
        out_specs=[pl.BlockSpec((1, hp, 1, MLA_QK_PAD, tm), lambda bi, i, h: (bi, h, i, 0, 0)),
                   pl.BlockSpec((1, hp, 1, tm, 2 * MLA_V), lambda bi, i, h: (bi, h, i, 0, 0))],
        out_shape=[jax.ShapeDtypeStruct((b, MLA_HEADS, s // tm, MLA_QK_PAD, tm), BF16),
                   jax.ShapeDtypeStruct((b, MLA_HEADS, s // tm, tm, 2 * MLA_V), BF16)],
        scratch_shapes=[pltpu.VMEM((tm, MLA_KV_RANK), BF16), pltpu.VMEM((LANE, tm), BF16)],
        compiler_params=_cp(3),
        name="mla_kv",
    )(pm3, g_kv, w_ukv, pm3, pm3, cos128, sin128)


def _sm_init(acc_ref, m_ref):
    m_ref[...] = jnp.full(m_ref.shape, NEG, F32)
    acc_ref[...] = jnp.zeros(acc_ref.shape, F32)


def _sm_step(s, v_ext, acc_ref, m_ref):
    m_old = m_ref[...]
    m_new = jnp.maximum(m_old, jnp.max(s, axis=-1, keepdims=True))
    alpha = jnp.exp2(m_old - m_new)
    p = jnp.exp2(s - jnp.tile(m_new, (1, s.shape[1] // LANE)))
    pv = jnp.dot(p.astype(BF16), v_ext, preferred_element_type=F32)
    acc_ref[...] = jnp.tile(alpha, (1, acc_ref.shape[1] // LANE)) * acc_ref[...] + pv
    m_ref[...] = m_new


def _sm_result(acc_ref, d):
    acc = acc_ref[...]
    return acc[:, :d] / acc[:, d:]


def _mla_attn_kernel(q_ref, kt_ref, v_ref, z_ref, o_ref, s_ref, acc_ref, m_ref):
    i = pl.program_id(2)
    t = ATT_TK
    n_chains = ATT_TQ // t
    assert n_chains % 2 == 0
    chains = range(n_chains)
    qpos = lax.broadcasted_iota(jnp.int32, (t, t), 0)
    kpos = lax.broadcasted_iota(jnp.int32, (t, t), 1)

    def scores(u, j):
        return jnp.dot(q_ref[0, u * t:(u + 1) * t, :], kt_ref[0, 0, j], preferred_element_type=F32)

    def update(u, s, j, diagonal):
        if diagonal:
            s = jnp.where(kpos <= qpos, s, NEG)
        _sm_step(s, v_ref[0, 0, j], acc_ref.at[u], m_ref.at[u])

    def consume(u, buf, j, diagonal):
        update(u, s_ref[buf, u], j, diagonal)

    for u in chains:
        _sm_init(acc_ref.at[u], m_ref.at[u])
        s_ref[0, u] = scores(u, 0)

    def block_pair(p, carry):
        j = 2 * p
        for u in chains:
            consume(u, 0, j, False)
            s_ref[1, u] = scores(u, j + 1)
        for u in chains:
            consume(u, 1, j + 1, False)
            s_ref[0, u] = scores(u, j + 2)
        return carry

    first = n_chains * i
    lax.fori_loop(0, first // 2, block_pair, 0)
    for u in chains:
        consume(u, 0, first, u == 0)
        for d in range(1, u + 1):
            update(u, scores(u, first + d), first + d, d == u)
    for u in chains:
        rows = slice(u * t, (u + 1) * t)
        o_ref[0, rows, :] = (_sm_result(acc_ref.at[u], MLA_V) * z_ref[0, rows, :].astype(F32)).astype(o_ref.dtype)


def _mla_attn(q, kt, v_ext, pz3):
    b, s, _ = q.shape
    nk = s // ATT_TK
    return pl.pallas_call(
        _mla_attn_kernel,
        grid=(b, MLA_HEADS, s // ATT_TQ),
        in_specs=[pl.BlockSpec((1, ATT_TQ, MLA_QK_PAD), lambda bi, h, i: (bi, i, h)),
                  pl.BlockSpec((1, 1, nk, MLA_QK_PAD, ATT_TK), lambda bi, h, i: (bi, h, 0, 0, 0)),
                  pl.BlockSpec((1, 1, nk, ATT_TK, 2 * MLA_V), lambda bi, h, i: (bi, h, 0, 0, 0)),
                  pl.BlockSpec((1, ATT_TQ, MLA_V), lambda bi, h, i: (bi, i, h))],
        out_specs=pl.BlockSpec((1, ATT_TQ, MLA_V), lambda bi, h, i: (bi, i, h)),
        out_shape=jax.ShapeDtypeStruct((b, s, MLA_WIDTH), BF16),
        scratch_shapes=[pltpu.VMEM((2, ATT_TQ // ATT_TK, ATT_TK, ATT_TK), F32),
                        pltpu.VMEM((ATT_TQ // ATT_TK, ATT_TK, 2 * MLA_V), F32),
                        pltpu.VMEM((ATT_TQ // ATT_TK, ATT_TK, LANE), F32)],
        compiler_params=_cp(3),
        name="mla_attn",
    )(q, kt, v_ext, pz3)


def _log_sigmoid(x):
    return jnp.minimum(x, 0.0) - jnp.log(1.0 + jnp.exp(-jnp.abs(x)))


def _gla_kernel(q_ref, k_ref, v_ref, gl_ref, z_ref, w2_ref, bg_ref, go_ref, o_ref, st_ref, *, nb):
    @pl.when(pl.program_id(0) == 0)
    def _():
        st_ref[...] = jnp.zeros(st_ref.shape, F32)

    c = GLA_CHUNK
    row = lax.broadcasted_iota(jnp.int32, (c, c), 0)
    col = lax.broadcasted_iota(jnp.int32, (c, c), 1)
    tril = row >= col
    tril_b = jnp.where(tril, 1.0, 0.0).astype(BF16)
    qw = GLA_HEADS * GLA_DK
    for b in range(nb):
        pre = jnp.dot(gl_ref[b], w2_ref[...], preferred_element_type=F32) + bg_ref[...]
        glog = _log_sigmoid(pre) / GLA_GATE_NORM
        g1 = glog.astype(BF16)
        r1 = glog - g1.astype(F32)
        g2 = r1.astype(BF16)
        g3 = (r1 - g2.astype(F32)).astype(BF16)
        cs = jnp.dot(tril_b, jnp.concatenate([g1, g2, g3], axis=1), preferred_element_type=F32)
        bc_all = cs[:, :qw] + cs[:, qw:2 * qw] + cs[:, 2 * qw:]
        for h in range(GLA_HEADS):
            ks = slice(h * GLA_DK, (h + 1) * GLA_DK)
            vs = slice(h * GLA_DV, (h + 1) * GLA_DV)
            q = q_ref[b, :, ks].astype(F32)
            k = k_ref[b, :, ks].astype(F32)
            v = v_ref[b, :, vs]
            bc = bc_all[:, ks]
            b_last = bc[c - 1:c, :]
            q_dec = (q * GLA_DK ** -0.5 * jnp.exp(bc)).astype(BF16)
            k_inv = (k * jnp.exp(-bc)).astype(BF16)
            k_end = (k * jnp.exp(b_last - bc)).astype(BF16)
            decay = jnp.exp(b_last)
            attn = jnp.where(tril, _nt(q_dec, k_inv), 0.0).astype(BF16)
            st = st_ref[b * GLA_HEADS + h]
            o = jnp.dot(jnp.concatenate([q_dec, attn], axis=1),
                        jnp.concatenate([st.astype(BF16), v], axis=0), preferred_element_type=F32)
            upd = lax.dot_general(k_end, v, (((0,), (0,)), ((), ())), preferred_element_type=F32)
            decay_col = jnp.tile(jnp.broadcast_to(decay, (GLA_DK, GLA_DK)).T, (1, GLA_DV // GLA_DK))
            st_ref[b * GLA_HEADS + h] = st * decay_col + upd
            on = _rms(o, go_ref[...])
            o_ref[b, :, vs] = (on * z_ref[b, :, vs].astype(F32)).astype(o_ref.dtype)


def _gla(pm3, pz3, w2p, bg, gout):
    b, s, _ = pm3.shape
    c = GLA_CHUNK
    qw = GLA_HEADS * GLA_DK
    q_blk, k_blk = _PM_OFF["gq"] // qw, _PM_OFF["gk"] // qw
    v_blk, gl_blk = _PM_OFF["gv"] // GLA_WIDTH, _PM_OFF["glow"] // LANE
    z_blk = MLA_WIDTH // GLA_WIDTH
    return pl.pallas_call(
        functools.partial(_gla_kernel, nb=b),
        grid=(s // c,),
        in_specs=[pl.BlockSpec((b, c, qw), lambda i: (0, i, q_blk)),
                  pl.BlockSpec((b, c, qw), lambda i: (0, i, k_blk)),
                  pl.BlockSpec((b, c, GLA_WIDTH), lambda i: (0, i, v_blk)),
                  pl.BlockSpec((b, c, LANE), lambda i: (0, i, gl_blk)),
                  pl.BlockSpec((b, c, GLA_WIDTH), lambda i: (0, i, z_blk)),
                  pl.BlockSpec((LANE, qw), lambda i: (0, 0)),
                  pl.BlockSpec((1, qw), lambda i: (0, 0)),
                  pl.BlockSpec((1, GLA_DV), lambda i: (0, 0))],
        out_specs=pl.BlockSpec((b, c, GLA_WIDTH), lambda i: (0, i, 0)),
        out_shape=jax.ShapeDtypeStruct((b, s, GLA_WIDTH), BF16),
        scratch_shapes=[pltpu.VMEM((b * GLA_HEADS, GLA_DK, GLA_DV), F32)],
        compiler_params=_cp(1),
        name="gla",
    )(pm3, pm3, pm3, pm3, pz3, w2p, bg, gout)


def _dsa_kernel(iq_ref, iw_ref, onehot_ref, ikt_ref, q_ref, kt_ref, v_ref, z_ref, o_ref,
                key_ref, keyt_ref, iqs_ref, wb_ref, qs_ref, s_ref, acc_ref, m_ref, *, top_k):
    tq, tk = DSA_TQ, ATT_TK
    nk = key_ref.shape[0]
    lanes = tk // LANE
    i = pl.program_id(1)
    nkb = (i * tq + tq - 1) // tk + 1
    qpos = i * tq + lax.broadcasted_iota(jnp.int32, (tq, tk), 0)
    heads_per_dot = 4

    for h in range(IDX_HEADS):
        iqs_ref[h] = iq_ref[0, :, h * IDX_DH:(h + 1) * IDX_DH]

    wb = jnp.dot(iw_ref[0], onehot_ref[...], preferred_element_type=F32)
    for h in range(IDX_HEADS):
        wb_ref[h] = wb[:, h * LANE:(h + 1) * LANE] * (IDX_HEADS ** -0.5 * IDX_DH ** -0.5)

    def sortable(x):
        bits = lax.bitcast_convert_type(x, jnp.int32)
        return bits ^ ((bits >> 31) & 0x7FFFFFFF)

    qpos_t = i * tq + lax.broadcasted_iota(jnp.int32, (tk, tq), 1)

    def score_block(j, carry):
        ikt = ikt_ref[0, j]
        sc = jnp.zeros((tq, tk), F32)
        for hg in range(IDX_HEADS // heads_per_dot):
            iq = iqs_ref[hg * heads_per_dot:(hg + 1) * heads_per_dot].reshape(heads_per_dot * tq, IDX_DH)
            r = jnp.dot(iq, ikt, preferred_element_type=F32)
            for hh in range(heads_per_dot):
                wrow = jnp.tile(wb_ref[hg * heads_per_dot + hh], (1, lanes))
                sc = sc + jnp.maximum(r[hh * tq:(hh + 1) * tq, :], 0.0) * wrow
        kpos = j * tk + lax.broadcasted_iota(jnp.int32, (tq, tk), 1)
        key_ref[j] = jnp.where(kpos <= qpos, sortable(sc), INT_MIN)
        kpos_t = j * tk + lax.broadcasted_iota(jnp.int32, (tk, tq), 0)
        keyt_ref[j] = jnp.where(kpos_t <= qpos_t, sortable(sc.T), INT_MIN)
        return carry

    lax.fori_loop(0, nkb, score_block, 0)

    def count(pred):
        def body(j, acc):
            hit = jnp.where(pred(keyt_ref[j], j), 1, 0).astype(jnp.int32)
            return acc + hit.reshape(tk // 8, 8, tq).sum(axis=0)
        acc = lax.fori_loop(0, nkb, body, jnp.zeros((8, tq), jnp.int32))
        return acc.sum(axis=0, keepdims=True)

    def count_ge(cand):
        return count(lambda k, j: k >= cand)

    zero = jnp.zeros((1, tq), jnp.int32)
    c0 = count_ge(zero)
    thr = jnp.where(c0 >= top_k, zero, INT_MIN)
    n_ge = jnp.where(c0 >= top_k, c0, nkb * tk)

    def bit_step(it, carry):
        thr, n_ge = carry
        cand = thr | lax.shift_left(jnp.int32(1), 30 - it)
        c = count_ge(cand)
        return jnp.where(c >= top_k, cand, thr), jnp.where(c >= top_k, c, n_ge)

    thr, n_ge = lax.fori_loop(0, 31, bit_step, (thr, n_ge))

    n_gt = count(lambda k, j: k > thr)
    n_take = top_k - n_gt
    overflow = (thr > INT_MIN) & (n_ge - n_gt > n_take)

    @pl.when(jnp.max(jnp.where(overflow, 1, 0)) > 0)
    def _():
        rows = lax.broadcasted_iota(jnp.int32, (tk, tq), 0)
        bound = jnp.zeros((1, tq), jnp.int32)
        for bit in range((nk * tk - 1).bit_length() - 1, -1, -1):
            cand = bound | (1 << bit)
            below = count(lambda k, j: (k == thr) & (j * tk + rows < cand))
            bound = jnp.where(below < n_take, cand, bound)
        bound = jnp.where(overflow, bound, nk * tk)
        thr_c = jnp.tile(jnp.broadcast_to(thr, (tq, tq)).T, (1, lanes))
        bound_c = jnp.tile(jnp.broadcast_to(bound, (tq, tq)).T, (1, lanes))

        def demote(j, carry):
            k = key_ref[j]
            kpos = j * tk + lax.broadcasted_iota(jnp.int32, (tq, tk), 1)
            key_ref[j] = jnp.where((k == thr_c) & (kpos > bound_c), INT_MIN, k)
            return carry

        lax.fori_loop(0, nkb, demote, 0)

    thr = jnp.maximum(thr, INT_MIN + 1)
    thr_w = jnp.tile(jnp.broadcast_to(thr, (tq, tq)).T, (1, lanes))

    scale = DSA_DH ** -0.5 * LOG2E
    groups = range(DSA_KV_HEADS)

    def scores(g, j):
        return jnp.dot(qs_ref[g], kt_ref[0, g, j], preferred_element_type=F32)

    def consume(g, buf, j):
        sel = key_ref[j] >= thr_w
        s = s_ref[buf, g]
        s = jnp.concatenate([jnp.where(sel, s[r * tq:(r + 1) * tq, :], NEG) for r in range(DSA_REP)], axis=0)
        _sm_step(s, v_ref[0, g, j], acc_ref.at[g], m_ref.at[g])

    for g in groups:
        for r in range(DSA_REP):
            cs = slice((g * DSA_REP + r) * DSA_DH, (g * DSA_REP + r + 1) * DSA_DH)
            qs_ref[g, r * tq:(r + 1) * tq, :] = (q_ref[0, :, cs].astype(F32) * scale).astype(BF16)
        _sm_init(acc_ref.at[g], m_ref.at[g])
    for g in groups:
        s_ref[0, g] = scores(g, 0)

    def block_pair(p, carry):
        j = 2 * p
        for g in groups:
            consume(g, 0, j)
            s_ref[1, g] = scores(g, j + 1)
        for g in groups:
            consume(g, 1, j + 1)
            s_ref[0, g] = scores(g, jnp.minimum(j + 2, nk - 1))
        return carry

    lax.fori_loop(0, nkb // 2, block_pair, 0)

    @pl.when(nkb % 2 == 1)
    def _():
        for g in groups:
            consume(g, 0, nkb - 1)

    for g in groups:
        res = _sm_result(acc_ref.at[g], DSA_DH)
        for r in range(DSA_REP):
            cs = slice((g * DSA_REP + r) * DSA_DH, (g * DSA_REP + r + 1) * DSA_DH)
            o_ref[0, :, cs] = (res[r * tq:(r + 1) * tq, :] * z_ref[0, :, cs].astype(F32)).astype(o_ref.dtype)


def _dsa(pm3, pz3, ikt, dkt, dv_ext, top_k):
    b, s, _ = pm3.shape
    tq, tk = DSA_TQ, ATT_TK
    nk = s // tk
    q_blk = _PM_OFF["dq"] // DSA_WIDTH
    z_blk = (MLA_WIDTH + GLA_WIDTH) // DSA_WIDTH
    rows = DSA_REP * tq
    iq_blk = _PM_OFF["iq"] // (IDX_HEADS * IDX_DH)
    assert tq == LANE
    iw_blk = _PM_OFF["iw"] // LANE
    onehot = jnp.repeat(jnp.eye(LANE, IDX_HEADS, dtype=BF16), LANE, axis=1)
    return pl.pallas_call(
        functools.partial(_dsa_kernel, top_k=top_k),
        grid=(b, s // tq),
        in_specs=[pl.BlockSpec((1, tq, IDX_HEADS * IDX_DH), lambda bi, i: (bi, i, iq_blk)),
                  pl.BlockSpec((1, tq, LANE), lambda bi, i: (bi, i, iw_blk)),
                  pl.BlockSpec((LANE, IDX_HEADS * LANE), lambda bi, i: (0, 0)),
                  pl.BlockSpec((1, nk, IDX_DH, tk), lambda bi, i: (bi, 0, 0, 0)),
                  pl.BlockSpec((1, tq, DSA_WIDTH), lambda bi, i: (bi, i, q_blk)),
                  pl.BlockSpec((1, DSA_KV_HEADS, nk, DSA_DH, tk), lambda bi, i: (bi, 0, 0, 0, 0)),
                  pl.BlockSpec((1, DSA_KV_HEADS, nk, tk, 2 * DSA_DH), lambda bi, i: (bi, 0, 0, 0, 0)),
                  pl.BlockSpec((1, tq, DSA_WIDTH), lambda bi, i: (bi, i, z_blk))],
        out_specs=pl.BlockSpec((1, tq, DSA_WIDTH), lambda bi, i: (bi, i, 0)),
        out_shape=jax.ShapeDtypeStruct((b, s, DSA_WIDTH), BF16),
        scratch_shapes=[pltpu.VMEM((nk, tq, tk), jnp.int32),
                        pltpu.VMEM((nk, tk, tq), jnp.int32),
                        pltpu.VMEM((IDX_HEADS, tq, IDX_DH), BF16),
                        pltpu.VMEM((IDX_HEADS, tq, LANE), F32),
                        pltpu.VMEM((DSA_KV_HEADS, rows, DSA_DH), BF16),
                        pltpu.VMEM((2, DSA_KV_HEADS, rows, tk), F32),
                        pltpu.VMEM((DSA_KV_HEADS, rows, 2 * DSA_DH), F32),
                        pltpu.VMEM((DSA_KV_HEADS, rows, LANE), F32)],
        compiler_params=_cp(2),
        name="dsa",
    )(pm3, pm3, onehot, ikt, pm3, dkt, dv_ext, pz3)


def _dsa_layout_kernel(ik_ref, dk_ref, dv_ref, ikt_ref, kt_ref, vx_ref):
    ikt_ref[0, 0] = ik_ref[0].astype(F32).T[:IDX_DH, :].astype(BF16)
    ones = jnp.ones((ATT_TK, DSA_DH), BF16)
    for g in range(DSA_KV_HEADS):
        cs = slice(g * DSA_DH, (g + 1) * DSA_DH)
        kt_ref[0, g, 0] = dk_ref[0, :, cs].astype(F32).T.astype(BF16)
        vx_ref[0, g, 0, :, :DSA_DH] = dv_ref[0, :, cs]
        vx_ref[0, g, 0, :, DSA_DH:] = ones


def _dsa_layout(pm3):
    b, s, _ = pm3.shape
    tk = ATT_TK
    nk = s // tk
    kvw = DSA_KV_HEADS * DSA_DH
    ik_blk, dk_blk, dv_blk = _PM_OFF["ik"] // LANE, _PM_OFF["dk"] // kvw, _PM_OFF["dv"] // kvw
    return pl.pallas_call(
        _dsa_layout_kernel,
        grid=(b, nk),
        in_specs=[pl.BlockSpec((1, tk, LANE), lambda bi, j: (bi, j, ik_blk)),
                  pl.BlockSpec((1, tk, kvw), lambda bi, j: (bi, j, dk_blk)),
                  pl.BlockSpec((1, tk, kvw), lambda bi, j: (bi, j, dv_blk))],
        out_specs=[pl.BlockSpec((1, 1, IDX_DH, tk), lambda bi, j: (bi, j, 0, 0)),
                   pl.BlockSpec((1, DSA_KV_HEADS, 1, DSA_DH, tk), lambda bi, j: (bi, 0, j, 0, 0)),
                   pl.BlockSpec((1, DSA_KV_HEADS, 1, tk, 2 * DSA_DH), lambda bi, j: (bi, 0, j, 0, 0))],
        out_shape=[jax.ShapeDtypeStruct((b, nk, IDX_DH, tk), BF16),
                   jax.ShapeDtypeStruct((b, DSA_KV_HEADS, nk, DSA_DH, tk), BF16),
                   jax.ShapeDtypeStruct((b, DSA_KV_HEADS, nk, tk, 2 * DSA_DH), BF16)],
        compiler_params=_cp(2),
        name="dsa_layout",
    )(pm3, pm3, pm3)


def _lift_kernel(ym_ref, yg_ref, yd_ref, wm_ref, wg_ref, wd_ref, ga_ref, gb_ref, gc_ref, o_ref):
    a = jnp.dot(ym_ref[...], wm_ref[...], preferred_element_type=F32)
    b = jnp.dot(yg_ref[...], wg_ref[...], preferred_element_type=F32)
    c = jnp.dot(yd_ref[...], wd_ref[...], preferred_element_type=F32)
    o = ga_ref[...].astype(F32) * a + gb_ref[...].astype(F32) * b + gc_ref[...].astype(F32) * c
    o_ref[...] = o.astype(o_ref.dtype)


def _lift(y_mla, y_gla, y_dsa, w_bm, w_bg, w_bd, gates, *, tm=512, tn=1024):
    t = y_mla.shape[0]
    d = w_bm.shape[1]
    nj = d // tn
    row = lambda width: pl.BlockSpec((tm, width), lambda i, j: (i, 0))
    wcol = lambda depth: pl.BlockSpec((depth, tn), lambda i, j: (0, j))
    gate = lambda br: pl.BlockSpec((tm, tn), lambda i, j: (i, br * nj + j))
    return pl.pallas_call(
        _lift_kernel,
        grid=(t // tm, nj),
        in_specs=[row(MLA_WIDTH), row(GLA_WIDTH), row(DSA_WIDTH),
                  wcol(MLA_WIDTH), wcol(GLA_WIDTH), wcol(DSA_WIDTH),
                  gate(0), gate(1), gate(2)],
        out_specs=pl.BlockSpec((tm, tn), lambda i, j: (i, j)),
        out_shape=jax.ShapeDtypeStruct((t, d), BF16),
        compiler_params=_cp(2),
        name="lift_merge",
    )(y_mla, y_gla, y_dsa, w_bm, w_bg, w_bd, gates, gates, gates)


def _out_kernel(m_ref, w_ref, x_ref, gate_ref, o_ref):
    r = jnp.dot(m_ref[0], w_ref[...], preferred_element_type=F32)
    o_ref[0] = x_ref[0] + gate_ref[0] * r


def _out_proj(merged3, w_o, x, gate, *, tm=1024, tn=1024):
    b, s, d = x.shape
    tm = min(tm, s)
    return pl.pallas_call(
        _out_kernel,
        grid=(b, s // tm, d // tn),
        in_specs=[pl.BlockSpec((1, tm, d), lambda bi, i, j: (bi, i, 0)),
                  pl.BlockSpec((d, tn), lambda bi, i, j: (0, j)),
                  pl.BlockSpec((1, tm, tn), lambda bi, i, j: (bi, i, j)),
                  pl.BlockSpec((1, 1, tn), lambda bi, i, j: (bi, 0, j))],
        out_specs=pl.BlockSpec((1, tm, tn), lambda bi, i, j: (bi, i, j)),
        out_shape=jax.ShapeDtypeStruct((b, s, d), F32),
        compiler_params=_cp(3),
        name="out_proj",
    )(merged3, w_o, x, gate)


def _win_layout_kernel(w_ref, om_ref, oz_ref):
    half = MLA_ROPE // 2
    for name, width in _PM_LAYOUT:
        dst = _PM_OFF[name]
        if name == "pad":
            om_ref[:, dst:dst + width] = jnp.zeros((om_ref.shape[0], width), BF16)
        elif name in ("kr", "krr", "glow", "ik", "iw"):
            src_name = {"kr": "krope", "krr": "krope"}.get(name, name)
            src, w = _IN_OFF[src_name], _IN_W[src_name]
            om_ref[:, dst:dst + width] = jnp.zeros((om_ref.shape[0], width), BF16)
            if name == "krr":
                om_ref[:, dst:dst + half] = (-w_ref[:, src + half:src + w]).astype(BF16)
                om_ref[:, dst + half:dst + w] = w_ref[:, src:src + half].astype(BF16)
            else:
                om_ref[:, dst:dst + w] = w_ref[:, src:src + w].astype(BF16)
        else:
            src = _IN_OFF[name]
            om_ref[:, dst:dst + width] = w_ref[:, src:src + width].astype(BF16)
    oz_ref[...] = w_ref[:, _IN_OFF["z_mla"]:].astype(BF16)


def _win_layout(w_in, *, tr=128):
    d, n = w_in.shape
    nz = n - _IN_OFF["z_mla"]
    return pl.pallas_call(
        _win_layout_kernel,
        grid=(d // tr,),
        in_specs=[pl.BlockSpec((tr, n), lambda i: (i, 0))],
        out_specs=[pl.BlockSpec((tr, PM_WIDTH), lambda i: (i, 0)),
                   pl.BlockSpec((tr, nz), lambda i: (i, 0))],
        out_shape=[jax.ShapeDtypeStruct((d, PM_WIDTH), BF16), jax.ShapeDtypeStruct((d, nz), BF16)],
        compiler_params=_cp(1),
        name="win_layout",
    )(w_in)


def _mla_q_weights(w_uq):
    r = w_uq.shape[0]
    w = w_uq.reshape(r, MLA_HEADS, MLA_NOPE + MLA_ROPE)
    nope, rope = w[..., :MLA_NOPE], w[..., MLA_NOPE:]
    half = MLA_ROPE // 2
    zeros = jnp.zeros((r, MLA_HEADS, LANE - MLA_ROPE), w.dtype)
    w1 = jnp.concatenate([nope, rope, zeros], axis=-1).reshape(r, MLA_HEADS * MLA_QK_PAD)
    rot = jnp.concatenate([-rope[..., half:], rope[..., :half]], axis=-1)
    w2 = jnp.concatenate([rot, zeros], axis=-1).reshape(r, MLA_HEADS * LANE)
    return w1.astype(BF16), w2.astype(BF16)


def _layer(x, c_pad, cos128, sin128, norm_g, w_ada, b_ada, w_in, mla_gq, mla_wuq, mla_gkv, mla_wukv,
           gla_wg2, gla_bg, gla_gout, w_mg, b_mg, w_bm, w_bg, w_bd, w_o):
    b, s, d = x.shape
    t = b * s
    mod = _mm(c_pad, w_ada, b_ada[None, :], tm=c_pad.shape[0], tn=512, out_dtype=F32,
              a_act="silu", name="ada")[:b]
    shift, scale, gate = (mod[:, None, k * d:(k + 1) * d] for k in range(3))
    h = _norm_mod(x, norm_g[None, :], scale, shift)
    h2 = h.reshape(t, d)

    mm_tm = min(1024, t)
    w_main, w_z = _win_layout(w_in)
    pm = _mm(h2, w_main, None, tm=mm_tm, tn=1536, out_dtype=BF16, name="proj_main")
    pz = _mm(h2, w_z, None, tm=mm_tm, tn=1024, out_dtype=BF16, act="silu", name="proj_gate_paths")
    gates = _mm(h2, w_mg, b_mg[None, :], tm=mm_tm, tn=512, out_dtype=BF16,
                act="sigmoid", name="merge_gates")
    pm3 = pm.reshape(b, s, PM_WIDTH)
    pz3 = pz.reshape(b, s, d)

    w1, w2 = _mla_q_weights(mla_wuq)
    q = _mla_q(pm3, mla_gq[None, :], w1, w2, cos128, sin128)
    kcat, vt = _mla_kv(pm3, mla_gkv[None, :], mla_wukv.astype(BF16), cos128, sin128)
    y_mla = _mla_attn(q, kcat, vt, pz3)

    w2p = jnp.concatenate([gla_wg2, jnp.zeros((LANE - GLA_GATE_RANK, gla_wg2.shape[1]), gla_wg2.dtype)],
                          axis=0).astype(BF16)
    y_gla = _gla(pm3, pz3, w2p, gla_bg[None, :], gla_gout[None, :])

    ikt, dkt, dv_ext = _dsa_layout(pm3)
    y_dsa = _dsa(pm3, pz3, ikt, dkt, dv_ext, min(IDX_TOPK, s // 4))

    merged = _lift(y_mla.reshape(t, MLA_WIDTH), y_gla.reshape(t, GLA_WIDTH), y_dsa.reshape(t, DSA_WIDTH),
                   w_bm.astype(BF16), w_bg.astype(BF16), w_bd.astype(BF16), gates)
    return _out_proj(merged.reshape(b, s, d), w_o.astype(BF16), x, gate)


def kernel(x, c, positions, norm_g, w_ada, b_ada, w_in, mla_gq, mla_wuq, mla_gkv, mla_wukv, gla_wg2,
           gla_bg, gla_gout, w_mg, b_mg, w_bm, w_bg, w_bd, w_o, final_g):
    b = x.shape[0]
    cos128, sin128 = _rope_tables(positions)
    c_pad = jnp.concatenate([c, jnp.zeros((8 - b, c.shape[1]), c.dtype)], axis=0)
    for l in range(DEPTH):
        x = _layer(x, c_pad, cos128, sin128, norm_g[l], w_ada[l], b_ada[l], w_in[l], mla_gq[l],
                   mla_wuq[l], mla_gkv[l], mla_wukv[l], gla_wg2[l], gla_bg[l], gla_gout[l],
                   w_mg[l], b_mg[l], w_bm[l], w_bg[l], w_bd[l], w_o[l])
    return _final_norm(x, final_g[None, :])
```

```python
import functools

import jax
import jax.numpy as jnp
import numpy as np
from jax import lax
from jax.experimental import pallas as pl
from jax.experimental.pallas import tpu as pltpu

D_MODEL = 4096
DEPTH = 2
MLA_HEADS = 16
MLA_Q_RANK = 768
MLA_KV_RANK = 512
MLA_NOPE = 128
MLA_ROPE = 64
MLA_V = 128
ROPE_THETA = 10000.0
GLA_HEADS = 4
GLA_DK = 128
GLA_DV = 256
GLA_GATE_RANK = 16
GLA_GATE_NORM = 16.0
GLA_CHUNK = 64
DSA_HEADS = 8
DSA_KV_HEADS = 2
DSA_DH = 128
IDX_HEADS = 32
IDX_DH = 64
IDX_TOPK = 256
NORM_EPS = 1e-6
NEG = -1e30

MLA_WIDTH = MLA_HEADS * MLA_V
GLA_WIDTH = GLA_HEADS * GLA_DV
DSA_WIDTH = DSA_HEADS * DSA_DH
DSA_REP = DSA_HEADS // DSA_KV_HEADS

IN_SPLITS = (
    MLA_Q_RANK, MLA_KV_RANK, MLA_ROPE,
    GLA_HEADS * GLA_DK, GLA_HEADS * GLA_DK, GLA_WIDTH, GLA_GATE_RANK,
    DSA_WIDTH, DSA_KV_HEADS * DSA_DH, DSA_KV_HEADS * DSA_DH,
    IDX_HEADS * IDX_DH, IDX_DH, IDX_HEADS,
    MLA_WIDTH, GLA_WIDTH, DSA_WIDTH,
)
_IN_NAMES = ("cq", "ckv", "krope", "gq", "gk", "gv", "glow", "dq", "dk", "dv",
             "iq", "ik", "iw", "z_mla", "z_gla", "z_dsa")
_IN_OFF = dict(zip(_IN_NAMES, np.concatenate([[0], np.cumsum(IN_SPLITS)[:-1]]).tolist()))
_IN_W = dict(zip(_IN_NAMES, IN_SPLITS))

LANE = 128
MLA_QK_PAD = 256
ATT_TQ = 2048
ATT_TK = 512
MLA_HEADS_PER_STEP = 4
LOG2E = 1.4426950408889634
DSA_TQ = 128
MM_CHUNK = 256
VMEM_LIMIT = 56 * 1024 * 1024

_PM_LAYOUT = (("dq", 1024), ("gv", 1024), ("iq", 2048), ("gq", 512), ("cq", 768), ("dk", 256),
              ("ckv", 512), ("gk", 512), ("dv", 256), ("kr", 128), ("krr", 128), ("glow", 128),
              ("ik", 128), ("iw", 128), ("pad", 128))
_PM_OFF = {}
_o = 0
for _n, _w in _PM_LAYOUT:
    assert _o % _w == 0
    _PM_OFF[_n] = _o
    _o += _w
PM_WIDTH = _o

BF16 = jnp.bfloat16
F32 = jnp.float32
INT_MIN = -2 ** 31


def _cp(n_axes, flags=None):
    return pltpu.CompilerParams(dimension_semantics=("arbitrary",) * n_axes,
                                vmem_limit_bytes=VMEM_LIMIT, flags=flags)


def _nt(a, b):
    return lax.dot_general(a, b, (((1,), (1,)), ((), ())), preferred_element_type=F32)


def _rms(x, g):
    return x * lax.rsqrt(jnp.mean(x * x, axis=-1, keepdims=True) + NORM_EPS) * g


def _mm_kernel(*refs, a_act, act, has_bias):
    if has_bias:
        a_ref, w_ref, b_ref, o_ref = refs
    else:
        a_ref, w_ref, o_ref = refs
    a = a_ref[...]
    if a_act == "silu":
        a = a.astype(F32)
        a = a * jax.nn.sigmoid(a)
    a = a.astype(BF16)
    for c0 in range(0, o_ref.shape[1], MM_CHUNK):
        cs = slice(c0, c0 + MM_CHUNK)
        r = jnp.dot(a, w_ref[:, cs].astype(BF16), preferred_element_type=F32)
        if has_bias:
            r = r + b_ref[:, cs]
        if act == "sigmoid":
            r = jax.nn.sigmoid(r)
        elif act == "silu":
            r = r * jax.nn.sigmoid(r)
        o_ref[:, cs] = r.astype(o_ref.dtype)


def _mm(a, w, bias, *, tm, tn, out_dtype, a_act=None, act=None, layer=None, name):
    m, k = a.shape
    n = w.shape[-1]
    if layer is None:
        w_spec = pl.BlockSpec((k, tn), lambda i, j: (0, j))
    else:
        w_spec = pl.BlockSpec((None, k, tn), lambda i, j: (layer, 0, j))
    in_specs = [pl.BlockSpec((tm, k), lambda i, j: (i, 0)), w_spec]
    args = [a, w]
    if bias is not None:
        in_specs.append(pl.BlockSpec((1, tn), lambda i, j: (0, j)))
        args.append(bias)
    return pl.pallas_call(
        functools.partial(_mm_kernel, a_act=a_act, act=act, has_bias=bias is not None),
        grid=(m // tm, n // tn),
        in_specs=in_specs,
        out_specs=pl.BlockSpec((tm, tn), lambda i, j: (i, j)),
        out_shape=jax.ShapeDtypeStruct((m, n), out_dtype),
        compiler_params=_cp(2),
        name=name,
    )(*args)


def _norm_mod_kernel(x_ref, g_ref, sc_ref, sh_ref, o_ref):
    y = _rms(x_ref[0], g_ref[...])
    o_ref[0] = (y * (1.0 + sc_ref[0]) + sh_ref[0]).astype(o_ref.dtype)


def _norm_mod(x, g, scale, shift, *, tm=256):
    b, s, d = x.shape
    return pl.pallas_call(
        _norm_mod_kernel,
        grid=(b, s // tm),
        in_specs=[pl.BlockSpec((1, tm, d), lambda bi, i: (bi, i, 0)),
                  pl.BlockSpec((1, d), lambda bi, i: (0, 0)),
                  pl.BlockSpec((1, 1, d), lambda bi, i: (bi, 0, 0)),
                  pl.BlockSpec((1, 1, d), lambda bi, i: (bi, 0, 0))],
        out_specs=pl.BlockSpec((1, tm, d), lambda bi, i: (bi, i, 0)),
        out_shape=jax.ShapeDtypeStruct((b, s, d), BF16),
        compiler_params=_cp(2),
        name="norm_mod",
    )(x, g, scale, shift)


def _final_norm_kernel(x_ref, g_ref, o_ref):
    o_ref[0] = _rms(x_ref[0], g_ref[...])


def _final_norm(x, g, *, tm=256):
    b, s, d = x.shape
    return pl.pallas_call(
        _final_norm_kernel,
        grid=(b, s // tm),
        in_specs=[pl.BlockSpec((1, tm, d), lambda bi, i: (bi, i, 0)),
                  pl.BlockSpec((1, d), lambda bi, i: (0, 0))],
        out_specs=pl.BlockSpec((1, tm, d), lambda bi, i: (bi, i, 0)),
        out_shape=jax.ShapeDtypeStruct((b, s, d), F32),
        compiler_params=_cp(2),
        name="final_norm",
    )(x, g)


def _rope_kernel(pos_ref, inv_ref, cos_ref, sin_ref):
    ang = pos_ref[0] * inv_ref[...]
    live = lax.broadcasted_iota(jnp.int32, ang.shape, 1) < MLA_ROPE
    cos_ref[0] = jnp.where(live, jnp.cos(ang), 0.0)
    sin_ref[0] = jnp.where(live, jnp.sin(ang), 0.0)


def _rope_tables(positions, *, tm=512):
    b, s = positions.shape
    inv = ROPE_THETA ** (-jnp.arange(0, MLA_ROPE, 2, dtype=F32) / MLA_ROPE)
    inv128 = jnp.concatenate([inv, inv, jnp.zeros((LANE - MLA_ROPE,), F32)])[None, :]
    pos128 = jnp.broadcast_to(positions.astype(F32)[:, :, None], (b, s, LANE))
    spec = pl.BlockSpec((1, tm, LANE), lambda bi, i: (bi, i, 0))
    return pl.pallas_call(
        _rope_kernel,
        grid=(b, s // tm),
        in_specs=[spec, pl.BlockSpec((1, LANE), lambda bi, i: (0, 0))],
        out_specs=[spec, spec],
        out_shape=[jax.ShapeDtypeStruct((b, s, LANE), F32)] * 2,
        compiler_params=_cp(2),
        name="rope_tables",
    )(pos128, inv128)


def _mla_q_kernel(cq_ref, g_ref, w1_ref, w2_ref, cos_ref, sin_ref, o_ref, an_ref, *, scale):
    @pl.when(pl.program_id(2) == 0)
    def _():
        an_ref[...] = _rms(cq_ref[0].astype(F32), g_ref[...]).astype(BF16)

    a = an_ref[...]
    a1 = jnp.dot(a, w1_ref[...], preferred_element_type=F32)
    a2 = jnp.dot(a, w2_ref[...], preferred_element_type=F32)
    cos, sin = cos_ref[0], sin_ref[0]
    for u in range(MLA_HEADS_PER_STEP):
        c0 = u * MLA_QK_PAD
        rope = a1[:, c0 + MLA_NOPE:c0 + MLA_QK_PAD] * cos + a2[:, u * LANE:(u + 1) * LANE] * sin
        o_ref[0, :, c0:c0 + MLA_NOPE] = (a1[:, c0:c0 + MLA_NOPE] * scale).astype(BF16)
        o_ref[0, :, c0 + MLA_NOPE:c0 + MLA_QK_PAD] = (rope * scale).astype(BF16)


def _mla_q(pm3, g_q, w1, w2, cos128, sin128, *, tm=512):
    b, s, _ = pm3.shape
    hp = MLA_HEADS_PER_STEP
    cq_blk = _PM_OFF["cq"] // MLA_Q_RANK
    return pl.pallas_call(
        functools.partial(_mla_q_kernel, scale=(MLA_NOPE + MLA_ROPE) ** -0.5 * LOG2E),
        grid=(b, s // tm, MLA_HEADS // hp),
        in_specs=[pl.BlockSpec((1, tm, MLA_Q_RANK), lambda bi, i, h: (bi, i, cq_blk)),
                  pl.BlockSpec((1, MLA_Q_RANK), lambda bi, i, h: (0, 0)),
                  pl.BlockSpec((MLA_Q_RANK, hp * MLA_QK_PAD), lambda bi, i, h: (0, h)),
                  pl.BlockSpec((MLA_Q_RANK, hp * LANE), lambda bi, i, h: (0, h)),
                  pl.BlockSpec((1, tm, LANE), lambda bi, i, h: (bi, i, 0)),
                  pl.BlockSpec((1, tm, LANE), lambda bi, i, h: (bi, i, 0))],
        out_specs=pl.BlockSpec((1, tm, hp * MLA_QK_PAD), lambda bi, i, h: (bi, i, h)),
        out_shape=jax.ShapeDtypeStruct((b, s, MLA_HEADS * MLA_QK_PAD), BF16),
        scratch_shapes=[pltpu.VMEM((tm, MLA_Q_RANK), BF16)],
        compiler_params=_cp(3),
        name="mla_q",
    )(pm3, g_q, w1, w2, cos128, sin128)


def _mla_kv_kernel(ckv_ref, g_ref, w_ref, kr_ref, krr_ref, cos_ref, sin_ref, k_ref, v_ref, an_ref, krt_ref):
    @pl.when(pl.program_id(2) == 0)
    def _():
        an_ref[...] = _rms(ckv_ref[0].astype(F32), g_ref[...]).astype(BF16)
        kr = kr_ref[0].astype(F32) * cos_ref[0] + krr_ref[0].astype(F32) * sin_ref[0]
        krt_ref[...] = kr.T.astype(BF16)

    acc = jnp.dot(an_ref[...], w_ref[...], preferred_element_type=F32)
    ones = jnp.ones((acc.shape[0], MLA_V), BF16)
    for u in range(MLA_HEADS_PER_STEP):
        c0 = u * (MLA_NOPE + MLA_V)
        k_ref[0, u, 0, :MLA_NOPE, :] = acc[:, c0:c0 + MLA_NOPE].T.astype(BF16)
        k_ref[0, u, 0, MLA_NOPE:, :] = krt_ref[...]
        v_ref[0, u, 0, :, :MLA_V] = acc[:, c0 + MLA_NOPE:c0 + MLA_NOPE + MLA_V].astype(BF16)
        v_ref[0, u, 0, :, MLA_V:] = ones


def _mla_kv(pm3, g_kv, w_ukv, cos128, sin128):
    b, s, _ = pm3.shape
    tm = ATT_TK
    hp = MLA_HEADS_PER_STEP
    ckv_blk = _PM_OFF["ckv"] // MLA_KV_RANK
    kr_blk = _PM_OFF["kr"] // LANE
    krr_blk = _PM_OFF["krr"] // LANE
    return pl.pallas_call(
        _mla_kv_kernel,
        grid=(b, s // tm, MLA_HEADS // hp),
        in_specs=[pl.BlockSpec((1, tm, MLA_KV_RANK), lambda bi, i, h: (bi, i, ckv_blk)),
                  pl.BlockSpec((1, MLA_KV_RANK), lambda bi, i, h: (0, 0)),
                  pl.BlockSpec((MLA_KV_RANK, hp * (MLA_NOPE + MLA_V)), lambda bi, i, h: (0, h)),
                  pl.BlockSpec((1, tm, LANE), lambda bi, i, h: (bi, i, kr_blk)),
                  pl.BlockSpec((1, tm, LANE), lambda bi, i, h: (bi, i, krr_blk)),
                  pl.BlockSpec((1, tm, LANE), lambda bi, i, h: (bi, i, 0)),
                  pl.BlockSpec((1, tm, LANE), lambda bi, i, h: (bi, i, 0))],
        out_specs=[pl.BlockSpec((1, hp, 1, MLA_QK_PAD, tm), lambda bi, i, h: (bi, h, i, 0, 0)),
                   pl.BlockSpec((1, hp, 1, tm, 2 * MLA_V), lambda bi, i, h: (bi, h, i, 0, 0))],
        out_shape=[jax.ShapeDtypeStruct((b, MLA_HEADS, s // tm, MLA_QK_PAD, tm), BF16),
                   jax.ShapeDtypeStruct((b, MLA_HEADS, s // tm, tm, 2 * MLA_V), BF16)],
        scratch_shapes=[pltpu.VMEM((tm, MLA_KV_RANK), BF16), pltpu.VMEM((LANE, tm), BF16)],
        compiler_params=_cp(3),
        name="mla_kv",
    )(pm3, g_kv, w_ukv, pm3, pm3, cos128, sin128)


def _sm_init(acc_ref, m_ref):
    m_ref[...] = jnp.full(m_ref.shape, NEG, F32)
    acc_ref[...] = jnp.zeros(acc_ref.shape, F32)


def _sm_step(s, v_ext, acc_ref, m_ref):
    m_old = m_ref[...]
    m_new = jnp.maximum(m_old, jnp.max(s, axis=-1, keepdims=True))
    alpha = jnp.exp2(m_old - m_new)
    p = jnp.exp2(s - jnp.tile(m_new, (1, s.shape[1] // LANE)))
    pv = jnp.dot(p.astype(BF16), v_ext, preferred_element_type=F32)
    acc_ref[...] = jnp.tile(alpha, (1, acc_ref.shape[1] // LANE)) * acc_ref[...] + pv
    m_ref[...] = m_new


def _sm_result(acc_ref, d):
    acc = acc_ref[...]
    return acc[:, :d] / acc[:, d:]


def _mla_attn_kernel(q_ref, kt_ref, v_ref, z_ref, o_ref, s_ref, acc_ref, m_ref):
    i = pl.program_id(2)
    t = ATT_TK
    n_chains = ATT_TQ // t
    assert n_chains % 2 == 0
    chains = range(n_chains)
    qpos = lax.broadcasted_iota(jnp.int32, (t, t), 0)
    kpos = lax.broadcasted_iota(jnp.int32, (t, t), 1)

    def scores(u, j):
        return jnp.dot(q_ref[0, u * t:(u + 1) * t, :], kt_ref[0, 0, j], preferred_element_type=F32)

    def update(u, s, j, diagonal):
        if diagonal:
            s = jnp.where(kpos <= qpos, s, NEG)
        _sm_step(s, v_ref[0, 0, j], acc_ref.at[u], m_ref.at[u])

    def consume(u, buf, j, diagonal):
        update(u, s_ref[buf, u], j, diagonal)

    for u in chains:
        _sm_init(acc_ref.at[u], m_ref.at[u])
        s_ref[0, u] = scores(u, 0)

    def block_pair(p, carry):
        j = 2 * p
        for u in chains:
            consume(u, 0, j, False)
            s_ref[1, u] = scores(u, j + 1)
        for u in chains:
            consume(u, 1, j + 1, False)
            s_ref[0, u] = scores(u, j + 2)
        return carry

    first = n_chains * i
    lax.fori_loop(0, first // 2, block_pair, 0)
    for u in chains:
        consume(u, 0, first, u == 0)
        for d in range(1, u + 1):
            update(u, scores(u, first + d), first + d, d == u)
    for u in chains:
        rows = slice(u * t, (u + 1) * t)
        o_ref[0, rows, :] = (_sm_result(acc_ref.at[u], MLA_V) * z_ref[0, rows, :].astype(F32)).astype(o_ref.dtype)


def _mla_attn(q, kt, v_ext, pz3):
    b, s, _ = q.shape
    nk = s // ATT_TK
    return pl.pallas_call(
        _mla_attn_kernel,
        grid=(b, MLA_HEADS, s // ATT_TQ),
        in_specs=[pl.BlockSpec((1, ATT_TQ, MLA_QK_PAD), lambda bi, h, i: (bi, i, h)),
                  pl.BlockSpec((1, 1, nk, MLA_QK_PAD, ATT_TK), lambda bi, h, i: (bi, h, 0, 0, 0)),
                  pl.BlockSpec((1, 1, nk, ATT_TK, 2 * MLA_V), lambda bi, h, i: (bi, h, 0, 0, 0)),
                  pl.BlockSpec((1, ATT_TQ, MLA_V), lambda bi, h, i: (bi, i, h))],
        out_specs=pl.BlockSpec((1, ATT_TQ, MLA_V), lambda bi, h, i: (bi, i, h)),
        out_shape=jax.ShapeDtypeStruct((b, s, MLA_WIDTH), BF16),
        scratch_shapes=[pltpu.VMEM((2, ATT_TQ // ATT_TK, ATT_TK, ATT_TK), F32),
                        pltpu.VMEM((ATT_TQ // ATT_TK, ATT_TK, 2 * MLA_V), F32),
                        pltpu.VMEM((ATT_TQ // ATT_TK, ATT_TK, LANE), F32)],
        compiler_params=_cp(3),
        name="mla_attn",
    )(q, kt, v_ext, pz3)


def _log_sigmoid(x):
    return jnp.minimum(x, 0.0) - jnp.log(1.0 + jnp.exp(-jnp.abs(x)))


def _gla_kernel(q_ref, k_ref, v_ref, gl_ref, z_ref, w2_ref, bg_ref, go_ref, o_ref, st_ref, *, nb):
    @pl.when(pl.program_id(0) == 0)
    def _():
        st_ref[...] = jnp.zeros(st_ref.shape, F32)

    c = GLA_CHUNK
    row = lax.broadcasted_iota(jnp.int32, (c, c), 0)
    col = lax.broadcasted_iota(jnp.int32, (c, c), 1)
    tril = row >= col
    tril_b = jnp.where(tril, 1.0, 0.0).astype(BF16)
    qw = GLA_HEADS * GLA_DK
    for b in range(nb):
        pre = jnp.dot(gl_ref[b], w2_ref[...], preferred_element_type=F32) + bg_ref[...]
        glog = _log_sigmoid(pre) / GLA_GATE_NORM
        g1 = glog.astype(BF16)
        r1 = glog - g1.astype(F32)
        g2 = r1.astype(BF16)
        g3 = (r1 - g2.astype(F32)).astype(BF16)
        cs = jnp.dot(tril_b, jnp.concatenate([g1, g2, g3], axis=1), preferred_element_type=F32)
        bc_all = cs[:, :qw] + cs[:, qw:2 * qw] + cs[:, 2 * qw:]
        for h in range(GLA_HEADS):
            ks = slice(h * GLA_DK, (h + 1) * GLA_DK)
            vs = slice(h * GLA_DV, (h + 1) * GLA_DV)
            q = q_ref[b, :, ks].astype(F32)
            k = k_ref[b, :, ks].astype(F32)
            v = v_ref[b, :, vs]
            bc = bc_all[:, ks]
            b_last = bc[c - 1:c, :]
            q_dec = (q * GLA_DK ** -0.5 * jnp.exp(bc)).astype(BF16)
            k_inv = (k * jnp.exp(-bc)).astype(BF16)
            k_end = (k * jnp.exp(b_last - bc)).astype(BF16)
            decay = jnp.exp(b_last)
            attn = jnp.where(tril, _nt(q_dec, k_inv), 0.0).astype(BF16)
            st = st_ref[b * GLA_HEADS + h]
            o = jnp.dot(jnp.concatenate([q_dec, attn], axis=1),
                        jnp.concatenate([st.astype(BF16), v], axis=0), preferred_element_type=F32)
            upd = lax.dot_general(k_end, v, (((0,), (0,)), ((), ())), preferred_element_type=F32)
            decay_col = jnp.tile(jnp.broadcast_to(decay, (GLA_DK, GLA_DK)).T, (1, GLA_DV // GLA_DK))
            st_ref[b * GLA_HEADS + h] = st * decay_col + upd
            on = _rms(o, go_ref[...])
            o_ref[b, :, vs] = (on * z_ref[b, :, vs].astype(F32)).astype(o_ref.dtype)


def _gla(pm3, pz3, w2p, bg, gout):
    b, s, _ = pm3.shape
    c = GLA_CHUNK
    qw = GLA_HEADS * GLA_DK
    q_blk, k_blk = _PM_OFF["gq"] // qw, _PM_OFF["gk"] // qw
    v_blk, gl_blk = _PM_OFF["gv"] // GLA_WIDTH, _PM_OFF["glow"] // LANE
    z_blk = MLA_WIDTH // GLA_WIDTH
    return pl.pallas_call(
        functools.partial(_gla_kernel, nb=b),
        grid=(s // c,),
        in_specs=[pl.BlockSpec((b, c, qw), lambda i: (0, i, q_blk)),
                  pl.BlockSpec((b, c, qw), lambda i: (0, i, k_blk)),
                  pl.BlockSpec((b, c, GLA_WIDTH), lambda i: (0, i, v_blk)),
                  pl.BlockSpec((b, c, LANE), lambda i: (0, i, gl_blk)),
                  pl.BlockSpec((b, c, GLA_WIDTH), lambda i: (0, i, z_blk)),
                  pl.BlockSpec((LANE, qw), lambda i: (0, 0)),
                  pl.BlockSpec((1, qw), lambda i: (0, 0)),
                  pl.BlockSpec((1, GLA_DV), lambda i: (0, 0))],
        out_specs=pl.BlockSpec((b, c, GLA_WIDTH), lambda i: (0, i, 0)),
        out_shape=jax.ShapeDtypeStruct((b, s, GLA_WIDTH), BF16),
        scratch_shapes=[pltpu.VMEM((b * GLA_HEADS, GLA_DK, GLA_DV), F32)],
        compiler_params=_cp(1),
        name="gla",
    )(pm3, pm3, pm3, pm3, pz3, w2p, bg, gout)


def _dsa_kernel(iq_ref, iw_ref, onehot_ref, ikt_ref, q_ref, kt_ref, v_ref, z_ref, o_ref,
                key_ref, keyt_ref, iqs_ref, wb_ref, qs_ref, s_ref, acc_ref, m_ref, *, top_k):
    tq, tk = DSA_TQ, ATT_TK
    nk = key_ref.shape[0]
    lanes = tk // LANE
    i = pl.program_id(1)
    nkb = (i * tq + tq - 1) // tk + 1
    qpos = i * tq + lax.broadcasted_iota(jnp.int32, (tq, tk), 0)
    heads_per_dot = 4

    for h in range(IDX_HEADS):
        iqs_ref[h] = iq_ref[0, :, h * IDX_DH:(h + 1) * IDX_DH]

    wb = jnp.dot(iw_ref[0], onehot_ref[...], preferred_element_type=F32)
    for h in range(IDX_HEADS):
        wb_ref[h] = wb[:, h * LANE:(h + 1) * LANE] * (IDX_HEADS ** -0.5 * IDX_DH ** -0.5)

    def sortable(x):
        bits = lax.bitcast_convert_type(x, jnp.int32)
        return bits ^ ((bits >> 31) & 0x7FFFFFFF)

    qpos_t = i * tq + lax.broadcasted_iota(jnp.int32, (tk, tq), 1)

    def score_block(j, carry):
        ikt = ikt_ref[0, j]
        sc = jnp.zeros((tq, tk), F32)
        for hg in range(IDX_HEADS // heads_per_dot):
            iq = iqs_ref[hg * heads_per_dot:(hg + 1) * heads_per_dot].reshape(heads_per_dot * tq, IDX_DH)
            r = jnp.dot(iq, ikt, preferred_element_type=F32)
            for hh in range(heads_per_dot):
                wrow = jnp.tile(wb_ref[hg * heads_per_dot + hh], (1, lanes))
                sc = sc + jnp.maximum(r[hh * tq:(hh + 1) * tq, :], 0.0) * wrow
        kpos = j * tk + lax.broadcasted_iota(jnp.int32, (tq, tk), 1)
        key_ref[j] = jnp.where(kpos <= qpos, sortable(sc), INT_MIN)
        kpos_t = j * tk + lax.broadcasted_iota(jnp.int32, (tk, tq), 0)
        keyt_ref[j] = jnp.where(kpos_t <= qpos_t, sortable(sc.T), INT_MIN)
        return carry

    lax.fori_loop(0, nkb, score_block, 0)

    def count(pred):
        def body(j, acc):
            hit = jnp.where(pred(keyt_ref[j], j), 1, 0).astype(jnp.int32)
            return acc + hit.reshape(tk // 8, 8, tq).sum(axis=0)
        acc = lax.fori_loop(0, nkb, body, jnp.zeros((8, tq), jnp.int32))
        return acc.sum(axis=0, keepdims=True)

    def count_ge(cand):
        return count(lambda k, j: k >= cand)

    zero = jnp.zeros((1, tq), jnp.int32)
    c0 = count_ge(zero)
    thr = jnp.where(c0 >= top_k, zero, INT_MIN)
    n_ge = jnp.where(c0 >= top_k, c0, nkb * tk)

    def bit_step(it, carry):
        thr, n_ge = carry
        cand = thr | lax.shift_left(jnp.int32(1), 30 - it)
        c = count_ge(cand)
        return jnp.where(c >= top_k, cand, thr), jnp.where(c >= top_k, c, n_ge)

    thr, n_ge = lax.fori_loop(0, 31, bit_step, (thr, n_ge))

    n_gt = count(lambda k, j: k > thr)
    n_take = top_k - n_gt
    overflow = (thr > INT_MIN) & (n_ge - n_gt > n_take)

    @pl.when(jnp.max(jnp.where(overflow, 1, 0)) > 0)
    def _():
        rows = lax.broadcasted_iota(jnp.int32, (tk, tq), 0)
        bound = jnp.zeros((1, tq), jnp.int32)
        for bit in range((nk * tk - 1).bit_length() - 1, -1, -1):
            cand = bound | (1 << bit)
            below = count(lambda k, j: (k == thr) & (j * tk + rows < cand))
            bound = jnp.where(below < n_take, cand, bound)
        bound = jnp.where(overflow, bound, nk * tk)
        thr_c = jnp.tile(jnp.broadcast_to(thr, (tq, tq)).T, (1, lanes))
        bound_c = jnp.tile(jnp.broadcast_to(bound, (tq, tq)).T, (1, lanes))

        def demote(j, carry):
            k = key_ref[j]
            kpos = j * tk + lax.broadcasted_iota(jnp.int32, (tq, tk), 1)
            key_ref[j] = jnp.where((k == thr_c) & (kpos > bound_c), INT_MIN, k)
            return carry

        lax.fori_loop(0, nkb, demote, 0)

    thr = jnp.maximum(thr, INT_MIN + 1)
    thr_w = jnp.tile(jnp.broadcast_to(thr, (tq, tq)).T, (1, lanes))

    scale = DSA_DH ** -0.5 * LOG2E
    groups = range(DSA_KV_HEADS)

    def scores(g, j):
        return jnp.dot(qs_ref[g], kt_ref[0, g, j], preferred_element_type=F32)

    def consume(g, buf, j):
        sel = key_ref[j] >= thr_w
        s = s_ref[buf, g]
        s = jnp.concatenate([jnp.where(sel, s[r * tq:(r + 1) * tq, :], NEG) for r in range(DSA_REP)], axis=0)
        _sm_step(s, v_ref[0, g, j], acc_ref.at[g], m_ref.at[g])

    for g in groups:
        for r in range(DSA_REP):
            cs = slice((g * DSA_REP + r) * DSA_DH, (g * DSA_REP + r + 1) * DSA_DH)
            qs_ref[g, r * tq:(r + 1) * tq, :] = (q_ref[0, :, cs].astype(F32) * scale).astype(BF16)
        _sm_init(acc_ref.at[g], m_ref.at[g])
    for g in groups:
        s_ref[0, g] = scores(g, 0)

    def block_pair(p, carry):
        j = 2 * p
        for g in groups:
            consume(g, 0, j)
            s_ref[1, g] = scores(g, j + 1)
        for g in groups:
            consume(g, 1, j + 1)
            s_ref[0, g] = scores(g, jnp.minimum(j + 2, nk - 1))
        return carry

    lax.fori_loop(0, nkb // 2, block_pair, 0)

    @pl.when(nkb % 2 == 1)
    def _():
        for g in groups:
            consume(g, 0, nkb - 1)

    for g in groups:
        res = _sm_result(acc_ref.at[g], DSA_DH)
        for r in range(DSA_REP):
            cs = slice((g * DSA_REP + r) * DSA_DH, (g * DSA_REP + r + 1) * DSA_DH)
            o_ref[0, :, cs] = (res[r * tq:(r + 1) * tq, :] * z_ref[0, :, cs].astype(F32)).astype(o_ref.dtype)


def _dsa(pm3, pz3, ikt, dkt, dv_ext, top_k):
    b, s, _ = pm3.shape
    tq, tk = DSA_TQ, ATT_TK
    nk = s // tk
    q_blk = _PM_OFF["dq"] // DSA_WIDTH
    z_blk = (MLA_WIDTH + GLA_WIDTH) // DSA_WIDTH
    rows = DSA_REP * tq
    iq_blk = _PM_OFF["iq"] // (IDX_HEADS * IDX_DH)
    assert tq == LANE
    iw_blk = _PM_OFF["iw"] // LANE
    onehot = jnp.repeat(jnp.eye(LANE, IDX_HEADS, dtype=BF16), LANE, axis=1)
    return pl.pallas_call(
        functools.partial(_dsa_kernel, top_k=top_k),
        grid=(b, s // tq),
        in_specs=[pl.BlockSpec((1, tq, IDX_HEADS * IDX_DH), lambda bi, i: (bi, i, iq_blk)),
                  pl.BlockSpec((1, tq, LANE), lambda bi, i: (bi, i, iw_blk)),
                  pl.BlockSpec((LANE, IDX_HEADS * LANE), lambda bi, i: (0, 0)),
                  pl.BlockSpec((1, nk, IDX_DH, tk), lambda bi, i: (bi, 0, 0, 0)),
                  pl.BlockSpec((1, tq, DSA_WIDTH), lambda bi, i: (bi, i, q_blk)),
                  pl.BlockSpec((1, DSA_KV_HEADS, nk, DSA_DH, tk), lambda bi, i: (bi, 0, 0, 0, 0)),
                  pl.BlockSpec((1, DSA_KV_HEADS, nk, tk, 2 * DSA_DH), lambda bi, i: (bi, 0, 0, 0, 0)),
                  pl.BlockSpec((1, tq, DSA_WIDTH), lambda bi, i: (bi, i, z_blk))],
        out_specs=pl.BlockSpec((1, tq, DSA_WIDTH), lambda bi, i: (bi, i, 0)),
        out_shape=jax.ShapeDtypeStruct((b, s, DSA_WIDTH), BF16),
        scratch_shapes=[pltpu.VMEM((nk, tq, tk), jnp.int32),
                        pltpu.VMEM((nk, tk, tq), jnp.int32),
                        pltpu.VMEM((IDX_HEADS, tq, IDX_DH), BF16),
                        pltpu.VMEM((IDX_HEADS, tq, LANE), F32),
                        pltpu.VMEM((DSA_KV_HEADS, rows, DSA_DH), BF16),
                        pltpu.VMEM((2, DSA_KV_HEADS, rows, tk), F32),
                        pltpu.VMEM((DSA_KV_HEADS, rows, 2 * DSA_DH), F32),
                        pltpu.VMEM((DSA_KV_HEADS, rows, LANE), F32)],
        compiler_params=_cp(2),
        name="dsa",
    )(pm3, pm3, onehot, ikt, pm3, dkt, dv_ext, pz3)


def _dsa_layout_kernel(ik_ref, dk_ref, dv_ref, ikt_ref, kt_ref, vx_ref):
    ikt_ref[0, 0] = ik_ref[0].astype(F32).T[:IDX_DH, :].astype(BF16)
    ones = jnp.ones((ATT_TK, DSA_DH), BF16)
    for g in range(DSA_KV_HEADS):
        cs = slice(g * DSA_DH, (g + 1) * DSA_DH)
        kt_ref[0, g, 0] = dk_ref[0, :, cs].astype(F32).T.astype(BF16)
        vx_ref[0, g, 0, :, :DSA_DH] = dv_ref[0, :, cs]
        vx_ref[0, g, 0, :, DSA_DH:] = ones


def _dsa_layout(pm3):
    b, s, _ = pm3.shape
    tk = ATT_TK
    nk = s // tk
    kvw = DSA_KV_HEADS * DSA_DH
    ik_blk, dk_blk, dv_blk = _PM_OFF["ik"] // LANE, _PM_OFF["dk"] // kvw, _PM_OFF["dv"] // kvw
    return pl.pallas_call(
        _dsa_layout_kernel,
        grid=(b, nk),
        in_specs=[pl.BlockSpec((1, tk, LANE), lambda bi, j: (bi, j, ik_blk)),
                  pl.BlockSpec((1, tk, kvw), lambda bi, j: (bi, j, dk_blk)),
                  pl.BlockSpec((1, tk, kvw), lambda bi, j: (bi, j, dv_blk))],
        out_specs=[pl.BlockSpec((1, 1, IDX_DH, tk), lambda bi, j: (bi, j, 0, 0)),
                   pl.BlockSpec((1, DSA_KV_HEADS, 1, DSA_DH, tk), lambda bi, j: (bi, 0, j, 0, 0)),
                   pl.BlockSpec((1, DSA_KV_HEADS, 1, tk, 2 * DSA_DH), lambda bi, j: (bi, 0, j, 0, 0))],
        out_shape=[jax.ShapeDtypeStruct((b, nk, IDX_DH, tk), BF16),
                   jax.ShapeDtypeStruct((b, DSA_KV_HEADS, nk, DSA_DH, tk), BF16),
                   jax.ShapeDtypeStruct((b, DSA_KV_HEADS, nk, tk, 2 * DSA_DH), BF16)],
        compiler_params=_cp(2),
        name="dsa_layout",
    )(pm3, pm3, pm3)


def _lift_kernel(ym_ref, yg_ref, yd_ref, wm_ref, wg_ref, wd_ref, ga_ref, gb_ref, gc_ref, o_ref):
    a = jnp.dot(ym_ref[...], wm_ref[...], preferred_element_type=F32)
    b = jnp.dot(yg_ref[...], wg_ref[...], preferred_element_type=F32)
    c = jnp.dot(yd_ref[...], wd_ref[...], preferred_element_type=F32)
    o = ga_ref[...].astype(F32) * a + gb_ref[...].astype(F32) * b + gc_ref[...].astype(F32) * c
    o_ref[...] = o.astype(o_ref.dtype)


def _lift(y_mla, y_gla, y_dsa, w_bm, w_bg, w_bd, gates, *, tm=512, tn=1024):
    t = y_mla.shape[0]
    d = w_bm.shape[1]
    nj = d // tn
    row = lambda width: pl.BlockSpec((tm, width), lambda i, j: (i, 0))
    wcol = lambda depth: pl.BlockSpec((depth, tn), lambda i, j: (0, j))
    gate = lambda br: pl.BlockSpec((tm, tn), lambda i, j: (i, br * nj + j))
    return pl.pallas_call(
        _lift_kernel,
        grid=(t // tm, nj),
        in_specs=[row(MLA_WIDTH), row(GLA_WIDTH), row(DSA_WIDTH),
                  wcol(MLA_WIDTH), wcol(GLA_WIDTH), wcol(DSA_WIDTH),
                  gate(0), gate(1), gate(2)],
        out_specs=pl.BlockSpec((tm, tn), lambda i, j: (i, j)),
        out_shape=jax.ShapeDtypeStruct((t, d), BF16),
        compiler_params=_cp(2),
        name="lift_merge",
    )(y_mla, y_gla, y_dsa, w_bm, w_bg, w_bd, gates, gates, gates)


def _out_kernel(m_ref, w_ref, x_ref, gate_ref, o_ref):
    r = jnp.dot(m_ref[0], w_ref[...], preferred_element_type=F32)
    o_ref[0] = x_ref[0] + gate_ref[0] * r


def _out_proj(merged3, w_o, x, gate, *, tm=1024, tn=1024):
    b, s, d = x.shape
    tm = min(tm, s)
    return pl.pallas_call(
        _out_kernel,
        grid=(b, s // tm, d // tn),
        in_specs=[pl.BlockSpec((1, tm, d), lambda bi, i, j: (bi, i, 0)),
                  pl.BlockSpec((d, tn), lambda bi, i, j: (0, j)),
                  pl.BlockSpec((1, tm, tn), lambda bi, i, j: (bi, i, j)),
                  pl.BlockSpec((1, 1, tn), lambda bi, i, j: (bi, 0, j))],
        out_specs=pl.BlockSpec((1, tm, tn), lambda bi, i, j: (bi, i, j)),
        out_shape=jax.ShapeDtypeStruct((b, s, d), F32),
        compiler_params=_cp(3),
        name="out_proj",
    )(merged3, w_o, x, gate)


def _win_layout_kernel(w_ref, om_ref, oz_ref):
    half = MLA_ROPE // 2
    for name, width in _PM_LAYOUT:
        dst = _PM_OFF[name]
        if name == "pad":
            om_ref[:, dst:dst + width] = jnp.zeros((om_ref.shape[0], width), BF16)
        elif name in ("kr", "krr", "glow", "ik", "iw"):
            src_name = {"kr": "krope", "krr": "krope"}.get(name, name)
            src, w = _IN_OFF[src_name], _IN_W[src_name]
            om_ref[:, dst:dst + width] = jnp.zeros((om_ref.shape[0], width), BF16)
            if name == "krr":
                om_ref[:, dst:dst + half] = (-w_ref[:, src + half:src + w]).astype(BF16)
                om_ref[:, dst + half:dst + w] = w_ref[:, src:src + half].astype(BF16)
            else:
                om_ref[:, dst:dst + w] = w_ref[:, src:src + w].astype(BF16)
        else:
            src = _IN_OFF[name]
            om_ref[:, dst:dst + width] = w_ref[:, src:src + width].astype(BF16)
    oz_ref[...] = w_ref[:, _IN_OFF["z_mla"]:].astype(BF16)


def _win_layout(w_in, layer, *, tr=128):
    _, d, n = w_in.shape
    nz = n - _IN_OFF["z_mla"]
    return pl.pallas_call(
        _win_layout_kernel,
        grid=(d // tr,),
        in_specs=[pl.BlockSpec((None, tr, n), lambda i: (layer, i, 0))],
        out_specs=[pl.BlockSpec((tr, PM_WIDTH), lambda i: (i, 0)),
                   pl.BlockSpec((tr, nz), lambda i: (i, 0))],
        out_shape=[jax.ShapeDtypeStruct((d, PM_WIDTH), BF16), jax.ShapeDtypeStruct((d, nz), BF16)],
        compiler_params=_cp(1),
        name="win_layout",
    )(w_in)


def _mla_q_weights(w_uq):
    r = w_uq.shape[0]
    w = w_uq.reshape(r, MLA_HEADS, MLA_NOPE + MLA_ROPE)
    nope, rope = w[..., :MLA_NOPE], w[..., MLA_NOPE:]
    half = MLA_ROPE // 2
    zeros = jnp.zeros((r, MLA_HEADS, LANE - MLA_ROPE), w.dtype)
    w1 = jnp.concatenate([nope, rope, zeros], axis=-1).reshape(r, MLA_HEADS * MLA_QK_PAD)
    rot = jnp.concatenate([-rope[..., half:], rope[..., :half]], axis=-1)
    w2 = jnp.concatenate([rot, zeros], axis=-1).reshape(r, MLA_HEADS * LANE)
    return w1.astype(BF16), w2.astype(BF16)


def _layer(layer, x, c_pad, cos128, sin128, norm_g, w_ada, b_ada, w_in, mla_gq, mla_wuq, mla_gkv, mla_wukv,
           gla_wg2, gla_bg, gla_gout, w_mg, b_mg, w_bm, w_bg, w_bd, w_o):
    b, s, d = x.shape
    t = b * s
    mod = _mm(c_pad, w_ada, b_ada[None, :], tm=c_pad.shape[0], tn=512, out_dtype=F32,
              a_act="silu", layer=layer, name="ada")[:b]
    shift, scale, gate = (mod[:, None, k * d:(k + 1) * d] for k in range(3))
    h = _norm_mod(x, norm_g[None, :], scale, shift)
    h2 = h.reshape(t, d)

    mm_tm = min(1024, t)
    w_main, w_z = _win_layout(w_in, layer)
    pm = _mm(h2, w_main, None, tm=mm_tm, tn=1536, out_dtype=BF16, name="proj_main")
    pz = _mm(h2, w_z, None, tm=mm_tm, tn=1024, out_dtype=BF16, act="silu", name="proj_gate_paths")
    gates = _mm(h2, w_mg, b_mg[None, :], tm=mm_tm, tn=512, out_dtype=BF16,
                act="sigmoid", layer=layer, name="merge_gates")
    pm3 = pm.reshape(b, s, PM_WIDTH)
    pz3 = pz.reshape(b, s, d)

    w1, w2 = _mla_q_weights(mla_wuq)
    q = _mla_q(pm3, mla_gq[None, :], w1, w2, cos128, sin128)
    kcat, vt = _mla_kv(pm3, mla_gkv[None, :], mla_wukv.astype(BF16), cos128, sin128)
    y_mla = _mla_attn(q, kcat, vt, pz3)

    w2p = jnp.concatenate([gla_wg2, jnp.zeros((LANE - GLA_GATE_RANK, gla_wg2.shape[1]), gla_wg2.dtype)],
                          axis=0).astype(BF16)
    y_gla = _gla(pm3, pz3, w2p, gla_bg[None, :], gla_gout[None, :])

    ikt, dkt, dv_ext = _dsa_layout(pm3)
    y_dsa = _dsa(pm3, pz3, ikt, dkt, dv_ext, min(IDX_TOPK, s // 4))

    merged = _lift(y_mla.reshape(t, MLA_WIDTH), y_gla.reshape(t, GLA_WIDTH), y_dsa.reshape(t, DSA_WIDTH),
                   w_bm.astype(BF16), w_bg.astype(BF16), w_bd.astype(BF16), gates)
    return _out_proj(merged.reshape(b, s, d), w_o.astype(BF16), x, gate)


def kernel(x, c, positions, norm_g, w_ada, b_ada, w_in, mla_gq, mla_wuq, mla_gkv, mla_wukv, gla_wg2,
           gla_bg, gla_gout, w_mg, b_mg, w_bm, w_bg, w_bd, w_o, final_g):
    b = x.shape[0]
    cos128, sin128 = _rope_tables(positions)
    c_pad = jnp.concatenate([c, jnp.zeros((8 - b, c.shape[1]), c.dtype)], axis=0)
    for l in range(DEPTH):
        x = _layer(l, x, c_pad, cos128, sin128, norm_g[l], w_ada, b_ada[l], w_in, mla_gq[l],
                   mla_wuq[l], mla_gkv[l], mla_wukv[l], gla_wg2[l], gla_bg[l], gla_gout[l],
                   w_mg, b_mg[l], w_bm[l], w_bg[l], w_bd[l], w_o[l])
    return _final_norm(x, final_g[None, :])
```

```python
import functools

import jax
import jax.numpy as jnp
import numpy as np
from jax import lax
from jax.experimental import pallas as pl
from jax.experimental.pallas import tpu as pltpu

D_MODEL = 4096
DEPTH = 2
MLA_HEADS = 16
MLA_Q_RANK = 768
MLA_KV_RANK = 512
MLA_NOPE = 128
MLA_ROPE = 64
MLA_V = 128
ROPE_THETA = 10000.0
GLA_HEADS = 4
GLA_DK = 128
GLA_DV = 256
GLA_GATE_RANK = 16
GLA_GATE_NORM = 16.0
GLA_CHUNK = 64
DSA_HEADS = 8
DSA_KV_HEADS = 2
DSA_DH = 128
IDX_HEADS = 32
IDX_DH = 64
IDX_TOPK = 256
NORM_EPS = 1e-6
NEG = -1e30

MLA_WIDTH = MLA_HEADS * MLA_V
GLA_WIDTH = GLA_HEADS * GLA_DV
DSA_WIDTH = DSA_HEADS * DSA_DH
DSA_REP = DSA_HEADS // DSA_KV_HEADS

IN_SPLITS = (
    MLA_Q_RANK, MLA_KV_RANK, MLA_ROPE,
    GLA_HEADS * GLA_DK, GLA_HEADS * GLA_DK, GLA_WIDTH, GLA_GATE_RANK,
    DSA_WIDTH, DSA_KV_HEADS * DSA_DH, DSA_KV_HEADS * DSA_DH,
    IDX_HEADS * IDX_DH, IDX_DH, IDX_HEADS,
    MLA_WIDTH, GLA_WIDTH, DSA_WIDTH,
)
_IN_NAMES = ("cq", "ckv", "krope", "gq", "gk", "gv", "glow", "dq", "dk", "dv",
             "iq", "ik", "iw", "z_mla", "z_gla", "z_dsa")
_IN_OFF = dict(zip(_IN_NAMES, np.concatenate([[0], np.cumsum(IN_SPLITS)[:-1]]).tolist()))
_IN_W = dict(zip(_IN_NAMES, IN_SPLITS))

LANE = 128
MLA_QK_PAD = 256
ATT_TQ = 2048
ATT_TK = 512
MLA_HEADS_PER_STEP = 4
LOG2E = 1.4426950408889634
DSA_TQ = 128
MM_CHUNK = 256
VMEM_LIMIT = 56 * 1024 * 1024

_PM_LAYOUT = (("dq", 1024), ("gv", 1024), ("iq", 2048), ("gq", 512), ("cq", 768), ("dk", 256),
              ("ckv", 512), ("gk", 512), ("dv", 256), ("kr", 128), ("krr", 128), ("glow", 128),
              ("ik", 128), ("iw", 128), ("pad", 128))
_PM_OFF = {}
_o = 0
for _n, _w in _PM_LAYOUT:
    assert _o % _w == 0
    _PM_OFF[_n] = _o
    _o += _w
PM_WIDTH = _o

BF16 = jnp.bfloat16
F32 = jnp.float32
INT_MIN = -2 ** 31


def _cp(n_axes, flags=None):
    return pltpu.CompilerParams(dimension_semantics=("arbitrary",) * n_axes,
                                vmem_limit_bytes=VMEM_LIMIT, flags=flags)


def _nt(a, b):
    return lax.dot_general(a, b, (((1,), (1,)), ((), ())), preferred_element_type=F32)


def _rms(x, g):
    return x * lax.rsqrt(jnp.mean(x * x, axis=-1, keepdims=True) + NORM_EPS) * g


def _mm_kernel(*refs, a_act, act, has_bias):
    if has_bias:
        a_ref, w_ref, b_ref, o_ref = refs
    else:
        a_ref, w_ref, o_ref = refs
    a = a_ref[...]
    if a_act == "silu":
        a = a.astype(F32)
        a = a * jax.nn.sigmoid(a)
    a = a.astype(BF16)
    for c0 in range(0, o_ref.shape[1], MM_CHUNK):
        cs = slice(c0, c0 + MM_CHUNK)
        r = jnp.dot(a, w_ref[:, cs].astype(BF16), preferred_element_type=F32)
        if has_bias:
            r = r + b_ref[:, cs]
        if act == "sigmoid":
            r = jax.nn.sigmoid(r)
        elif act == "silu":
            r = r * jax.nn.sigmoid(r)
        o_ref[:, cs] = r.astype(o_ref.dtype)


def _mm(a, w, bias, *, tm, tn, out_dtype, a_act=None, act=None, layer=None, name):
    m, k = a.shape
    n = w.shape[-1]
    if layer is None:
        w_spec = pl.BlockSpec((k, tn), lambda i, j: (0, j))
    else:
        w_spec = pl.BlockSpec((None, k, tn), lambda i, j: (layer, 0, j))
    in_specs = [pl.BlockSpec((tm, k), lambda i, j: (i, 0)), w_spec]
    args = [a, w]
    if bias is not None:
        in_specs.append(pl.BlockSpec((1, tn), lambda i, j: (0, j)))
        args.append(bias)
    return pl.pallas_call(
        functools.partial(_mm_kernel, a_act=a_act, act=act, has_bias=bias is not None),
        grid=(m // tm, n // tn),
        in_specs=in_specs,
        out_specs=pl.BlockSpec((tm, tn), lambda i, j: (i, j)),
        out_shape=jax.ShapeDtypeStruct((m, n), out_dtype),
        compiler_params=_cp(2),
        name=name,
    )(*args)


def _norm_mod_kernel(x_ref, g_ref, sc_ref, sh_ref, o_ref):
    y = _rms(x_ref[0], g_ref[...])
    o_ref[0] = (y * (1.0 + sc_ref[0]) + sh_ref[0]).astype(o_ref.dtype)


def _norm_mod(x, g, scale, shift, *, tm=256):
    b, s, d = x.shape
    return pl.pallas_call(
        _norm_mod_kernel,
        grid=(b, s // tm),
        in_specs=[pl.BlockSpec((1, tm, d), lambda bi, i: (bi, i, 0)),
                  pl.BlockSpec((1, d), lambda bi, i: (0, 0)),
                  pl.BlockSpec((1, 1, d), lambda bi, i: (bi, 0, 0)),
                  pl.BlockSpec((1, 1, d), lambda bi, i: (bi, 0, 0))],
        out_specs=pl.BlockSpec((1, tm, d), lambda bi, i: (bi, i, 0)),
        out_shape=jax.ShapeDtypeStruct((b, s, d), BF16),
        compiler_params=_cp(2),
        name="norm_mod",
    )(x, g, scale, shift)


def _final_norm_kernel(x_ref, g_ref, o_ref):
    o_ref[0] = _rms(x_ref[0], g_ref[...])


def _final_norm(x, g, *, tm=256):
    b, s, d = x.shape
    return pl.pallas_call(
        _final_norm_kernel,
        grid=(b, s // tm),
        in_specs=[pl.BlockSpec((1, tm, d), lambda bi, i: (bi, i, 0)),
                  pl.BlockSpec((1, d), lambda bi, i: (0, 0))],
        out_specs=pl.BlockSpec((1, tm, d), lambda bi, i: (bi, i, 0)),
        out_shape=jax.ShapeDtypeStruct((b, s, d), F32),
        compiler_params=_cp(2),
        name="final_norm",
    )(x, g)


def _rope_kernel(pos_ref, inv_ref, cos_ref, sin_ref):
    ang = pos_ref[0] * inv_ref[...]
    live = lax.broadcasted_iota(jnp.int32, ang.shape, 1) < MLA_ROPE
    cos_ref[0] = jnp.where(live, jnp.cos(ang), 0.0)
    sin_ref[0] = jnp.where(live, jnp.sin(ang), 0.0)


def _rope_tables(positions, *, tm=512):
    b, s = positions.shape
    inv = ROPE_THETA ** (-jnp.arange(0, MLA_ROPE, 2, dtype=F32) / MLA_ROPE)
    inv128 = jnp.concatenate([inv, inv, jnp.zeros((LANE - MLA_ROPE,), F32)])[None, :]
    pos128 = jnp.broadcast_to(positions.astype(F32)[:, :, None], (b, s, LANE))
    spec = pl.BlockSpec((1, tm, LANE), lambda bi, i: (bi, i, 0))
    return pl.pallas_call(
        _rope_kernel,
        grid=(b, s // tm),
        in_specs=[spec, pl.BlockSpec((1, LANE), lambda bi, i: (0, 0))],
        out_specs=[spec, spec],
        out_shape=[jax.ShapeDtypeStruct((b, s, LANE), F32)] * 2,
        compiler_params=_cp(2),
        name="rope_tables",
    )(pos128, inv128)


def _mla_q_kernel(cq_ref, g_ref, w1_ref, w2_ref, cos_ref, sin_ref, o_ref, an_ref, *, scale):
    @pl.when(pl.program_id(2) == 0)
    def _():
        an_ref[...] = _rms(cq_ref[0].astype(F32), g_ref[...]).astype(BF16)

    a = an_ref[...]
    a1 = jnp.dot(a, w1_ref[...], preferred_element_type=F32)
    a2 = jnp.dot(a, w2_ref[...], preferred_element_type=F32)
    cos, sin = cos_ref[0], sin_ref[0]
    for u in range(MLA_HEADS_PER_STEP):
        c0 = u * MLA_QK_PAD
        rope = a1[:, c0 + MLA_NOPE:c0 + MLA_QK_PAD] * cos + a2[:, u * LANE:(u + 1) * LANE] * sin
        o_ref[0, :, c0:c0 + MLA_NOPE] = (a1[:, c0:c0 + MLA_NOPE] * scale).astype(BF16)
        o_ref[0, :, c0 + MLA_NOPE:c0 + MLA_QK_PAD] = (rope * scale).astype(BF16)


def _mla_q(pm3, g_q, w1, w2, cos128, sin128, *, tm=512):
    b, s, _ = pm3.shape
    hp = MLA_HEADS_PER_STEP
    cq_blk = _PM_OFF["cq"] // MLA_Q_RANK
    return pl.pallas_call(
        functools.partial(_mla_q_kernel, scale=(MLA_NOPE + MLA_ROPE) ** -0.5 * LOG2E),
        grid=(b, s // tm, MLA_HEADS // hp),
        in_specs=[pl.BlockSpec((1, tm, MLA_Q_RANK), lambda bi, i, h: (bi, i, cq_blk)),
                  pl.BlockSpec((1, MLA_Q_RANK), lambda bi, i, h: (0, 0)),
                  pl.BlockSpec((MLA_Q_RANK, hp * MLA_QK_PAD), lambda bi, i, h: (0, h)),
                  pl.BlockSpec((MLA_Q_RANK, hp * LANE), lambda bi, i, h: (0, h)),
                  pl.BlockSpec((1, tm, LANE), lambda bi, i, h: (bi, i, 0)),
                  pl.BlockSpec((1, tm, LANE), lambda bi, i, h: (bi, i, 0))],
        out_specs=pl.BlockSpec((1, tm, hp * MLA_QK_PAD), lambda bi, i, h: (bi, i, h)),
        out_shape=jax.ShapeDtypeStruct((b, s, MLA_HEADS * MLA_QK_PAD), BF16),
        scratch_shapes=[pltpu.VMEM((tm, MLA_Q_RANK), BF16)],
        compiler_params=_cp(3),
        name="mla_q",
    )(pm3, g_q, w1, w2, cos128, sin128)


def _mla_kv_kernel(ckv_ref, g_ref, w_ref, kr_ref, krr_ref, cos_ref, sin_ref, k_ref, v_ref, an_ref, krt_ref):
    @pl.when(pl.program_id(2) == 0)
    def _():
        an_ref[...] = _rms(ckv_ref[0].astype(F32), g_ref[...]).astype(BF16)
        kr = kr_ref[0].astype(F32) * cos_ref[0] + krr_ref[0].astype(F32) * sin_ref[0]
        krt_ref[...] = kr.T.astype(BF16)

    acc = jnp.dot(an_ref[...], w_ref[...], preferred_element_type=F32)
    ones = jnp.ones((acc.shape[0], MLA_V), BF16)
    for u in range(MLA_HEADS_PER_STEP):
        c0 = u * (MLA_NOPE + MLA_V)
        k_ref[0, u, 0, :MLA_NOPE, :] = acc[:, c0:c0 + MLA_NOPE].T.astype(BF16)
        k_ref[0, u, 0, MLA_NOPE:, :] = krt_ref[...]
        v_ref[0, u, 0, :, :MLA_V] = acc[:, c0 + MLA_NOPE:c0 + MLA_NOPE + MLA_V].astype(BF16)
        v_ref[0, u, 0, :, MLA_V:] = ones


def _mla_kv(pm3, g_kv, w_ukv, cos128, sin128):
    b, s, _ = pm3.shape
    tm = ATT_TK
    hp = MLA_HEADS_PER_STEP
    ckv_blk = _PM_OFF["ckv"] // MLA_KV_RANK
    kr_blk = _PM_OFF["kr"] // LANE
    krr_blk = _PM_OFF["krr"] // LANE
    return pl.pallas_call(
        _mla_kv_kernel,
        grid=(b, s // tm, MLA_HEADS // hp),
        in_specs=[pl.BlockSpec((1, tm, MLA_KV_RANK), lambda bi, i, h: (bi, i, ckv_blk)),
                  pl.BlockSpec((1, MLA_KV_RANK), lambda bi, i, h: (0, 0)),
                  pl.BlockSpec((MLA_KV_RANK, hp * (MLA_NOPE + MLA_V)), lambda bi, i, h: (0, h)),
                  pl.BlockSpec((1, tm, LANE), lambda bi, i, h: (bi, i, kr_blk)),
                  pl.BlockSpec((1, tm, LANE), lambda bi, i, h: (bi, i, krr_blk)),
                  pl.BlockSpec((1, tm, LANE), lambda bi, i, h: (bi, i, 0)),
                  pl.BlockSpec((1, tm, LANE), lambda bi, i, h: (bi, i, 0))],
        out_specs=[pl.BlockSpec((1, hp, 1, MLA_QK_PAD, tm), lambda bi, i, h: (bi, h, i, 0, 0)),
                   pl.BlockSpec((1, hp, 1, tm, 2 * MLA_V), lambda bi, i, h: (bi, h, i, 0, 0))],
        out_shape=[jax.ShapeDtypeStruct((b, MLA_HEADS, s // tm, MLA_QK_PAD, tm), BF16),
                   jax.ShapeDtypeStruct((b, MLA_HEADS, s // tm, tm, 2 * MLA_V), BF16)],
        scratch_shapes=[pltpu.VMEM((tm, MLA_KV_RANK), BF16), pltpu.VMEM((LANE, tm), BF16)],
        compiler_params=_cp(3),
        name="mla_kv",
    )(pm3, g_kv, w_ukv, pm3, pm3, cos128, sin128)


def _sm_init(acc_ref, m_ref):
    m_ref[...] = jnp.full(m_ref.shape, NEG, F32)
    acc_ref[...] = jnp.zeros(acc_ref.shape, F32)


def _sm_step(s, v_ext, acc_ref, m_ref):
    m_old = m_ref[...]
    m_new = jnp.maximum(m_old, jnp.max(s, axis=-1, keepdims=True))
    alpha = jnp.exp2(m_old - m_new)
    p = jnp.exp2(s - jnp.tile(m_new, (1, s.shape[1] // LANE)))
    pv = jnp.dot(p.astype(BF16), v_ext, preferred_element_type=F32)
    acc_ref[...] = jnp.tile(alpha, (1, acc_ref.shape[1] // LANE)) * acc_ref[...] + pv
    m_ref[...] = m_new


def _sm_result(acc_ref, d):
    acc = acc_ref[...]
    return acc[:, :d] / acc[:, d:]


def _mla_attn_kernel(q_ref, kt_ref, v_ref, z_ref, o_ref, s_ref, acc_ref, m_ref):
    i = pl.program_id(2)
    t = ATT_TK
    n_chains = ATT_TQ // t
    assert n_chains % 2 == 0
    chains = range(n_chains)
    qpos = lax.broadcasted_iota(jnp.int32, (t, t), 0)
    kpos = lax.broadcasted_iota(jnp.int32, (t, t), 1)

    def scores(u, j):
        return jnp.dot(q_ref[0, u * t:(u + 1) * t, :], kt_ref[0, 0, j], preferred_element_type=F32)

    def update(u, s, j, diagonal):
        if diagonal:
            s = jnp.where(kpos <= qpos, s, NEG)
        _sm_step(s, v_ref[0, 0, j], acc_ref.at[u], m_ref.at[u])

    def consume(u, buf, j, diagonal):
        update(u, s_ref[buf, u], j, diagonal)

    for u in chains:
        _sm_init(acc_ref.at[u], m_ref.at[u])
        s_ref[0, u] = scores(u, 0)

    def block_pair(p, carry):
        j = 2 * p
        for u in chains:
            consume(u, 0, j, False)
            s_ref[1, u] = scores(u, j + 1)
        for u in chains:
            consume(u, 1, j + 1, False)
            s_ref[0, u] = scores(u, j + 2)
        return carry

    first = n_chains * i
    lax.fori_loop(0, first // 2, block_pair, 0)
    for u in chains:
        consume(u, 0, first, u == 0)
        for d in range(1, u + 1):
            update(u, scores(u, first + d), first + d, d == u)
    for u in chains:
        rows = slice(u * t, (u + 1) * t)
        o_ref[0, rows, :] = (_sm_result(acc_ref.at[u], MLA_V) * z_ref[0, rows, :].astype(F32)).astype(o_ref.dtype)


def _mla_attn(q, kt, v_ext, pz3):
    b, s, _ = q.shape
    nk = s // ATT_TK
    return pl.pallas_call(
        _mla_attn_kernel,
        grid=(b, MLA_HEADS, s // ATT_TQ),
        in_specs=[pl.BlockSpec((1, ATT_TQ, MLA_QK_PAD), lambda bi, h, i: (bi, i, h)),
                  pl.BlockSpec((1, 1, nk, MLA_QK_PAD, ATT_TK), lambda bi, h, i: (bi, h, 0, 0, 0)),
                  pl.BlockSpec((1, 1, nk, ATT_TK, 2 * MLA_V), lambda bi, h, i: (bi, h, 0, 0, 0)),
                  pl.BlockSpec((1, ATT_TQ, MLA_V), lambda bi, h, i: (bi, i, h))],
        out_specs=pl.BlockSpec((1, ATT_TQ, MLA_V), lambda bi, h, i: (bi, i, h)),
        out_shape=jax.ShapeDtypeStruct((b, s, MLA_WIDTH), BF16),
        scratch_shapes=[pltpu.VMEM((2, ATT_TQ // ATT_TK, ATT_TK, ATT_TK), F32),
                        pltpu.VMEM((ATT_TQ // ATT_TK, ATT_TK, 2 * MLA_V), F32),
                        pltpu.VMEM((ATT_TQ // ATT_TK, ATT_TK, LANE), F32)],
        compiler_params=_cp(3),
        name="mla_attn",
    )(q, kt, v_ext, pz3)


def _log_sigmoid(x):
    return jnp.minimum(x, 0.0) - jnp.log(1.0 + jnp.exp(-jnp.abs(x)))


def _gla_kernel(q_ref, k_ref, v_ref, gl_ref, z_ref, w2_ref, bg_ref, go_ref, o_ref, st_ref, *, nb):
    @pl.when(pl.program_id(0) == 0)
    def _():
        st_ref[...] = jnp.zeros(st_ref.shape, F32)

    c = GLA_CHUNK
    row = lax.broadcasted_iota(jnp.int32, (c, c), 0)
    col = lax.broadcasted_iota(jnp.int32, (c, c), 1)
    tril = row >= col
    tril_b = jnp.where(tril, 1.0, 0.0).astype(BF16)
    qw = GLA_HEADS * GLA_DK
    for b in range(nb):
        pre = jnp.dot(gl_ref[b], w2_ref[...], preferred_element_type=F32) + bg_ref[...]
        glog = _log_sigmoid(pre) / GLA_GATE_NORM
        g1 = glog.astype(BF16)
        r1 = glog - g1.astype(F32)
        g2 = r1.astype(BF16)
        g3 = (r1 - g2.astype(F32)).astype(BF16)
        cs = jnp.dot(tril_b, jnp.concatenate([g1, g2, g3], axis=1), preferred_element_type=F32)
        bc_all = cs[:, :qw] + cs[:, qw:2 * qw] + cs[:, 2 * qw:]
        for h in range(GLA_HEADS):
            ks = slice(h * GLA_DK, (h + 1) * GLA_DK)
            vs = slice(h * GLA_DV, (h + 1) * GLA_DV)
            q = q_ref[b, :, ks].astype(F32)
            k = k_ref[b, :, ks].astype(F32)
            v = v_ref[b, :, vs]
            bc = bc_all[:, ks]
            b_last = bc[c - 1:c, :]
            q_dec = (q * GLA_DK ** -0.5 * jnp.exp(bc)).astype(BF16)
            k_inv = (k * jnp.exp(-bc)).astype(BF16)
            k_end = (k * jnp.exp(b_last - bc)).astype(BF16)
            decay = jnp.exp(b_last)
            attn = jnp.where(tril, _nt(q_dec, k_inv), 0.0).astype(BF16)
            st = st_ref[b * GLA_HEADS + h]
            o = jnp.dot(jnp.concatenate([q_dec, attn], axis=1),
                        jnp.concatenate([st.astype(BF16), v], axis=0), preferred_element_type=F32)
            upd = lax.dot_general(k_end, v, (((0,), (0,)), ((), ())), preferred_element_type=F32)
            decay_col = jnp.tile(jnp.broadcast_to(decay, (GLA_DK, GLA_DK)).T, (1, GLA_DV // GLA_DK))
            st_ref[b * GLA_HEADS + h] = st * decay_col + upd
            on = _rms(o, go_ref[...])
            o_ref[b, :, vs] = (on * z_ref[b, :, vs].astype(F32)).astype(o_ref.dtype)


def _gla(pm3, pz3, w2p, bg, gout):
    b, s, _ = pm3.shape
    c = GLA_CHUNK
    qw = GLA_HEADS * GLA_DK
    q_blk, k_blk = _PM_OFF["gq"] // qw, _PM_OFF["gk"] // qw
    v_blk, gl_blk = _PM_OFF["gv"] // GLA_WIDTH, _PM_OFF["glow"] // LANE
    z_blk = MLA_WIDTH // GLA_WIDTH
    return pl.pallas_call(
        functools.partial(_gla_kernel, nb=b),
        grid=(s // c,),
        in_specs=[pl.BlockSpec((b, c, qw), lambda i: (0, i, q_blk)),
                  pl.BlockSpec((b, c, qw), lambda i: (0, i, k_blk)),
                  pl.BlockSpec((b, c, GLA_WIDTH), lambda i: (0, i, v_blk)),
                  pl.BlockSpec((b, c, LANE), lambda i: (0, i, gl_blk)),
                  pl.BlockSpec((b, c, GLA_WIDTH), lambda i: (0, i, z_blk)),
                  pl.BlockSpec((LANE, qw), lambda i: (0, 0)),
                  pl.BlockSpec((1, qw), lambda i: (0, 0)),
                  pl.BlockSpec((1, GLA_DV), lambda i: (0, 0))],
        out_specs=pl.BlockSpec((b, c, GLA_WIDTH), lambda i: (0, i, 0)),
        out_shape=jax.ShapeDtypeStruct((b, s, GLA_WIDTH), BF16),
        scratch_shapes=[pltpu.VMEM((b * GLA_HEADS, GLA_DK, GLA_DV), F32)],
        compiler_params=_cp(1),
        name="gla",
    )(pm3, pm3, pm3, pm3, pz3, w2p, bg, gout)


def _dsa_kernel(iq_ref, iw_ref, onehot_ref, ikt_ref, q_ref, kt_ref, v_ref, z_ref, o_ref,
                key_ref, keyt_ref, iqs_ref, wb_ref, qs_ref, s_ref, acc_ref, m_ref, *, top_k):
    tq, tk = DSA_TQ, ATT_TK
    nk = key_ref.shape[0]
    lanes = tk // LANE
    i = pl.program_id(1)
    nkb = (i * tq + tq - 1) // tk + 1
    qpos = i * tq + lax.broadcasted_iota(jnp.int32, (tq, tk), 0)
    heads_per_dot = 4

    for h in range(IDX_HEADS):
        iqs_ref[h] = iq_ref[0, :, h * IDX_DH:(h + 1) * IDX_DH]

    wb = jnp.dot(iw_ref[0], onehot_ref[...], preferred_element_type=F32)
    for h in range(IDX_HEADS):
        wb_ref[h] = wb[:, h * LANE:(h + 1) * LANE] * (IDX_HEADS ** -0.5 * IDX_DH ** -0.5)

    def sortable(x):
        bits = lax.bitcast_convert_type(x, jnp.int32)
        return bits ^ ((bits >> 31) & 0x7FFFFFFF)

    qpos_t = i * tq + lax.broadcasted_iota(jnp.int32, (tk, tq), 1)

    def score_block(j, carry):
        ikt = ikt_ref[0, j]
        sc = jnp.zeros((tq, tk), F32)
        for hg in range(IDX_HEADS // heads_per_dot):
            iq = iqs_ref[hg * heads_per_dot:(hg + 1) * heads_per_dot].reshape(heads_per_dot * tq, IDX_DH)
            r = jnp.dot(iq, ikt, preferred_element_type=F32)
            for hh in range(heads_per_dot):
                wrow = jnp.tile(wb_ref[hg * heads_per_dot + hh], (1, lanes))
                sc = sc + jnp.maximum(r[hh * tq:(hh + 1) * tq, :], 0.0) * wrow
        kpos = j * tk + lax.broadcasted_iota(jnp.int32, (tq, tk), 1)
        key_ref[j] = jnp.where(kpos <= qpos, sortable(sc), INT_MIN)
        kpos_t = j * tk + lax.broadcasted_iota(jnp.int32, (tk, tq), 0)
        keyt_ref[j] = jnp.where(kpos_t <= qpos_t, sortable(sc.T), INT_MIN)
        return carry

    lax.fori_loop(0, nkb, score_block, 0)

    def count(pred):
        def body(j, acc):
            hit = jnp.where(pred(keyt_ref[j], j), 1, 0).astype(jnp.int32)
            return acc + hit.reshape(tk // 8, 8, tq).sum(axis=0)
        acc = lax.fori_loop(0, nkb, body, jnp.zeros((8, tq), jnp.int32))
        return acc.sum(axis=0, keepdims=True)

    def count_ge(cand):
        return count(lambda k, j: k >= cand)

    zero = jnp.zeros((1, tq), jnp.int32)
    c0 = count_ge(zero)
    thr = jnp.where(c0 >= top_k, zero, INT_MIN)
    n_ge = jnp.where(c0 >= top_k, c0, nkb * tk)

    def bit_step(it, carry):
        thr, n_ge = carry
        cand = thr | lax.shift_left(jnp.int32(1), 30 - it)
        c = count_ge(cand)
        return jnp.where(c >= top_k, cand, thr), jnp.where(c >= top_k, c, n_ge)

    thr, n_ge = lax.fori_loop(0, 31, bit_step, (thr, n_ge))

    overflow = (thr > INT_MIN) & (n_ge > top_k)

    @pl.when(jnp.max(jnp.where(overflow, 1, 0)) > 0)
    def _():
        n_take = top_k - count(lambda k, j: k > thr)
        rows = lax.broadcasted_iota(jnp.int32, (tk, tq), 0)
        bound = jnp.zeros((1, tq), jnp.int32)
        for bit in range((nk * tk - 1).bit_length() - 1, -1, -1):
            cand = bound | (1 << bit)
            below = count(lambda k, j: (k == thr) & (j * tk + rows < cand))
            bound = jnp.where(below < n_take, cand, bound)
        bound = jnp.where(overflow, bound, nk * tk)
        thr_c = jnp.tile(jnp.broadcast_to(thr, (tq, tq)).T, (1, lanes))
        bound_c = jnp.tile(jnp.broadcast_to(bound, (tq, tq)).T, (1, lanes))

        def demote(j, carry):
            k = key_ref[j]
            kpos = j * tk + lax.broadcasted_iota(jnp.int32, (tq, tk), 1)
            key_ref[j] = jnp.where((k == thr_c) & (kpos > bound_c), INT_MIN, k)
            return carry

        lax.fori_loop(0, nkb, demote, 0)

    thr = jnp.maximum(thr, INT_MIN + 1)
    thr_w = jnp.tile(jnp.broadcast_to(thr, (tq, tq)).T, (1, lanes))

    scale = DSA_DH ** -0.5 * LOG2E
    groups = range(DSA_KV_HEADS)

    def scores(g, j):
        return jnp.dot(qs_ref[g], kt_ref[0, g, j], preferred_element_type=F32)

    def consume(g, buf, j):
        sel = key_ref[j] >= thr_w
        s = s_ref[buf, g]
        s = jnp.concatenate([jnp.where(sel, s[r * tq:(r + 1) * tq, :], NEG) for r in range(DSA_REP)], axis=0)
        _sm_step(s, v_ref[0, g, j], acc_ref.at[g], m_ref.at[g])

    for g in groups:
        for r in range(DSA_REP):
            cs = slice((g * DSA_REP + r) * DSA_DH, (g * DSA_REP + r + 1) * DSA_DH)
            qs_ref[g, r * tq:(r + 1) * tq, :] = (q_ref[0, :, cs].astype(F32) * scale).astype(BF16)
        _sm_init(acc_ref.at[g], m_ref.at[g])
    for g in groups:
        s_ref[0, g] = scores(g, 0)

    def block_pair(p, carry):
        j = 2 * p
        for g in groups:
            consume(g, 0, j)
            s_ref[1, g] = scores(g, j + 1)
        for g in groups:
            consume(g, 1, j + 1)
            s_ref[0, g] = scores(g, jnp.minimum(j + 2, nk - 1))
        return carry

    lax.fori_loop(0, nkb // 2, block_pair, 0)

    @pl.when(nkb % 2 == 1)
    def _():
        for g in groups:
            consume(g, 0, nkb - 1)

    for g in groups:
        res = _sm_result(acc_ref.at[g], DSA_DH)
        for r in range(DSA_REP):
            cs = slice((g * DSA_REP + r) * DSA_DH, (g * DSA_REP + r + 1) * DSA_DH)
            o_ref[0, :, cs] = (res[r * tq:(r + 1) * tq, :] * z_ref[0, :, cs].astype(F32)).astype(o_ref.dtype)


def _dsa(pm3, pz3, ikt, dkt, dv_ext, top_k):
    b, s, _ = pm3.shape
    tq, tk = DSA_TQ, ATT_TK
    nk = s // tk
    q_blk = _PM_OFF["dq"] // DSA_WIDTH
    z_blk = (MLA_WIDTH + GLA_WIDTH) // DSA_WIDTH
    rows = DSA_REP * tq
    iq_blk = _PM_OFF["iq"] // (IDX_HEADS * IDX_DH)
    assert tq == LANE
    iw_blk = _PM_OFF["iw"] // LANE
    onehot = jnp.repeat(jnp.eye(LANE, IDX_HEADS, dtype=BF16), LANE, axis=1)
    return pl.pallas_call(
        functools.partial(_dsa_kernel, top_k=top_k),
        grid=(b, s // tq),
        in_specs=[pl.BlockSpec((1, tq, IDX_HEADS * IDX_DH), lambda bi, i: (bi, i, iq_blk)),
                  pl.BlockSpec((1, tq, LANE), lambda bi, i: (bi, i, iw_blk)),
                  pl.BlockSpec((LANE, IDX_HEADS * LANE), lambda bi, i: (0, 0)),
                  pl.BlockSpec((1, nk, IDX_DH, tk), lambda bi, i: (bi, 0, 0, 0)),
                  pl.BlockSpec((1, tq, DSA_WIDTH), lambda bi, i: (bi, i, q_blk)),
                  pl.BlockSpec((1, DSA_KV_HEADS, nk, DSA_DH, tk), lambda bi, i: (bi, 0, 0, 0, 0)),
                  pl.BlockSpec((1, DSA_KV_HEADS, nk, tk, 2 * DSA_DH), lambda bi, i: (bi, 0, 0, 0, 0)),
                  pl.BlockSpec((1, tq, DSA_WIDTH), lambda bi, i: (bi, i, z_blk))],
        out_specs=pl.BlockSpec((1, tq, DSA_WIDTH), lambda bi, i: (bi, i, 0)),
        out_shape=jax.ShapeDtypeStruct((b, s, DSA_WIDTH), BF16),
        scratch_shapes=[pltpu.VMEM((nk, tq, tk), jnp.int32),
                        pltpu.VMEM((nk, tk, tq), jnp.int32),
                        pltpu.VMEM((IDX_HEADS, tq, IDX_DH), BF16),
                        pltpu.VMEM((IDX_HEADS, tq, LANE), F32),
                        pltpu.VMEM((DSA_KV_HEADS, rows, DSA_DH), BF16),
                        pltpu.VMEM((2, DSA_KV_HEADS, rows, tk), F32),
                        pltpu.VMEM((DSA_KV_HEADS, rows, 2 * DSA_DH), F32),
                        pltpu.VMEM((DSA_KV_HEADS, rows, LANE), F32)],
        compiler_params=_cp(2),
        name="dsa",
    )(pm3, pm3, onehot, ikt, pm3, dkt, dv_ext, pz3)


def _dsa_layout_kernel(ik_ref, dk_ref, dv_ref, ikt_ref, kt_ref, vx_ref):
    ikt_ref[0, 0] = ik_ref[0].astype(F32).T[:IDX_DH, :].astype(BF16)
    ones = jnp.ones((ATT_TK, DSA_DH), BF16)
    for g in range(DSA_KV_HEADS):
        cs = slice(g * DSA_DH, (g + 1) * DSA_DH)
        kt_ref[0, g, 0] = dk_ref[0, :, cs].astype(F32).T.astype(BF16)
        vx_ref[0, g, 0, :, :DSA_DH] = dv_ref[0, :, cs]
        vx_ref[0, g, 0, :, DSA_DH:] = ones


def _dsa_layout(pm3):
    b, s, _ = pm3.shape
    tk = ATT_TK
    nk = s // tk
    kvw = DSA_KV_HEADS * DSA_DH
    ik_blk, dk_blk, dv_blk = _PM_OFF["ik"] // LANE, _PM_OFF["dk"] // kvw, _PM_OFF["dv"] // kvw
    return pl.pallas_call(
        _dsa_layout_kernel,
        grid=(b, nk),
        in_specs=[pl.BlockSpec((1, tk, LANE), lambda bi, j: (bi, j, ik_blk)),
                  pl.BlockSpec((1, tk, kvw), lambda bi, j: (bi, j, dk_blk)),
                  pl.BlockSpec((1, tk, kvw), lambda bi, j: (bi, j, dv_blk))],
        out_specs=[pl.BlockSpec((1, 1, IDX_DH, tk), lambda bi, j: (bi, j, 0, 0)),
                   pl.BlockSpec((1, DSA_KV_HEADS, 1, DSA_DH, tk), lambda bi, j: (bi, 0, j, 0, 0)),
                   pl.BlockSpec((1, DSA_KV_HEADS, 1, tk, 2 * DSA_DH), lambda bi, j: (bi, 0, j, 0, 0))],
        out_shape=[jax.ShapeDtypeStruct((b, nk, IDX_DH, tk), BF16),
                   jax.ShapeDtypeStruct((b, DSA_KV_HEADS, nk, DSA_DH, tk), BF16),
                   jax.ShapeDtypeStruct((b, DSA_KV_HEADS, nk, tk, 2 * DSA_DH), BF16)],
        compiler_params=_cp(2),
        name="dsa_layout",
    )(pm3, pm3, pm3)


def _lift_kernel(ym_ref, yg_ref, yd_ref, wm_ref, wg_ref, wd_ref, ga_ref, gb_ref, gc_ref, o_ref):
    a = jnp.dot(ym_ref[...], wm_ref[...], preferred_element_type=F32)
    b = jnp.dot(yg_ref[...], wg_ref[...], preferred_element_type=F32)
    c = jnp.dot(yd_ref[...], wd_ref[...], preferred_element_type=F32)
    o = ga_ref[...].astype(F32) * a + gb_ref[...].astype(F32) * b + gc_ref[...].astype(F32) * c
    o_ref[...] = o.astype(o_ref.dtype)


def _lift(y_mla, y_gla, y_dsa, w_bm, w_bg, w_bd, gates, *, tm=512, tn=1024):
    t = y_mla.shape[0]
    d = w_bm.shape[1]
    nj = d // tn
    row = lambda width: pl.BlockSpec((tm, width), lambda i, j: (i, 0))
    wcol = lambda depth: pl.BlockSpec((depth, tn), lambda i, j: (0, j))
    gate = lambda br: pl.BlockSpec((tm, tn), lambda i, j: (i, br * nj + j))
    return pl.pallas_call(
        _lift_kernel,
        grid=(t // tm, nj),
        in_specs=[row(MLA_WIDTH), row(GLA_WIDTH), row(DSA_WIDTH),
                  wcol(MLA_WIDTH), wcol(GLA_WIDTH), wcol(DSA_WIDTH),
                  gate(0), gate(1), gate(2)],
        out_specs=pl.BlockSpec((tm, tn), lambda i, j: (i, j)),
        out_shape=jax.ShapeDtypeStruct((t, d), BF16),
        compiler_params=_cp(2),
        name="lift_merge",
    )(y_mla, y_gla, y_dsa, w_bm, w_bg, w_bd, gates, gates, gates)


def _out_kernel(m_ref, w_ref, x_ref, gate_ref, o_ref):
    r = jnp.dot(m_ref[0], w_ref[...], preferred_element_type=F32)
    o_ref[0] = x_ref[0] + gate_ref[0] * r


def _out_proj(merged3, w_o, x, gate, *, tm=1024, tn=1024):
    b, s, d = x.shape
    tm = min(tm, s)
    return pl.pallas_call(
        _out_kernel,
        grid=(b, s // tm, d // tn),
        in_specs=[pl.BlockSpec((1, tm, d), lambda bi, i, j: (bi, i, 0)),
                  pl.BlockSpec((d, tn), lambda bi, i, j: (0, j)),
                  pl.BlockSpec((1, tm, tn), lambda bi, i, j: (bi, i, j)),
                  pl.BlockSpec((1, 1, tn), lambda bi, i, j: (bi, 0, j))],
        out_specs=pl.BlockSpec((1, tm, tn), lambda bi, i, j: (bi, i, j)),
        out_shape=jax.ShapeDtypeStruct((b, s, d), F32),
        compiler_params=_cp(3),
        name="out_proj",
    )(merged3, w_o, x, gate)


def _win_layout_kernel(w_ref, om_ref, oz_ref):
    half = MLA_ROPE // 2
    for name, width in _PM_LAYOUT:
        dst = _PM_OFF[name]
        if name == "pad":
            om_ref[:, dst:dst + width] = jnp.zeros((om_ref.shape[0], width), BF16)
        elif name in ("kr", "krr", "glow", "ik", "iw"):
            src_name = {"kr": "krope", "krr": "krope"}.get(name, name)
            src, w = _IN_OFF[src_name], _IN_W[src_name]
            om_ref[:, dst:dst + width] = jnp.zeros((om_ref.shape[0], width), BF16)
            if name == "krr":
                om_ref[:, dst:dst + half] = (-w_ref[:, src + half:src + w]).astype(BF16)
                om_ref[:, dst + half:dst + w] = w_ref[:, src:src + half].astype(BF16)
            else:
                om_ref[:, dst:dst + w] = w_ref[:, src:src + w].astype(BF16)
        else:
            src = _IN_OFF[name]
            om_ref[:, dst:dst + width] = w_ref[:, src:src + width].astype(BF16)
    oz_ref[...] = w_ref[:, _IN_OFF["z_mla"]:].astype(BF16)


def _win_layout(w_in, layer, *, tr=128):
    _, d, n = w_in.shape
    nz = n - _IN_OFF["z_mla"]
    return pl.pallas_call(
        _win_layout_kernel,
        grid=(d // tr,),
        in_specs=[pl.BlockSpec((None, tr, n), lambda i: (layer, i, 0))],
        out_specs=[pl.BlockSpec((tr, PM_WIDTH), lambda i: (i, 0)),
                   pl.BlockSpec((tr, nz), lambda i: (i, 0))],
        out_shape=[jax.ShapeDtypeStruct((d, PM_WIDTH), BF16), jax.ShapeDtypeStruct((d, nz), BF16)],
        compiler_params=_cp(1),
        name="win_layout",
    )(w_in)


def _mla_q_weights(w_uq):
    r = w_uq.shape[0]
    w = w_uq.reshape(r, MLA_HEADS, MLA_NOPE + MLA_ROPE)
    nope, rope = w[..., :MLA_NOPE], w[..., MLA_NOPE:]
    half = MLA_ROPE // 2
    zeros = jnp.zeros((r, MLA_HEADS, LANE - MLA_ROPE), w.dtype)
    w1 = jnp.concatenate([nope, rope, zeros], axis=-1).reshape(r, MLA_HEADS * MLA_QK_PAD)
    rot = jnp.concatenate([-rope[..., half:], rope[..., :half]], axis=-1)
    w2 = jnp.concatenate([rot, zeros], axis=-1).reshape(r, MLA_HEADS * LANE)
    return w1.astype(BF16), w2.astype(BF16)


def _layer(layer, x, c_pad, cos128, sin128, norm_g, w_ada, b_ada, w_in, mla_gq, mla_wuq, mla_gkv, mla_wukv,
           gla_wg2, gla_bg, gla_gout, w_mg, b_mg, w_bm, w_bg, w_bd, w_o):
    b, s, d = x.shape
    t = b * s
    mod = _mm(c_pad, w_ada, b_ada[None, :], tm=c_pad.shape[0], tn=512, out_dtype=F32,
              a_act="silu", layer=layer, name="ada")[:b]
    shift, scale, gate = (mod[:, None, k * d:(k + 1) * d] for k in range(3))
    h = _norm_mod(x, norm_g[None, :], scale, shift)
    h2 = h.reshape(t, d)

    mm_tm = min(1024, t)
    w_main, w_z = _win_layout(w_in, layer)
    pm = _mm(h2, w_main, None, tm=mm_tm, tn=1536, out_dtype=BF16, name="proj_main")
    pz = _mm(h2, w_z, None, tm=mm_tm, tn=1024, out_dtype=BF16, act="silu", name="proj_gate_paths")
    gates = _mm(h2, w_mg, b_mg[None, :], tm=mm_tm, tn=512, out_dtype=BF16,
                act="sigmoid", layer=layer, name="merge_gates")
    pm3 = pm.reshape(b, s, PM_WIDTH)
    pz3 = pz.reshape(b, s, d)

    w1, w2 = _mla_q_weights(mla_wuq)
    q = _mla_q(pm3, mla_gq[None, :], w1, w2, cos128, sin128)
    kcat, vt = _mla_kv(pm3, mla_gkv[None, :], mla_wukv.astype(BF16), cos128, sin128)
    y_mla = _mla_attn(q, kcat, vt, pz3)

    w2p = jnp.concatenate([gla_wg2, jnp.zeros((LANE - GLA_GATE_RANK, gla_wg2.shape[1]), gla_wg2.dtype)],
                          axis=0).astype(BF16)
    y_gla = _gla(pm3, pz3, w2p, gla_bg[None, :], gla_gout[None, :])

    ikt, dkt, dv_ext = _dsa_layout(pm3)
    y_dsa = _dsa(pm3, pz3, ikt, dkt, dv_ext, min(IDX_TOPK, s // 4))

    merged = _lift(y_mla.reshape(t, MLA_WIDTH), y_gla.reshape(t, GLA_WIDTH), y_dsa.reshape(t, DSA_WIDTH),
                   w_bm.astype(BF16), w_bg.astype(BF16), w_bd.astype(BF16), gates)
    return _out_proj(merged.reshape(b, s, d), w_o.astype(BF16), x, gate)


def kernel(x, c, positions, norm_g, w_ada, b_ada, w_in, mla_gq, mla_wuq, mla_gkv, mla_wukv, gla_wg2,
           gla_bg, gla_gout, w_mg, b_mg, w_bm, w_bg, w_bd, w_o, final_g):
    b = x.shape[0]
    cos128, sin128 = _rope_tables(positions)
    c_pad = jnp.concatenate([c, jnp.zeros((8 - b, c.shape[1]), c.dtype)], axis=0)
    for l in range(DEPTH):
        x = _layer(l, x, c_pad, cos128, sin128, norm_g[l], w_ada, b_ada[l], w_in, mla_gq[l],
                   mla_wuq[l], mla_gkv[l], mla_wukv[l], gla_wg2[l], gla_bg[l], gla_gout[l],
                   w_mg, b_mg[l], w_bm[l], w_bg[l], w_bd[l], w_o[l])
    return _final_norm(x, final_g[None, :])
```

```python
import functools

import jax
import jax.numpy as jnp
import numpy as np
from jax import lax
from jax.experimental import pallas as pl
from jax.experimental.pallas import tpu as pltpu

D_MODEL = 4096
DEPTH = 2
MLA_HEADS = 16
MLA_Q_RANK = 768
MLA_KV_RANK = 512
MLA_NOPE = 128
MLA_ROPE = 64
MLA_V = 128
ROPE_THETA = 10000.0
GLA_HEADS = 4
GLA_DK = 128
GLA_DV = 256
GLA_GATE_RANK = 16
GLA_GATE_NORM = 16.0
GLA_CHUNK = 64
DSA_HEADS = 8
DSA_KV_HEADS = 2
DSA_DH = 128
IDX_HEADS = 32
IDX_DH = 64
IDX_TOPK = 256
NORM_EPS = 1e-6
NEG = -1e30

MLA_WIDTH = MLA_HEADS * MLA_V
GLA_WIDTH = GLA_HEADS * GLA_DV
DSA_WIDTH = DSA_HEADS * DSA_DH
DSA_REP = DSA_HEADS // DSA_KV_HEADS

IN_SPLITS = (
    MLA_Q_RANK, MLA_KV_RANK, MLA_ROPE,
    GLA_HEADS * GLA_DK, GLA_HEADS * GLA_DK, GLA_WIDTH, GLA_GATE_RANK,
    DSA_WIDTH, DSA_KV_HEADS * DSA_DH, DSA_KV_HEADS * DSA_DH,
    IDX_HEADS * IDX_DH, IDX_DH, IDX_HEADS,
    MLA_WIDTH, GLA_WIDTH, DSA_WIDTH,
)
_IN_NAMES = ("cq", "ckv", "krope", "gq", "gk", "gv", "glow", "dq", "dk", "dv",
             "iq", "ik", "iw", "z_mla", "z_gla", "z_dsa")
_IN_OFF = dict(zip(_IN_NAMES, np.concatenate([[0], np.cumsum(IN_SPLITS)[:-1]]).tolist()))
_IN_W = dict(zip(_IN_NAMES, IN_SPLITS))

LANE = 128
MLA_QK_PAD = 256
ATT_TQ = 2048
ATT_TK = 512
MLA_HEADS_PER_STEP = 4
LOG2E = 1.4426950408889634
DSA_TQ = 128
MM_CHUNK = 256
VMEM_LIMIT = 56 * 1024 * 1024

_PM_LAYOUT = (("dq", 1024), ("gv", 1024), ("iq", 2048), ("gq", 512), ("cq", 768), ("dk", 256),
              ("ckv", 512), ("gk", 512), ("dv", 256), ("kr", 128), ("krr", 128), ("glow", 128),
              ("ik", 128), ("iw", 128), ("pad", 128))
_PM_OFF = {}
_o = 0
for _n, _w in _PM_LAYOUT:
    assert _o % _w == 0
    _PM_OFF[_n] = _o
    _o += _w
PM_WIDTH = _o

BF16 = jnp.bfloat16
F32 = jnp.float32
INT_MIN = -2 ** 31


def _cp(n_axes):
    return pltpu.CompilerParams(dimension_semantics=("arbitrary",) * n_axes,
                                vmem_limit_bytes=VMEM_LIMIT)


def _nt(a, b):
    return lax.dot_general(a, b, (((1,), (1,)), ((), ())), preferred_element_type=F32)


def _rms(x, g):
    return x * lax.rsqrt(jnp.mean(x * x, axis=-1, keepdims=True) + NORM_EPS) * g


def _mm_kernel(*refs, a_act, act, has_bias):
    if has_bias:
        a_ref, w_ref, b_ref, o_ref = refs
    else:
        a_ref, w_ref, o_ref = refs
    a = a_ref[...]
    if a_act == "silu":
        a = a.astype(F32)
        a = a * jax.nn.sigmoid(a)
    a = a.astype(BF16)
    for c0 in range(0, o_ref.shape[1], MM_CHUNK):
        cs = slice(c0, c0 + MM_CHUNK)
        r = jnp.dot(a, w_ref[:, cs].astype(BF16), preferred_element_type=F32)
        if has_bias:
            r = r + b_ref[:, cs]
        if act == "sigmoid":
            r = jax.nn.sigmoid(r)
        elif act == "silu":
            r = r * jax.nn.sigmoid(r)
        o_ref[:, cs] = r.astype(o_ref.dtype)


def _mm(a, w, bias, *, tm, tn, out_dtype, a_act=None, act=None, layer=None, name):
    m, k = a.shape
    n = w.shape[-1]
    if layer is None:
        w_spec = pl.BlockSpec((k, tn), lambda i, j: (0, j))
    else:
        w_spec = pl.BlockSpec((None, k, tn), lambda i, j: (layer, 0, j))
    in_specs = [pl.BlockSpec((tm, k), lambda i, j: (i, 0)), w_spec]
    args = [a, w]
    if bias is not None:
        in_specs.append(pl.BlockSpec((1, tn), lambda i, j: (0, j)))
        args.append(bias)
    return pl.pallas_call(
        functools.partial(_mm_kernel, a_act=a_act, act=act, has_bias=bias is not None),
        grid=(m // tm, n // tn),
        in_specs=in_specs,
        out_specs=pl.BlockSpec((tm, tn), lambda i, j: (i, j)),
        out_shape=jax.ShapeDtypeStruct((m, n), out_dtype),
        compiler_params=_cp(2),
        name=name,
    )(*args)


def _norm_mod_kernel(x_ref, g_ref, sc_ref, sh_ref, o_ref):
    y = _rms(x_ref[0], g_ref[...])
    o_ref[0] = (y * (1.0 + sc_ref[0]) + sh_ref[0]).astype(o_ref.dtype)


def _norm_mod(x, g, scale, shift, *, tm=256):
    b, s, d = x.shape
    return pl.pallas_call(
        _norm_mod_kernel,
        grid=(b, s // tm),
        in_specs=[pl.BlockSpec((1, tm, d), lambda bi, i: (bi, i, 0)),
                  pl.BlockSpec((1, d), lambda bi, i: (0, 0)),
                  pl.BlockSpec((1, 1, d), lambda bi, i: (bi, 0, 0)),
                  pl.BlockSpec((1, 1, d), lambda bi, i: (bi, 0, 0))],
        out_specs=pl.BlockSpec((1, tm, d), lambda bi, i: (bi, i, 0)),
        out_shape=jax.ShapeDtypeStruct((b, s, d), BF16),
        compiler_params=_cp(2),
        name="norm_mod",
    )(x, g, scale, shift)


def _final_norm_kernel(x_ref, g_ref, o_ref):
    o_ref[0] = _rms(x_ref[0], g_ref[...])


def _final_norm(x, g, *, tm=256):
    b, s, d = x.shape
    return pl.pallas_call(
        _final_norm_kernel,
        grid=(b, s // tm),
        in_specs=[pl.BlockSpec((1, tm, d), lambda bi, i: (bi, i, 0)),
                  pl.BlockSpec((1, d), lambda bi, i: (0, 0))],
        out_specs=pl.BlockSpec((1, tm, d), lambda bi, i: (bi, i, 0)),
        out_shape=jax.ShapeDtypeStruct((b, s, d), F32),
        compiler_params=_cp(2),
        name="final_norm",
    )(x, g)


def _rope_kernel(pos_ref, inv_ref, cos_ref, sin_ref):
    ang = pos_ref[0] * inv_ref[...]
    live = lax.broadcasted_iota(jnp.int32, ang.shape, 1) < MLA_ROPE
    cos_ref[0] = jnp.where(live, jnp.cos(ang), 0.0)
    sin_ref[0] = jnp.where(live, jnp.sin(ang), 0.0)


def _rope_tables(positions, *, tm=512):
    b, s = positions.shape
    inv = ROPE_THETA ** (-jnp.arange(0, MLA_ROPE, 2, dtype=F32) / MLA_ROPE)
    inv128 = jnp.concatenate([inv, inv, jnp.zeros((LANE - MLA_ROPE,), F32)])[None, :]
    pos128 = jnp.broadcast_to(positions.astype(F32)[:, :, None], (b, s, LANE))
    spec = pl.BlockSpec((1, tm, LANE), lambda bi, i: (bi, i, 0))
    return pl.pallas_call(
        _rope_kernel,
        grid=(b, s // tm),
        in_specs=[spec, pl.BlockSpec((1, LANE), lambda bi, i: (0, 0))],
        out_specs=[spec, spec],
        out_shape=[jax.ShapeDtypeStruct((b, s, LANE), F32)] * 2,
        compiler_params=_cp(2),
        name="rope_tables",
    )(pos128, inv128)


def _mla_q_kernel(cq_ref, g_ref, w1_ref, w2_ref, cos_ref, sin_ref, o_ref, an_ref, *, scale):
    @pl.when(pl.program_id(2) == 0)
    def _():
        an_ref[...] = _rms(cq_ref[0].astype(F32), g_ref[...]).astype(BF16)

    a = an_ref[...]
    a1 = jnp.dot(a, w1_ref[...], preferred_element_type=F32)
    a2 = jnp.dot(a, w2_ref[...], preferred_element_type=F32)
    cos, sin = cos_ref[0], sin_ref[0]
    for u in range(MLA_HEADS_PER_STEP):
        c0 = u * MLA_QK_PAD
        rope = a1[:, c0 + MLA_NOPE:c0 + MLA_QK_PAD] * cos + a2[:, u * LANE:(u + 1) * LANE] * sin
        o_ref[0, :, c0:c0 + MLA_NOPE] = (a1[:, c0:c0 + MLA_NOPE] * scale).astype(BF16)
        o_ref[0, :, c0 + MLA_NOPE:c0 + MLA_QK_PAD] = (rope * scale).astype(BF16)


def _mla_q(pm3, g_q, w1, w2, cos128, sin128, *, tm=512):
    b, s, _ = pm3.shape
    hp = MLA_HEADS_PER_STEP
    cq_blk = _PM_OFF["cq"] // MLA_Q_RANK
    return pl.pallas_call(
        functools.partial(_mla_q_kernel, scale=(MLA_NOPE + MLA_ROPE) ** -0.5 * LOG2E),
        grid=(b, s // tm, MLA_HEADS // hp),
        in_specs=[pl.BlockSpec((1, tm, MLA_Q_RANK), lambda bi, i, h: (bi, i, cq_blk)),
                  pl.BlockSpec((1, MLA_Q_RANK), lambda bi, i, h: (0, 0)),
                  pl.BlockSpec((MLA_Q_RANK, hp * MLA_QK_PAD), lambda bi, i, h: (0, h)),
                  pl.BlockSpec((MLA_Q_RANK, hp * LANE), lambda bi, i, h: (0, h)),
                  pl.BlockSpec((1, tm, LANE), lambda bi, i, h: (bi, i, 0)),
                  pl.BlockSpec((1, tm, LANE), lambda bi, i, h: (bi, i, 0))],
        out_specs=pl.BlockSpec((1, tm, hp * MLA_QK_PAD), lambda bi, i, h: (bi, i, h)),
        out_shape=jax.ShapeDtypeStruct((b, s, MLA_HEADS * MLA_QK_PAD), BF16),
        scratch_shapes=[pltpu.VMEM((tm, MLA_Q_RANK), BF16)],
        compiler_params=_cp(3),
        name="mla_q",
    )(pm3, g_q, w1, w2, cos128, sin128)


def _mla_kv_kernel(ckv_ref, g_ref, w_ref, kr_ref, krr_ref, cos_ref, sin_ref, k_ref, v_ref, an_ref, krt_ref):
    @pl.when(pl.program_id(2) == 0)
    def _():
        an_ref[...] = _rms(ckv_ref[0].astype(F32), g_ref[...]).astype(BF16)
        kr = kr_ref[0].astype(F32) * cos_ref[0] + krr_ref[0].astype(F32) * sin_ref[0]
        krt_ref[...] = kr.T.astype(BF16)

    acc = jnp.dot(an_ref[...], w_ref[...], preferred_element_type=F32)
    ones = jnp.ones((acc.shape[0], MLA_V), BF16)
    for u in range(MLA_HEADS_PER_STEP):
        c0 = u * (MLA_NOPE + MLA_V)
        k_ref[0, u, 0, :MLA_NOPE, :] = acc[:, c0:c0 + MLA_NOPE].T.astype(BF16)
        k_ref[0, u, 0, MLA_NOPE:, :] = krt_ref[...]
        v_ref[0, u, 0, :, :MLA_V] = acc[:, c0 + MLA_NOPE:c0 + MLA_NOPE + MLA_V].astype(BF16)
        v_ref[0, u, 0, :, MLA_V:] = ones


def _mla_kv(pm3, g_kv, w_ukv, cos128, sin128):
    b, s, _ = pm3.shape
    tm = ATT_TK
    hp = MLA_HEADS_PER_STEP
    ckv_blk = _PM_OFF["ckv"] // MLA_KV_RANK
    kr_blk = _PM_OFF["kr"] // LANE
    krr_blk = _PM_OFF["krr"] // LANE
    return pl.pallas_call(
        _mla_kv_kernel,
        grid=(b, s // tm, MLA_HEADS // hp),
        in_specs=[pl.BlockSpec((1, tm, MLA_KV_RANK), lambda bi, i, h: (bi, i, ckv_blk)),
                  pl.BlockSpec((1, MLA_KV_RANK), lambda bi, i, h: (0, 0)),
                  pl.BlockSpec((MLA_KV_RANK, hp * (MLA_NOPE + MLA_V)), lambda bi, i, h: (0, h)),
                  pl.BlockSpec((1, tm, LANE), lambda bi, i, h: (bi, i, kr_blk)),
                  pl.BlockSpec((1, tm, LANE), lambda bi, i, h: (bi, i, krr_blk)),
                  pl.BlockSpec((1, tm, LANE), lambda bi, i, h: (bi, i, 0)),
                  pl.BlockSpec((1, tm, LANE), lambda bi, i, h: (bi, i, 0))],
        out_specs=[pl.BlockSpec((1, hp, 1, MLA_QK_PAD, tm), lambda bi, i, h: (bi, h, i, 0, 0)),
                   pl.BlockSpec((1, hp, 1, tm, 2 * MLA_V), lambda bi, i, h: (bi, h, i, 0, 0))],
        out_shape=[jax.ShapeDtypeStruct((b, MLA_HEADS, s // tm, MLA_QK_PAD, tm), BF16),
                   jax.ShapeDtypeStruct((b, MLA_HEADS, s // tm, tm, 2 * MLA_V), BF16)],
        scratch_shapes=[pltpu.VMEM((tm, MLA_KV_RANK), BF16), pltpu.VMEM((LANE, tm), BF16)],
        compiler_params=_cp(3),
        name="mla_kv",
    )(pm3, g_kv, w_ukv, pm3, pm3, cos128, sin128)


def _sm_init(acc_ref, m_ref):
    m_ref[...] = jnp.full(m_ref.shape, NEG, F32)
    acc_ref[...] = jnp.zeros(acc_ref.shape, F32)


def _sm_step(s, v_ext, acc_ref, m_ref):
    m_old = m_ref[...]
    m_new = jnp.maximum(m_old, jnp.max(s, axis=-1, keepdims=True))
    alpha = jnp.exp2(m_old - m_new)
    p = jnp.exp2(s - jnp.tile(m_new, (1, s.shape[1] // LANE)))
    pv = jnp.dot(p.astype(BF16), v_ext, preferred_element_type=F32)
    acc_ref[...] = jnp.tile(alpha, (1, acc_ref.shape[1] // LANE)) * acc_ref[...] + pv
    m_ref[...] = m_new


def _sm_result(acc_ref, d):
    acc = acc_ref[...]
    return acc[:, :d] / acc[:, d:]


def _mla_attn_kernel(q_ref, kt_ref, v_ref, z_ref, o_ref, s_ref, acc_ref, m_ref):
    i = pl.program_id(2)
    t = ATT_TK
    n_chains = ATT_TQ // t
    assert n_chains % 2 == 0
    chains = range(n_chains)
    qpos = lax.broadcasted_iota(jnp.int32, (t, t), 0)
    kpos = lax.broadcasted_iota(jnp.int32, (t, t), 1)

    def scores(u, j):
        return jnp.dot(q_ref[0, u * t:(u + 1) * t, :], kt_ref[0, 0, j], preferred_element_type=F32)

    def update(u, s, j, diagonal):
        if diagonal:
            s = jnp.where(kpos <= qpos, s, NEG)
        _sm_step(s, v_ref[0, 0, j], acc_ref.at[u], m_ref.at[u])

    def consume(u, buf, j, diagonal):
        update(u, s_ref[buf, u], j, diagonal)

    for u in chains:
        _sm_init(acc_ref.at[u], m_ref.at[u])
        s_ref[0, u] = scores(u, 0)

    def block_pair(p, carry):
        j = 2 * p
        for u in chains:
            consume(u, 0, j, False)
            s_ref[1, u] = scores(u, j + 1)
        for u in chains:
            consume(u, 1, j + 1, False)
            s_ref[0, u] = scores(u, j + 2)
        return carry

    first = n_chains * i
    lax.fori_loop(0, first // 2, block_pair, 0)
    for u in chains:
        consume(u, 0, first, u == 0)
        for d in range(1, u + 1):
            update(u, scores(u, first + d), first + d, d == u)
    for u in chains:
        rows = slice(u * t, (u + 1) * t)
        o_ref[0, rows, :] = (_sm_result(acc_ref.at[u], MLA_V) * z_ref[0, rows, :].astype(F32)).astype(o_ref.dtype)


def _mla_attn(q, kt, v_ext, pz3):
    b, s, _ = q.shape
    nk = s // ATT_TK
    return pl.pallas_call(
        _mla_attn_kernel,
        grid=(b, MLA_HEADS, s // ATT_TQ),
        in_specs=[pl.BlockSpec((1, ATT_TQ, MLA_QK_PAD), lambda bi, h, i: (bi, i, h)),
                  pl.BlockSpec((1, 1, nk, MLA_QK_PAD, ATT_TK), lambda bi, h, i: (bi, h, 0, 0, 0)),
                  pl.BlockSpec((1, 1, nk, ATT_TK, 2 * MLA_V), lambda bi, h, i: (bi, h, 0, 0, 0)),
                  pl.BlockSpec((1, ATT_TQ, MLA_V), lambda bi, h, i: (bi, i, h))],
        out_specs=pl.BlockSpec((1, ATT_TQ, MLA_V), lambda bi, h, i: (bi, i, h)),
        out_shape=jax.ShapeDtypeStruct((b, s, MLA_WIDTH), BF16),
        scratch_shapes=[pltpu.VMEM((2, ATT_TQ // ATT_TK, ATT_TK, ATT_TK), F32),
                        pltpu.VMEM((ATT_TQ // ATT_TK, ATT_TK, 2 * MLA_V), F32),
                        pltpu.VMEM((ATT_TQ // ATT_TK, ATT_TK, LANE), F32)],
        compiler_params=_cp(3),
        name="mla_attn",
    )(q, kt, v_ext, pz3)


def _log_sigmoid(x):
    return jnp.minimum(x, 0.0) - jnp.log(1.0 + jnp.exp(-jnp.abs(x)))


def _gla_kernel(q_ref, k_ref, v_ref, gl_ref, z_ref, w2_ref, bg_ref, go_ref, o_ref, st_ref, *, nb):
    @pl.when(pl.program_id(0) == 0)
    def _():
        st_ref[...] = jnp.zeros(st_ref.shape, F32)

    c = GLA_CHUNK
    row = lax.broadcasted_iota(jnp.int32, (c, c), 0)
    col = lax.broadcasted_iota(jnp.int32, (c, c), 1)
    tril = row >= col
    tril_b = jnp.where(tril, 1.0, 0.0).astype(BF16)
    qw = GLA_HEADS * GLA_DK
    for b in range(nb):
        pre = jnp.dot(gl_ref[b], w2_ref[...], preferred_element_type=F32) + bg_ref[...]
        glog = _log_sigmoid(pre) / GLA_GATE_NORM
        g1 = glog.astype(BF16)
        r1 = glog - g1.astype(F32)
        g2 = r1.astype(BF16)
        g3 = (r1 - g2.astype(F32)).astype(BF16)
        cs = jnp.dot(tril_b, jnp.concatenate([g1, g2, g3], axis=1), preferred_element_type=F32)
        bc_all = cs[:, :qw] + cs[:, qw:2 * qw] + cs[:, 2 * qw:]
        for h in range(GLA_HEADS):
            ks = slice(h * GLA_DK, (h + 1) * GLA_DK)
            vs = slice(h * GLA_DV, (h + 1) * GLA_DV)
            q = q_ref[b, :, ks].astype(F32)
            k = k_ref[b, :, ks].astype(F32)
            v = v_ref[b, :, vs]
            bc = bc_all[:, ks]
            b_last = bc[c - 1:c, :]
            q_dec = (q * GLA_DK ** -0.5 * jnp.exp(bc)).astype(BF16)
            k_inv = (k * jnp.exp(-bc)).astype(BF16)
            k_end = (k * jnp.exp(b_last - bc)).astype(BF16)
            decay = jnp.exp(b_last)
            attn = jnp.where(tril, _nt(q_dec, k_inv), 0.0).astype(BF16)
            st = st_ref[b * GLA_HEADS + h]
            o = jnp.dot(jnp.concatenate([q_dec, attn], axis=1),
                        jnp.concatenate([st.astype(BF16), v], axis=0), preferred_element_type=F32)
            upd = lax.dot_general(k_end, v, (((0,), (0,)), ((), ())), preferred_element_type=F32)
            decay_col = jnp.tile(jnp.broadcast_to(decay, (GLA_DK, GLA_DK)).T, (1, GLA_DV // GLA_DK))
            st_ref[b * GLA_HEADS + h] = st * decay_col + upd
            on = _rms(o, go_ref[...])
            o_ref[b, :, vs] = (on * z_ref[b, :, vs].astype(F32)).astype(o_ref.dtype)


def _gla(pm3, pz3, w2p, bg, gout):
    b, s, _ = pm3.shape
    c = GLA_CHUNK
    qw = GLA_HEADS * GLA_DK
    q_blk, k_blk = _PM_OFF["gq"] // qw, _PM_OFF["gk"] // qw
    v_blk, gl_blk = _PM_OFF["gv"] // GLA_WIDTH, _PM_OFF["glow"] // LANE
    z_blk = MLA_WIDTH // GLA_WIDTH
    return pl.pallas_call(
        functools.partial(_gla_kernel, nb=b),
        grid=(s // c,),
        in_specs=[pl.BlockSpec((b, c, qw), lambda i: (0, i, q_blk)),
                  pl.BlockSpec((b, c, qw), lambda i: (0, i, k_blk)),
                  pl.BlockSpec((b, c, GLA_WIDTH), lambda i: (0, i, v_blk)),
                  pl.BlockSpec((b, c, LANE), lambda i: (0, i, gl_blk)),
                  pl.BlockSpec((b, c, GLA_WIDTH), lambda i: (0, i, z_blk)),
                  pl.BlockSpec((LANE, qw), lambda i: (0, 0)),
                  pl.BlockSpec((1, qw), lambda i: (0, 0)),
                  pl.BlockSpec((1, GLA_DV), lambda i: (0, 0))],
        out_specs=pl.BlockSpec((b, c, GLA_WIDTH), lambda i: (0, i, 0)),
        out_shape=jax.ShapeDtypeStruct((b, s, GLA_WIDTH), BF16),
        scratch_shapes=[pltpu.VMEM((b * GLA_HEADS, GLA_DK, GLA_DV), F32)],
        compiler_params=_cp(1),
        name="gla",
    )(pm3, pm3, pm3, pm3, pz3, w2p, bg, gout)


def _dsa_kernel(iq_ref, iw_ref, onehot_ref, ikt_ref, q_ref, kt_ref, v_ref, z_ref, o_ref,
                key_ref, keyt_ref, iqs_ref, wb_ref, qs_ref, s_ref, acc_ref, m_ref, *, top_k):
    tq, tk = DSA_TQ, ATT_TK
    nk = key_ref.shape[0]
    lanes = tk // LANE
    i = pl.program_id(1)
    nkb = (i * tq + tq - 1) // tk + 1
    qpos = i * tq + lax.broadcasted_iota(jnp.int32, (tq, tk), 0)
    heads_per_dot = 4

    for h in range(IDX_HEADS):
        iqs_ref[h] = iq_ref[0, :, h * IDX_DH:(h + 1) * IDX_DH]

    wb = jnp.dot(iw_ref[0], onehot_ref[...], preferred_element_type=F32)
    for h in range(IDX_HEADS):
        wb_ref[h] = wb[:, h * LANE:(h + 1) * LANE] * (IDX_HEADS ** -0.5 * IDX_DH ** -0.5)

    def sortable(x):
        bits = lax.bitcast_convert_type(x, jnp.int32)
        return bits ^ ((bits >> 31) & 0x7FFFFFFF)

    qpos_t = i * tq + lax.broadcasted_iota(jnp.int32, (tk, tq), 1)

    def score_block(j, carry):
        ikt = ikt_ref[0, j]
        sc = jnp.zeros((tq, tk), F32)
        for hg in range(IDX_HEADS // heads_per_dot):
            iq = iqs_ref[hg * heads_per_dot:(hg + 1) * heads_per_dot].reshape(heads_per_dot * tq, IDX_DH)
            r = jnp.dot(iq, ikt, preferred_element_type=F32)
            for hh in range(heads_per_dot):
                wrow = jnp.tile(wb_ref[hg * heads_per_dot + hh], (1, lanes))
                sc = sc + jnp.maximum(r[hh * tq:(hh + 1) * tq, :], 0.0) * wrow
        kpos = j * tk + lax.broadcasted_iota(jnp.int32, (tq, tk), 1)
        key_ref[j] = jnp.where(kpos <= qpos, sortable(sc), INT_MIN)
        kpos_t = j * tk + lax.broadcasted_iota(jnp.int32, (tk, tq), 0)
        keyt_ref[j] = jnp.where(kpos_t <= qpos_t, sortable(sc.T), INT_MIN)
        return carry

    lax.fori_loop(0, nkb, score_block, 0)

    def count(pred):
        def body(j, acc):
            hit = jnp.where(pred(keyt_ref[j], j), 1, 0).astype(jnp.int32)
            return acc + hit.reshape(tk // 8, 8, tq).sum(axis=0)
        acc = lax.fori_loop(0, nkb, body, jnp.zeros((8, tq), jnp.int32))
        return acc.sum(axis=0, keepdims=True)

    def count_ge(cand):
        return count(lambda k, j: k >= cand)

    zero = jnp.zeros((1, tq), jnp.int32)
    c0 = count_ge(zero)
    thr = jnp.where(c0 >= top_k, zero, INT_MIN)
    n_ge = jnp.where(c0 >= top_k, c0, nkb * tk)

    def bit_step(it, carry):
        thr, n_ge = carry
        cand = thr | lax.shift_left(jnp.int32(1), 30 - it)
        c = count_ge(cand)
        return jnp.where(c >= top_k, cand, thr), jnp.where(c >= top_k, c, n_ge)

    thr, n_ge = lax.fori_loop(0, 31, bit_step, (thr, n_ge))

    overflow = (thr > INT_MIN) & (n_ge > top_k)

    @pl.when(jnp.max(jnp.where(overflow, 1, 0)) > 0)
    def _():
        n_take = top_k - count(lambda k, j: k > thr)
        rows = lax.broadcasted_iota(jnp.int32, (tk, tq), 0)
        bound = jnp.zeros((1, tq), jnp.int32)
        for bit in range((nk * tk - 1).bit_length() - 1, -1, -1):
            cand = bound | (1 << bit)
            below = count(lambda k, j: (k == thr) & (j * tk + rows < cand))
            bound = jnp.where(below < n_take, cand, bound)
        bound = jnp.where(overflow, bound, nk * tk)
        thr_c = jnp.tile(jnp.broadcast_to(thr, (tq, tq)).T, (1, lanes))
        bound_c = jnp.tile(jnp.broadcast_to(bound, (tq, tq)).T, (1, lanes))

        def demote(j, carry):
            k = key_ref[j]
            kpos = j * tk + lax.broadcasted_iota(jnp.int32, (tq, tk), 1)
            key_ref[j] = jnp.where((k == thr_c) & (kpos > bound_c), INT_MIN, k)
            return carry

        lax.fori_loop(0, nkb, demote, 0)

    thr = jnp.maximum(thr, INT_MIN + 1)
    thr_w = jnp.tile(jnp.broadcast_to(thr, (tq, tq)).T, (1, lanes))

    scale = DSA_DH ** -0.5 * LOG2E
    groups = range(DSA_KV_HEADS)

    def scores(g, j):
        return jnp.dot(qs_ref[g], kt_ref[0, g, j], preferred_element_type=F32)

    def consume(g, buf, j):
        sel = key_ref[j] >= thr_w
        s = s_ref[buf, g]
        s = jnp.concatenate([jnp.where(sel, s[r * tq:(r + 1) * tq, :], NEG) for r in range(DSA_REP)], axis=0)
        _sm_step(s, v_ref[0, g, j], acc_ref.at[g], m_ref.at[g])

    for g in groups:
        for r in range(DSA_REP):
            cs = slice((g * DSA_REP + r) * DSA_DH, (g * DSA_REP + r + 1) * DSA_DH)
            qs_ref[g, r * tq:(r + 1) * tq, :] = (q_ref[0, :, cs].astype(F32) * scale).astype(BF16)
        _sm_init(acc_ref.at[g], m_ref.at[g])
    for g in groups:
        s_ref[0, g] = scores(g, 0)

    def block_pair(p, carry):
        j = 2 * p
        for g in groups:
            consume(g, 0, j)
            s_ref[1, g] = scores(g, j + 1)
        for g in groups:
            consume(g, 1, j + 1)
            s_ref[0, g] = scores(g, jnp.minimum(j + 2, nk - 1))
        return carry

    lax.fori_loop(0, nkb // 2, block_pair, 0)

    @pl.when(nkb % 2 == 1)
    def _():
        for g in groups:
            consume(g, 0, nkb - 1)

    for g in groups:
        res = _sm_result(acc_ref.at[g], DSA_DH)
        for r in range(DSA_REP):
            cs = slice((g * DSA_REP + r) * DSA_DH, (g * DSA_REP + r + 1) * DSA_DH)
            o_ref[0, :, cs] = (res[r * tq:(r + 1) * tq, :] * z_ref[0, :, cs].astype(F32)).astype(o_ref.dtype)


def _dsa(pm3, pz3, ikt, dkt, dv_ext, top_k):
    b, s, _ = pm3.shape
    tq, tk = DSA_TQ, ATT_TK
    nk = s // tk
    q_blk = _PM_OFF["dq"] // DSA_WIDTH
    z_blk = (MLA_WIDTH + GLA_WIDTH) // DSA_WIDTH
    rows = DSA_REP * tq
    iq_blk = _PM_OFF["iq"] // (IDX_HEADS * IDX_DH)
    assert tq == LANE
    iw_blk = _PM_OFF["iw"] // LANE
    onehot = jnp.repeat(jnp.eye(LANE, IDX_HEADS, dtype=BF16), LANE, axis=1)
    return pl.pallas_call(
        functools.partial(_dsa_kernel, top_k=top_k),
        grid=(b, s // tq),
        in_specs=[pl.BlockSpec((1, tq, IDX_HEADS * IDX_DH), lambda bi, i: (bi, i, iq_blk)),
                  pl.BlockSpec((1, tq, LANE), lambda bi, i: (bi, i, iw_blk)),
                  pl.BlockSpec((LANE, IDX_HEADS * LANE), lambda bi, i: (0, 0)),
                  pl.BlockSpec((1, nk, IDX_DH, tk), lambda bi, i: (bi, 0, 0, 0)),
                  pl.BlockSpec((1, tq, DSA_WIDTH), lambda bi, i: (bi, i, q_blk)),
                  pl.BlockSpec((1, DSA_KV_HEADS, nk, DSA_DH, tk), lambda bi, i: (bi, 0, 0, 0, 0)),
                  pl.BlockSpec((1, DSA_KV_HEADS, nk, tk, 2 * DSA_DH), lambda bi, i: (bi, 0, 0, 0, 0)),
                  pl.BlockSpec((1, tq, DSA_WIDTH), lambda bi, i: (bi, i, z_blk))],
        out_specs=pl.BlockSpec((1, tq, DSA_WIDTH), lambda bi, i: (bi, i, 0)),
        out_shape=jax.ShapeDtypeStruct((b, s, DSA_WIDTH), BF16),
        scratch_shapes=[pltpu.VMEM((nk, tq, tk), jnp.int32),
                        pltpu.VMEM((nk, tk, tq), jnp.int32),
                        pltpu.VMEM((IDX_HEADS, tq, IDX_DH), BF16),
                        pltpu.VMEM((IDX_HEADS, tq, LANE), F32),
                        pltpu.VMEM((DSA_KV_HEADS, rows, DSA_DH), BF16),
                        pltpu.VMEM((2, DSA_KV_HEADS, rows, tk), F32),
                        pltpu.VMEM((DSA_KV_HEADS, rows, 2 * DSA_DH), F32),
                        pltpu.VMEM((DSA_KV_HEADS, rows, LANE), F32)],
        compiler_params=_cp(2),
        name="dsa",
    )(pm3, pm3, onehot, ikt, pm3, dkt, dv_ext, pz3)


def _dsa_layout_kernel(ik_ref, dk_ref, dv_ref, ikt_ref, kt_ref, vx_ref):
    ikt_ref[0, 0] = ik_ref[0].astype(F32).T[:IDX_DH, :].astype(BF16)
    ones = jnp.ones((ATT_TK, DSA_DH), BF16)
    for g in range(DSA_KV_HEADS):
        cs = slice(g * DSA_DH, (g + 1) * DSA_DH)
        kt_ref[0, g, 0] = dk_ref[0, :, cs].astype(F32).T.astype(BF16)
        vx_ref[0, g, 0, :, :DSA_DH] = dv_ref[0, :, cs]
        vx_ref[0, g, 0, :, DSA_DH:] = ones


def _dsa_layout(pm3):
    b, s, _ = pm3.shape
    tk = ATT_TK
    nk = s // tk
    kvw = DSA_KV_HEADS * DSA_DH
    ik_blk, dk_blk, dv_blk = _PM_OFF["ik"] // LANE, _PM_OFF["dk"] // kvw, _PM_OFF["dv"] // kvw
    return pl.pallas_call(
        _dsa_layout_kernel,
        grid=(b, nk),
        in_specs=[pl.BlockSpec((1, tk, LANE), lambda bi, j: (bi, j, ik_blk)),
                  pl.BlockSpec((1, tk, kvw), lambda bi, j: (bi, j, dk_blk)),
                  pl.BlockSpec((1, tk, kvw), lambda bi, j: (bi, j, dv_blk))],
        out_specs=[pl.BlockSpec((1, 1, IDX_DH, tk), lambda bi, j: (bi, j, 0, 0)),
                   pl.BlockSpec((1, DSA_KV_HEADS, 1, DSA_DH, tk), lambda bi, j: (bi, 0, j, 0, 0)),
                   pl.BlockSpec((1, DSA_KV_HEADS, 1, tk, 2 * DSA_DH), lambda bi, j: (bi, 0, j, 0, 0))],
        out_shape=[jax.ShapeDtypeStruct((b, nk, IDX_DH, tk), BF16),
                   jax.ShapeDtypeStruct((b, DSA_KV_HEADS, nk, DSA_DH, tk), BF16),
                   jax.ShapeDtypeStruct((b, DSA_KV_HEADS, nk, tk, 2 * DSA_DH), BF16)],
        compiler_params=_cp(2),
        name="dsa_layout",
    )(pm3, pm3, pm3)


def _lift_kernel(ym_ref, yg_ref, yd_ref, wm_ref, wg_ref, wd_ref, ga_ref, gb_ref, gc_ref, o_ref):
    a = jnp.dot(ym_ref[...], wm_ref[...], preferred_element_type=F32)
    b = jnp.dot(yg_ref[...], wg_ref[...], preferred_element_type=F32)
    c = jnp.dot(yd_ref[...], wd_ref[...], preferred_element_type=F32)
    o = ga_ref[...].astype(F32) * a + gb_ref[...].astype(F32) * b + gc_ref[...].astype(F32) * c
    o_ref[...] = o.astype(o_ref.dtype)


def _lift(y_mla, y_gla, y_dsa, w_bm, w_bg, w_bd, gates, *, tm=512, tn=1024):
    t = y_mla.shape[0]
    d = w_bm.shape[1]
    nj = d // tn
    row = lambda width: pl.BlockSpec((tm, width), lambda i, j: (i, 0))
    wcol = lambda depth: pl.BlockSpec((depth, tn), lambda i, j: (0, j))
    gate = lambda br: pl.BlockSpec((tm, tn), lambda i, j: (i, br * nj + j))
    return pl.pallas_call(
        _lift_kernel,
        grid=(t // tm, nj),
        in_specs=[row(MLA_WIDTH), row(GLA_WIDTH), row(DSA_WIDTH),
                  wcol(MLA_WIDTH), wcol(GLA_WIDTH), wcol(DSA_WIDTH),
                  gate(0), gate(1), gate(2)],
        out_specs=pl.BlockSpec((tm, tn), lambda i, j: (i, j)),
        out_shape=jax.ShapeDtypeStruct((t, d), BF16),
        compiler_params=_cp(2),
        name="lift_merge",
    )(y_mla, y_gla, y_dsa, w_bm, w_bg, w_bd, gates, gates, gates)


def _out_kernel(m_ref, w_ref, x_ref, gate_ref, o_ref):
    r = jnp.dot(m_ref[0], w_ref[...], preferred_element_type=F32)
    o_ref[0] = x_ref[0] + gate_ref[0] * r


def _out_proj(merged3, w_o, x, gate, *, tm=1024, tn=1024):
    b, s, d = x.shape
    tm = min(tm, s)
    return pl.pallas_call(
        _out_kernel,
        grid=(b, s // tm, d // tn),
        in_specs=[pl.BlockSpec((1, tm, d), lambda bi, i, j: (bi, i, 0)),
                  pl.BlockSpec((d, tn), lambda bi, i, j: (0, j)),
                  pl.BlockSpec((1, tm, tn), lambda bi, i, j: (bi, i, j)),
                  pl.BlockSpec((1, 1, tn), lambda bi, i, j: (bi, 0, j))],
        out_specs=pl.BlockSpec((1, tm, tn), lambda bi, i, j: (bi, i, j)),
        out_shape=jax.ShapeDtypeStruct((b, s, d), F32),
        compiler_params=_cp(3),
        name="out_proj",
    )(merged3, w_o, x, gate)


def _win_layout_kernel(w_ref, om_ref, oz_ref):
    half = MLA_ROPE // 2
    for name, width in _PM_LAYOUT:
        dst = _PM_OFF[name]
        if name == "pad":
            om_ref[:, dst:dst + width] = jnp.zeros((om_ref.shape[0], width), BF16)
        elif name in ("kr", "krr", "glow", "ik", "iw"):
            src_name = {"kr": "krope", "krr": "krope"}.get(name, name)
            src, w = _IN_OFF[src_name], _IN_W[src_name]
            om_ref[:, dst:dst + width] = jnp.zeros((om_ref.shape[0], width), BF16)
            if name == "krr":
                om_ref[:, dst:dst + half] = (-w_ref[:, src + half:src + w]).astype(BF16)
                om_ref[:, dst + half:dst + w] = w_ref[:, src:src + half].astype(BF16)
            else:
                om_ref[:, dst:dst + w] = w_ref[:, src:src + w].astype(BF16)
        else:
            src = _IN_OFF[name]
            om_ref[:, dst:dst + width] = w_ref[:, src:src + width].astype(BF16)
    oz_ref[...] = w_ref[:, _IN_OFF["z_mla"]:].astype(BF16)


def _win_layout(w_in, layer, *, tr=128):
    _, d, n = w_in.shape
    nz = n - _IN_OFF["z_mla"]
    return pl.pallas_call(
        _win_layout_kernel,
        grid=(d // tr,),
        in_specs=[pl.BlockSpec((None, tr, n), lambda i: (layer, i, 0))],
        out_specs=[pl.BlockSpec((tr, PM_WIDTH), lambda i: (i, 0)),
                   pl.BlockSpec((tr, nz), lambda i: (i, 0))],
        out_shape=[jax.ShapeDtypeStruct((d, PM_WIDTH), BF16), jax.ShapeDtypeStruct((d, nz), BF16)],
        compiler_params=_cp(1),
        name="win_layout",
    )(w_in)


def _mla_q_weights(w_uq):
    r = w_uq.shape[0]
    w = w_uq.reshape(r, MLA_HEADS, MLA_NOPE + MLA_ROPE)
    nope, rope = w[..., :MLA_NOPE], w[..., MLA_NOPE:]
    half = MLA_ROPE // 2
    zeros = jnp.zeros((r, MLA_HEADS, LANE - MLA_ROPE), w.dtype)
    w1 = jnp.concatenate([nope, rope, zeros], axis=-1).reshape(r, MLA_HEADS * MLA_QK_PAD)
    rot = jnp.concatenate([-rope[..., half:], rope[..., :half]], axis=-1)
    w2 = jnp.concatenate([rot, zeros], axis=-1).reshape(r, MLA_HEADS * LANE)
    return w1.astype(BF16), w2.astype(BF16)


def _layer(layer, x, c_pad, cos128, sin128, norm_g, w_ada, b_ada, w_in, mla_gq, mla_wuq, mla_gkv, mla_wukv,
           gla_wg2, gla_bg, gla_gout, w_mg, b_mg, w_bm, w_bg, w_bd, w_o):
    b, s, d = x.shape
    t = b * s
    mod = _mm(c_pad, w_ada, b_ada[None, :], tm=c_pad.shape[0], tn=512, out_dtype=F32,
              a_act="silu", layer=layer, name="ada")[:b]
    shift, scale, gate = (mod[:, None, k * d:(k + 1) * d] for k in range(3))
    h = _norm_mod(x, norm_g[None, :], scale, shift)
    h2 = h.reshape(t, d)

    mm_tm = min(1024, t)
    w_main, w_z = _win_layout(w_in, layer)
    pm = _mm(h2, w_main, None, tm=mm_tm, tn=1536, out_dtype=BF16, name="proj_main")
    pz = _mm(h2, w_z, None, tm=mm_tm, tn=1024, out_dtype=BF16, act="silu", name="proj_gate_paths")
    gates = _mm(h2, w_mg, b_mg[None, :], tm=mm_tm, tn=512, out_dtype=BF16,
                act="sigmoid", layer=layer, name="merge_gates")
    pm3 = pm.reshape(b, s, PM_WIDTH)
    pz3 = pz.reshape(b, s, d)

    w1, w2 = _mla_q_weights(mla_wuq)
    q = _mla_q(pm3, mla_gq[None, :], w1, w2, cos128, sin128)
    kcat, vt = _mla_kv(pm3, mla_gkv[None, :], mla_wukv.astype(BF16), cos128, sin128)
    y_mla = _mla_attn(q, kcat, vt, pz3)

    w2p = jnp.concatenate([gla_wg2, jnp.zeros((LANE - GLA_GATE_RANK, gla_wg2.shape[1]), gla_wg2.dtype)],
                          axis=0).astype(BF16)
    y_gla = _gla(pm3, pz3, w2p, gla_bg[None, :], gla_gout[None, :])

    ikt, dkt, dv_ext = _dsa_layout(pm3)
    y_dsa = _dsa(pm3, pz3, ikt, dkt, dv_ext, min(IDX_TOPK, s // 4))

    merged = _lift(y_mla.reshape(t, MLA_WIDTH), y_gla.reshape(t, GLA_WIDTH), y_dsa.reshape(t, DSA_WIDTH),
                   w_bm.astype(BF16), w_bg.astype(BF16), w_bd.astype(BF16), gates)
    return _out_proj(merged.reshape(b, s, d), w_o.astype(BF16), x, gate)


def kernel(x, c, positions, norm_g, w_ada, b_ada, w_in, mla_gq, mla_wuq, mla_gkv, mla_wukv, gla_wg2,
           gla_bg, gla_gout, w_mg, b_mg, w_bm, w_bg, w_bd, w_o, final_g):
    b = x.shape[0]
    cos128, sin128 = _rope_tables(positions)
    c_pad = jnp.concatenate([c, jnp.zeros((8 - b, c.shape[1]), c.dtype)], axis=0)
    for l in range(DEPTH):
        x = _layer(l, x, c_pad, cos128, sin128, norm_g[l], w_ada, b_ada[l], w_in, mla_gq[l],
                   mla_wuq[l], mla_gkv[l], mla_wukv[l], gla_wg2[l], gla_bg[l], gla_gout[l],
                   w_mg, b_mg[l], w_bm[l], w_bg[l], w_bd[l], w_o[l])
    return _final_norm(x, final_g[None, :])
```

```python
import functools

import jax
import jax.numpy as jnp
import numpy as np
from jax import lax
from jax.experimental import pallas as pl
from jax.experimental.pallas import tpu as pltpu

D_MODEL = 4096
DEPTH = 2
MLA_HEADS = 16
MLA_Q_RANK = 768
MLA_KV_RANK = 512
MLA_NOPE = 128
MLA_ROPE = 64
MLA_V = 128
ROPE_THETA = 10000.0
GLA_HEADS = 4
GLA_DK = 128
GLA_DV = 256
GLA_GATE_RANK = 16
GLA_GATE_NORM = 16.0
GLA_CHUNK = 64
DSA_HEADS = 8
DSA_KV_HEADS = 2
DSA_DH = 128
IDX_HEADS = 32
IDX_DH = 64
IDX_TOPK = 256
NORM_EPS = 1e-6
NEG = -1e30

MLA_WIDTH = MLA_HEADS * MLA_V
GLA_WIDTH = GLA_HEADS * GLA_DV
DSA_WIDTH = DSA_HEADS * DSA_DH
DSA_REP = DSA_HEADS // DSA_KV_HEADS

IN_SPLITS = (
    MLA_Q_RANK, MLA_KV_RANK, MLA_ROPE,
    GLA_HEADS * GLA_DK, GLA_HEADS * GLA_DK, GLA_WIDTH, GLA_GATE_RANK,
    DSA_WIDTH, DSA_KV_HEADS * DSA_DH, DSA_KV_HEADS * DSA_DH,
    IDX_HEADS * IDX_DH, IDX_DH, IDX_HEADS,
    MLA_WIDTH, GLA_WIDTH, DSA_WIDTH,
)
_IN_NAMES = ("cq", "ckv", "krope", "gq", "gk", "gv", "glow", "dq", "dk", "dv",
             "iq", "ik", "iw", "z_mla", "z_gla", "z_dsa")
_IN_OFF = dict(zip(_IN_NAMES, np.concatenate([[0], np.cumsum(IN_SPLITS)[:-1]]).tolist()))
_IN_W = dict(zip(_IN_NAMES, IN_SPLITS))

LANE = 128
MLA_QK_PAD = 256
ATT_TQ = 2048
ATT_TK = 512
MLA_HEADS_PER_STEP = 4
LOG2E = 1.4426950408889634
DSA_TQ = 128
MM_CHUNK = 256
VMEM_LIMIT = 56 * 1024 * 1024

_PM_LAYOUT = (("dq", 1024), ("gv", 1024), ("iq", 2048), ("gq", 512), ("cq", 768), ("dk", 256),
              ("ckv", 512), ("gk", 512), ("dv", 256), ("kr", 128), ("krr", 128), ("glow", 128),
              ("ik", 128), ("iw", 128), ("pad", 128))
_PM_OFF = {}
_o = 0
for _n, _w in _PM_LAYOUT:
    assert _o % _w == 0
    _PM_OFF[_n] = _o
    _o += _w
PM_WIDTH = _o

BF16 = jnp.bfloat16
F32 = jnp.float32
INT_MIN = -2 ** 31


def _cp(n_axes):
    return pltpu.CompilerParams(dimension_semantics=("arbitrary",) * n_axes,
                                vmem_limit_bytes=VMEM_LIMIT)


def _nt(a, b):
    return lax.dot_general(a, b, (((1,), (1,)), ((), ())), preferred_element_type=F32)


def _rms(x, g):
    return x * lax.rsqrt(jnp.mean(x * x, axis=-1, keepdims=True) + NORM_EPS) * g


def _mm_kernel(*refs, a_act, act, has_bias):
    if has_bias:
        a_ref, w_ref, b_ref, o_ref = refs
    else:
        a_ref, w_ref, o_ref = refs
    a = a_ref[...]
    if a_act == "silu":
        a = a.astype(F32)
        a = a * jax.nn.sigmoid(a)
    a = a.astype(BF16)
    for c0 in range(0, o_ref.shape[1], MM_CHUNK):
        cs = slice(c0, c0 + MM_CHUNK)
        r = jnp.dot(a, w_ref[:, cs].astype(BF16), preferred_element_type=F32)
        if has_bias:
            r = r + b_ref[:, cs]
        if act == "sigmoid":
            r = jax.nn.sigmoid(r)
        elif act == "silu":
            r = r * jax.nn.sigmoid(r)
        o_ref[:, cs] = r.astype(o_ref.dtype)


def _mm(a, w, bias, *, tm, tn, out_dtype, a_act=None, act=None, layer=None, name):
    m, k = a.shape
    n = w.shape[-1]
    if layer is None:
        w_spec = pl.BlockSpec((k, tn), lambda i, j: (0, j))
    else:
        w_spec = pl.BlockSpec((None, k, tn), lambda i, j: (layer, 0, j))
    in_specs = [pl.BlockSpec((tm, k), lambda i, j: (i, 0)), w_spec]
    args = [a, w]
    if bias is not None:
        in_specs.append(pl.BlockSpec((1, tn), lambda i, j: (0, j)))
        args.append(bias)
    return pl.pallas_call(
        functools.partial(_mm_kernel, a_act=a_act, act=act, has_bias=bias is not None),
        grid=(m // tm, n // tn),
        in_specs=in_specs,
        out_specs=pl.BlockSpec((tm, tn), lambda i, j: (i, j)),
        out_shape=jax.ShapeDtypeStruct((m, n), out_dtype),
        compiler_params=_cp(2),
        name=name,
    )(*args)


def _norm_mod_kernel(x_ref, g_ref, sc_ref, sh_ref, o_ref):
    y = _rms(x_ref[0], g_ref[...])
    o_ref[0] = (y * (1.0 + sc_ref[0]) + sh_ref[0]).astype(o_ref.dtype)


def _norm_mod(x, g, scale, shift, *, tm=512):
    b, s, d = x.shape
    return pl.pallas_call(
        _norm_mod_kernel,
        grid=(b, s // tm),
        in_specs=[pl.BlockSpec((1, tm, d), lambda bi, i: (bi, i, 0)),
                  pl.BlockSpec((1, d), lambda bi, i: (0, 0)),
                  pl.BlockSpec((1, 1, d), lambda bi, i: (bi, 0, 0)),
                  pl.BlockSpec((1, 1, d), lambda bi, i: (bi, 0, 0))],
        out_specs=pl.BlockSpec((1, tm, d), lambda bi, i: (bi, i, 0)),
        out_shape=jax.ShapeDtypeStruct((b, s, d), BF16),
        compiler_params=_cp(2),
        name="norm_mod",
    )(x, g, scale, shift)


def _final_norm_kernel(x_ref, g_ref, o_ref):
    o_ref[0] = _rms(x_ref[0], g_ref[...])


def _final_norm(x, g, *, tm=512):
    b, s, d = x.shape
    return pl.pallas_call(
        _final_norm_kernel,
        grid=(b, s // tm),
        in_specs=[pl.BlockSpec((1, tm, d), lambda bi, i: (bi, i, 0)),
                  pl.BlockSpec((1, d), lambda bi, i: (0, 0))],
        out_specs=pl.BlockSpec((1, tm, d), lambda bi, i: (bi, i, 0)),
        out_shape=jax.ShapeDtypeStruct((b, s, d), F32),
        compiler_params=_cp(2),
        name="final_norm",
    )(x, g)


def _rope_kernel(pos_ref, inv_ref, cos_ref, sin_ref):
    ang = pos_ref[0] * inv_ref[...]
    live = lax.broadcasted_iota(jnp.int32, ang.shape, 1) < MLA_ROPE
    cos_ref[0] = jnp.where(live, jnp.cos(ang), 0.0)
    sin_ref[0] = jnp.where(live, jnp.sin(ang), 0.0)


def _rope_tables(positions, *, tm=512):
    b, s = positions.shape
    inv = ROPE_THETA ** (-jnp.arange(0, MLA_ROPE, 2, dtype=F32) / MLA_ROPE)
    inv128 = jnp.concatenate([inv, inv, jnp.zeros((LANE - MLA_ROPE,), F32)])[None, :]
    pos128 = jnp.broadcast_to(positions.astype(F32)[:, :, None], (b, s, LANE))
    spec = pl.BlockSpec((1, tm, LANE), lambda bi, i: (bi, i, 0))
    return pl.pallas_call(
        _rope_kernel,
        grid=(b, s // tm),
        in_specs=[spec, pl.BlockSpec((1, LANE), lambda bi, i: (0, 0))],
        out_specs=[spec, spec],
        out_shape=[jax.ShapeDtypeStruct((b, s, LANE), F32)] * 2,
        compiler_params=_cp(2),
        name="rope_tables",
    )(pos128, inv128)


def _mla_q_kernel(cq_ref, g_ref, w1_ref, w2_ref, cos_ref, sin_ref, o_ref, an_ref, *, scale):
    @pl.when(pl.program_id(2) == 0)
    def _():
        an_ref[...] = _rms(cq_ref[0].astype(F32), g_ref[...]).astype(BF16)

    a = an_ref[...]
    a1 = jnp.dot(a, w1_ref[...], preferred_element_type=F32)
    a2 = jnp.dot(a, w2_ref[...], preferred_element_type=F32)
    cos, sin = cos_ref[0], sin_ref[0]
    for u in range(MLA_HEADS_PER_STEP):
        c0 = u * MLA_QK_PAD
        rope = a1[:, c0 + MLA_NOPE:c0 + MLA_QK_PAD] * cos + a2[:, u * LANE:(u + 1) * LANE] * sin
        o_ref[0, :, c0:c0 + MLA_NOPE] = (a1[:, c0:c0 + MLA_NOPE] * scale).astype(BF16)
        o_ref[0, :, c0 + MLA_NOPE:c0 + MLA_QK_PAD] = (rope * scale).astype(BF16)


def _mla_q(pm3, g_q, w1, w2, cos128, sin128, *, tm=1024):
    b, s, _ = pm3.shape
    hp = MLA_HEADS_PER_STEP
    cq_blk = _PM_OFF["cq"] // MLA_Q_RANK
    return pl.pallas_call(
        functools.partial(_mla_q_kernel, scale=(MLA_NOPE + MLA_ROPE) ** -0.5 * LOG2E),
        grid=(b, s // tm, MLA_HEADS // hp),
        in_specs=[pl.BlockSpec((1, tm, MLA_Q_RANK), lambda bi, i, h: (bi, i, cq_blk)),
                  pl.BlockSpec((1, MLA_Q_RANK), lambda bi, i, h: (0, 0)),
                  pl.BlockSpec((MLA_Q_RANK, hp * MLA_QK_PAD), lambda bi, i, h: (0, h)),
                  pl.BlockSpec((MLA_Q_RANK, hp * LANE), lambda bi, i, h: (0, h)),
                  pl.BlockSpec((1, tm, LANE), lambda bi, i, h: (bi, i, 0)),
                  pl.BlockSpec((1, tm, LANE), lambda bi, i, h: (bi, i, 0))],
        out_specs=pl.BlockSpec((1, tm, hp * MLA_QK_PAD), lambda bi, i, h: (bi, i, h)),
        out_shape=jax.ShapeDtypeStruct((b, s, MLA_HEADS * MLA_QK_PAD), BF16),
        scratch_shapes=[pltpu.VMEM((tm, MLA_Q_RANK), BF16)],
        compiler_params=_cp(3),
        name="mla_q",
    )(pm3, g_q, w1, w2, cos128, sin128)


def _mla_kv_kernel(ckv_ref, g_ref, w_ref, kr_ref, krr_ref, cos_ref, sin_ref, k_ref, v_ref, an_ref, krt_ref):
    @pl.when(pl.program_id(2) == 0)
    def _():
        an_ref[...] = _rms(ckv_ref[0].astype(F32), g_ref[...]).astype(BF16)
        kr = kr_ref[0].astype(F32) * cos_ref[0] + krr_ref[0].astype(F32) * sin_ref[0]
        krt_ref[...] = kr.T.astype(BF16)

    acc = jnp.dot(an_ref[...], w_ref[...], preferred_element_type=F32)
    ones = jnp.ones((acc.shape[0], MLA_V), BF16)
    for u in range(MLA_HEADS_PER_STEP):
        c0 = u * (MLA_NOPE + MLA_V)
        k_ref[0, u, 0, :MLA_NOPE, :] = acc[:, c0:c0 + MLA_NOPE].T.astype(BF16)
        k_ref[0, u, 0, MLA_NOPE:, :] = krt_ref[...]
        v_ref[0, u, 0, :, :MLA_V] = acc[:, c0 + MLA_NOPE:c0 + MLA_NOPE + MLA_V].astype(BF16)
        v_ref[0, u, 0, :, MLA_V:] = ones


def _mla_kv(pm3, g_kv, w_ukv, cos128, sin128):
    b, s, _ = pm3.shape
    tm = ATT_TK
    hp = MLA_HEADS_PER_STEP
    ckv_blk = _PM_OFF["ckv"] // MLA_KV_RANK
    kr_blk = _PM_OFF["kr"] // LANE
    krr_blk = _PM_OFF["krr"] // LANE
    return pl.pallas_call(
        _mla_kv_kernel,
        grid=(b, s // tm, MLA_HEADS // hp),
        in_specs=[pl.BlockSpec((1, tm, MLA_KV_RANK), lambda bi, i, h: (bi, i, ckv_blk)),
                  pl.BlockSpec((1, MLA_KV_RANK), lambda bi, i, h: (0, 0)),
                  pl.BlockSpec((MLA_KV_RANK, hp * (MLA_NOPE + MLA_V)), lambda bi, i, h: (0, h)),
                  pl.BlockSpec((1, tm, LANE), lambda bi, i, h: (bi, i, kr_blk)),
                  pl.BlockSpec((1, tm, LANE), lambda bi, i, h: (bi, i, krr_blk)),
                  pl.BlockSpec((1, tm, LANE), lambda bi, i, h: (bi, i, 0)),
                  pl.BlockSpec((1, tm, LANE), lambda bi, i, h: (bi, i, 0))],
        out_specs=[pl.BlockSpec((1, hp, 1, MLA_QK_PAD, tm), lambda bi, i, h: (bi, h, i, 0, 0)),
                   pl.BlockSpec((1, hp, 1, tm, 2 * MLA_V), lambda bi, i, h: (bi, h, i, 0, 0))],
        out_shape=[jax.ShapeDtypeStruct((b, MLA_HEADS, s // tm, MLA_QK_PAD, tm), BF16),
                   jax.ShapeDtypeStruct((b, MLA_HEADS, s // tm, tm, 2 * MLA_V), BF16)],
        scratch_shapes=[pltpu.VMEM((tm, MLA_KV_RANK), BF16), pltpu.VMEM((LANE, tm), BF16)],
        compiler_params=_cp(3),
        name="mla_kv",
    )(pm3, g_kv, w_ukv, pm3, pm3, cos128, sin128)


def _sm_init(acc_ref, m_ref):
    m_ref[...] = jnp.full(m_ref.shape, NEG, F32)
    acc_ref[...] = jnp.zeros(acc_ref.shape, F32)


def _sm_step(s, v_ext, acc_ref, m_ref):
    m_old = m_ref[...]
    m_new = jnp.maximum(m_old, jnp.max(s, axis=-1, keepdims=True))
    alpha = jnp.exp2(m_old - m_new)
    p = jnp.exp2(s - jnp.tile(m_new, (1, s.shape[1] // LANE)))
    pv = jnp.dot(p.astype(BF16), v_ext, preferred_element_type=F32)
    acc_ref[...] = jnp.tile(alpha, (1, acc_ref.shape[1] // LANE)) * acc_ref[...] + pv
    m_ref[...] = m_new


def _sm_result(acc_ref, d):
    acc = acc_ref[...]
    return acc[:, :d] / acc[:, d:]


def _mla_attn_kernel(q_ref, kt_ref, v_ref, z_ref, o_ref, s_ref, acc_ref, m_ref):
    i = pl.program_id(2)
    t = ATT_TK
    n_chains = ATT_TQ // t
    assert n_chains % 2 == 0
    chains = range(n_chains)
    qpos = lax.broadcasted_iota(jnp.int32, (t, t), 0)
    kpos = lax.broadcasted_iota(jnp.int32, (t, t), 1)

    def scores(u, j):
        return jnp.dot(q_ref[0, u * t:(u + 1) * t, :], kt_ref[0, 0, j], preferred_element_type=F32)

    def update(u, s, j, diagonal):
        if diagonal:
            s = jnp.where(kpos <= qpos, s, NEG)
        _sm_step(s, v_ref[0, 0, j], acc_ref.at[u], m_ref.at[u])

    def consume(u, buf, j, diagonal):
        update(u, s_ref[buf, u], j, diagonal)

    for u in chains:
        _sm_init(acc_ref.at[u], m_ref.at[u])
        s_ref[0, u] = scores(u, 0)

    def block_pair(p, carry):
        j = 2 * p
        for u in chains:
            consume(u, 0, j, False)
            s_ref[1, u] = scores(u, j + 1)
        for u in chains:
            consume(u, 1, j + 1, False)
            s_ref[0, u] = scores(u, j + 2)
        return carry

    first = n_chains * i
    lax.fori_loop(0, first // 2, block_pair, 0)
    for u in chains:
        consume(u, 0, first, u == 0)
        for d in range(1, u + 1):
            update(u, scores(u, first + d), first + d, d == u)
    for u in chains:
        rows = slice(u * t, (u + 1) * t)
        o_ref[0, rows, :] = (_sm_result(acc_ref.at[u], MLA_V) * z_ref[0, rows, :].astype(F32)).astype(o_ref.dtype)


def _mla_attn(q, kt, v_ext, pz3):
    b, s, _ = q.shape
    nk = s // ATT_TK
    return pl.pallas_call(
        _mla_attn_kernel,
        grid=(b, MLA_HEADS, s // ATT_TQ),
        in_specs=[pl.BlockSpec((1, ATT_TQ, MLA_QK_PAD), lambda bi, h, i: (bi, i, h)),
                  pl.BlockSpec((1, 1, nk, MLA_QK_PAD, ATT_TK), lambda bi, h, i: (bi, h, 0, 0, 0)),
                  pl.BlockSpec((1, 1, nk, ATT_TK, 2 * MLA_V), lambda bi, h, i: (bi, h, 0, 0, 0)),
                  pl.BlockSpec((1, ATT_TQ, MLA_V), lambda bi, h, i: (bi, i, h))],
        out_specs=pl.BlockSpec((1, ATT_TQ, MLA_V), lambda bi, h, i: (bi, i, h)),
        out_shape=jax.ShapeDtypeStruct((b, s, MLA_WIDTH), BF16),
        scratch_shapes=[pltpu.VMEM((2, ATT_TQ // ATT_TK, ATT_TK, ATT_TK), F32),
                        pltpu.VMEM((ATT_TQ // ATT_TK, ATT_TK, 2 * MLA_V), F32),
                        pltpu.VMEM((ATT_TQ // ATT_TK, ATT_TK, LANE), F32)],
        compiler_params=_cp(3),
        name="mla_attn",
    )(q, kt, v_ext, pz3)


def _log_sigmoid(x):
    return jnp.minimum(x, 0.0) - jnp.log(1.0 + jnp.exp(-jnp.abs(x)))


def _gla_kernel(q_ref, k_ref, v_ref, gl_ref, z_ref, w2_ref, bg_ref, go_ref, o_ref, st_ref, *, nb):
    @pl.when(pl.program_id(0) == 0)
    def _():
        st_ref[...] = jnp.zeros(st_ref.shape, F32)

    c = GLA_CHUNK
    row = lax.broadcasted_iota(jnp.int32, (c, c), 0)
    col = lax.broadcasted_iota(jnp.int32, (c, c), 1)
    tril = row >= col
    tril_b = jnp.where(tril, 1.0, 0.0).astype(BF16)
    qw = GLA_HEADS * GLA_DK
    for b in range(nb):
        pre = jnp.dot(gl_ref[b], w2_ref[...], preferred_element_type=F32) + bg_ref[...]
        glog = _log_sigmoid(pre) / GLA_GATE_NORM
        g1 = glog.astype(BF16)
        r1 = glog - g1.astype(F32)
        g2 = r1.astype(BF16)
        g3 = (r1 - g2.astype(F32)).astype(BF16)
        cs = jnp.dot(tril_b, jnp.concatenate([g1, g2, g3], axis=1), preferred_element_type=F32)
        bc_all = cs[:, :qw] + cs[:, qw:2 * qw] + cs[:, 2 * qw:]
        for h in range(GLA_HEADS):
            ks = slice(h * GLA_DK, (h + 1) * GLA_DK)
            vs = slice(h * GLA_DV, (h + 1) * GLA_DV)
            q = q_ref[b, :, ks].astype(F32)
            k = k_ref[b, :, ks].astype(F32)
            v = v_ref[b, :, vs]
            bc = bc_all[:, ks]
            b_last = bc[c - 1:c, :]
            q_dec = (q * GLA_DK ** -0.5 * jnp.exp(bc)).astype(BF16)
            k_inv = (k * jnp.exp(-bc)).astype(BF16)
            k_end = (k * jnp.exp(b_last - bc)).astype(BF16)
            decay = jnp.exp(b_last)
            attn = jnp.where(tril, _nt(q_dec, k_inv), 0.0).astype(BF16)
            st = st_ref[b * GLA_HEADS + h]
            o = jnp.dot(jnp.concatenate([q_dec, attn], axis=1),
                        jnp.concatenate([st.astype(BF16), v], axis=0), preferred_element_type=F32)
            upd = lax.dot_general(k_end, v, (((0,), (0,)), ((), ())), preferred_element_type=F32)
            decay_col = jnp.tile(jnp.broadcast_to(decay, (GLA_DK, GLA_DK)).T, (1, GLA_DV // GLA_DK))
            st_ref[b * GLA_HEADS + h] = st * decay_col + upd
            on = _rms(o, go_ref[...])
            o_ref[b, :, vs] = (on * z_ref[b, :, vs].astype(F32)).astype(o_ref.dtype)


def _gla(pm3, pz3, w2p, bg, gout):
    b, s, _ = pm3.shape
    c = GLA_CHUNK
    qw = GLA_HEADS * GLA_DK
    q_blk, k_blk = _PM_OFF["gq"] // qw, _PM_OFF["gk"] // qw
    v_blk, gl_blk = _PM_OFF["gv"] // GLA_WIDTH, _PM_OFF["glow"] // LANE
    z_blk = MLA_WIDTH // GLA_WIDTH
    return pl.pallas_call(
        functools.partial(_gla_kernel, nb=b),
        grid=(s // c,),
        in_specs=[pl.BlockSpec((b, c, qw), lambda i: (0, i, q_blk)),
                  pl.BlockSpec((b, c, qw), lambda i: (0, i, k_blk)),
                  pl.BlockSpec((b, c, GLA_WIDTH), lambda i: (0, i, v_blk)),
                  pl.BlockSpec((b, c, LANE), lambda i: (0, i, gl_blk)),
                  pl.BlockSpec((b, c, GLA_WIDTH), lambda i: (0, i, z_blk)),
                  pl.BlockSpec((LANE, qw), lambda i: (0, 0)),
                  pl.BlockSpec((1, qw), lambda i: (0, 0)),
                  pl.BlockSpec((1, GLA_DV), lambda i: (0, 0))],
        out_specs=pl.BlockSpec((b, c, GLA_WIDTH), lambda i: (0, i, 0)),
        out_shape=jax.ShapeDtypeStruct((b, s, GLA_WIDTH), BF16),
        scratch_shapes=[pltpu.VMEM((b * GLA_HEADS, GLA_DK, GLA_DV), F32)],
        compiler_params=_cp(1),
        name="gla",
    )(pm3, pm3, pm3, pm3, pz3, w2p, bg, gout)


def _dsa_kernel(iq_ref, iw_ref, onehot_ref, ikt_ref, q_ref, kt_ref, v_ref, z_ref, o_ref,
                key_ref, keyt_ref, iqs_ref, wb_ref, qs_ref, s_ref, acc_ref, m_ref, *, top_k):
    tq, tk = DSA_TQ, ATT_TK
    nk = key_ref.shape[0]
    lanes = tk // LANE
    i = pl.program_id(1)
    nkb = (i * tq + tq - 1) // tk + 1
    qpos = i * tq + lax.broadcasted_iota(jnp.int32, (tq, tk), 0)
    heads_per_dot = 4

    for h in range(IDX_HEADS):
        iqs_ref[h] = iq_ref[0, :, h * IDX_DH:(h + 1) * IDX_DH]

    wb = jnp.dot(iw_ref[0], onehot_ref[...], preferred_element_type=F32)
    for h in range(IDX_HEADS):
        wb_ref[h] = wb[:, h * LANE:(h + 1) * LANE] * (IDX_HEADS ** -0.5 * IDX_DH ** -0.5)

    def sortable(x):
        bits = lax.bitcast_convert_type(x, jnp.int32)
        return bits ^ ((bits >> 31) & 0x7FFFFFFF)

    qpos_t = i * tq + lax.broadcasted_iota(jnp.int32, (tk, tq), 1)

    def score_block(j, carry):
        ikt = ikt_ref[0, j]
        sc = jnp.zeros((tq, tk), F32)
        for hg in range(IDX_HEADS // heads_per_dot):
            iq = iqs_ref[hg * heads_per_dot:(hg + 1) * heads_per_dot].reshape(heads_per_dot * tq, IDX_DH)
            r = jnp.dot(iq, ikt, preferred_element_type=F32)
            for hh in range(heads_per_dot):
                wrow = jnp.tile(wb_ref[hg * heads_per_dot + hh], (1, lanes))
                sc = sc + jnp.maximum(r[hh * tq:(hh + 1) * tq, :], 0.0) * wrow
        kpos = j * tk + lax.broadcasted_iota(jnp.int32, (tq, tk), 1)
        key_ref[j] = jnp.where(kpos <= qpos, sortable(sc), INT_MIN)
        kpos_t = j * tk + lax.broadcasted_iota(jnp.int32, (tk, tq), 0)
        keyt_ref[j] = jnp.where(kpos_t <= qpos_t, sortable(sc.T), INT_MIN)
        return carry

    lax.fori_loop(0, nkb, score_block, 0)

    def count(pred):
        def body(j, acc):
            hit = jnp.where(pred(keyt_ref[j], j), 1, 0).astype(jnp.int32)
            return acc + hit.reshape(tk // 8, 8, tq).sum(axis=0)
        acc = lax.fori_loop(0, nkb, body, jnp.zeros((8, tq), jnp.int32))
        return acc.sum(axis=0, keepdims=True)

    def count_ge(cand):
        return count(lambda k, j: k >= cand)

    zero = jnp.zeros((1, tq), jnp.int32)
    c0 = count_ge(zero)
    thr = jnp.where(c0 >= top_k, zero, INT_MIN)
    n_ge = jnp.where(c0 >= top_k, c0, nkb * tk)

    def bit_step(it, carry):
        thr, n_ge = carry
        cand = thr | lax.shift_left(jnp.int32(1), 30 - it)
        c = count_ge(cand)
        return jnp.where(c >= top_k, cand, thr), jnp.where(c >= top_k, c, n_ge)

    thr, n_ge = lax.fori_loop(0, 31, bit_step, (thr, n_ge))

    overflow = (thr > INT_MIN) & (n_ge > top_k)

    @pl.when(jnp.max(jnp.where(overflow, 1, 0)) > 0)
    def _():
        n_take = top_k - count(lambda k, j: k > thr)
        rows = lax.broadcasted_iota(jnp.int32, (tk, tq), 0)
        bound = jnp.zeros((1, tq), jnp.int32)
        for bit in range((nk * tk - 1).bit_length() - 1, -1, -1):
            cand = bound | (1 << bit)
            below = count(lambda k, j: (k == thr) & (j * tk + rows < cand))
            bound = jnp.where(below < n_take, cand, bound)
        bound = jnp.where(overflow, bound, nk * tk)
        thr_c = jnp.tile(jnp.broadcast_to(thr, (tq, tq)).T, (1, lanes))
        bound_c = jnp.tile(jnp.broadcast_to(bound, (tq, tq)).T, (1, lanes))

        def demote(j, carry):
            k = key_ref[j]
            kpos = j * tk + lax.broadcasted_iota(jnp.int32, (tq, tk), 1)
            key_ref[j] = jnp.where((k == thr_c) & (kpos > bound_c), INT_MIN, k)
            return carry

        lax.fori_loop(0, nkb, demote, 0)

    thr = jnp.maximum(thr, INT_MIN + 1)
    thr_w = jnp.tile(jnp.broadcast_to(thr, (tq, tq)).T, (1, lanes))

    scale = DSA_DH ** -0.5 * LOG2E
    groups = range(DSA_KV_HEADS)

    def scores(g, j):
        return jnp.dot(qs_ref[g], kt_ref[0, g, j], preferred_element_type=F32)

    def consume(g, buf, j):
        sel = key_ref[j] >= thr_w
        s = s_ref[buf, g]
        s = jnp.concatenate([jnp.where(sel, s[r * tq:(r + 1) * tq, :], NEG) for r in range(DSA_REP)], axis=0)
        _sm_step(s, v_ref[0, g, j], acc_ref.at[g], m_ref.at[g])

    for g in groups:
        for r in range(DSA_REP):
            cs = slice((g * DSA_REP + r) * DSA_DH, (g * DSA_REP + r + 1) * DSA_DH)
            qs_ref[g, r * tq:(r + 1) * tq, :] = (q_ref[0, :, cs].astype(F32) * scale).astype(BF16)
        _sm_init(acc_ref.at[g], m_ref.at[g])
    for g in groups:
        s_ref[0, g] = scores(g, 0)

    def block_pair(p, carry):
        j = 2 * p
        for g in groups:
            consume(g, 0, j)
            s_ref[1, g] = scores(g, j + 1)
        for g in groups:
            consume(g, 1, j + 1)
            s_ref[0, g] = scores(g, jnp.minimum(j + 2, nk - 1))
        return carry

    lax.fori_loop(0, nkb // 2, block_pair, 0)

    @pl.when(nkb % 2 == 1)
    def _():
        for g in groups:
            consume(g, 0, nkb - 1)

    for g in groups:
        res = _sm_result(acc_ref.at[g], DSA_DH)
        for r in range(DSA_REP):
            cs = slice((g * DSA_REP + r) * DSA_DH, (g * DSA_REP + r + 1) * DSA_DH)
            o_ref[0, :, cs] = (res[r * tq:(r + 1) * tq, :] * z_ref[0, :, cs].astype(F32)).astype(o_ref.dtype)


def _dsa(pm3, pz3, ikt, dkt, dv_ext, top_k):
    b, s, _ = pm3.shape
    tq, tk = DSA_TQ, ATT_TK
    nk = s // tk
    q_blk = _PM_OFF["dq"] // DSA_WIDTH
    z_blk = (MLA_WIDTH + GLA_WIDTH) // DSA_WIDTH
    rows = DSA_REP * tq
    iq_blk = _PM_OFF["iq"] // (IDX_HEADS * IDX_DH)
    assert tq == LANE
    iw_blk = _PM_OFF["iw"] // LANE
    onehot = jnp.repeat(jnp.eye(LANE, IDX_HEADS, dtype=BF16), LANE, axis=1)
    return pl.pallas_call(
        functools.partial(_dsa_kernel, top_k=top_k),
        grid=(b, s // tq),
        in_specs=[pl.BlockSpec((1, tq, IDX_HEADS * IDX_DH), lambda bi, i: (bi, i, iq_blk)),
                  pl.BlockSpec((1, tq, LANE), lambda bi, i: (bi, i, iw_blk)),
                  pl.BlockSpec((LANE, IDX_HEADS * LANE), lambda bi, i: (0, 0)),
                  pl.BlockSpec((1, nk, IDX_DH, tk), lambda bi, i: (bi, 0, 0, 0)),
                  pl.BlockSpec((1, tq, DSA_WIDTH), lambda bi, i: (bi, i, q_blk)),
                  pl.BlockSpec((1, DSA_KV_HEADS, nk, DSA_DH, tk), lambda bi, i: (bi, 0, 0, 0, 0)),
                  pl.BlockSpec((1, DSA_KV_HEADS, nk, tk, 2 * DSA_DH), lambda bi, i: (bi, 0, 0, 0, 0)),
                  pl.BlockSpec((1, tq, DSA_WIDTH), lambda bi, i: (bi, i, z_blk))],
        out_specs=pl.BlockSpec((1, tq, DSA_WIDTH), lambda bi, i: (bi, i, 0)),
        out_shape=jax.ShapeDtypeStruct((b, s, DSA_WIDTH), BF16),
        scratch_shapes=[pltpu.VMEM((nk, tq, tk), jnp.int32),
                        pltpu.VMEM((nk, tk, tq), jnp.int32),
                        pltpu.VMEM((IDX_HEADS, tq, IDX_DH), BF16),
                        pltpu.VMEM((IDX_HEADS, tq, LANE), F32),
                        pltpu.VMEM((DSA_KV_HEADS, rows, DSA_DH), BF16),
                        pltpu.VMEM((2, DSA_KV_HEADS, rows, tk), F32),
                        pltpu.VMEM((DSA_KV_HEADS, rows, 2 * DSA_DH), F32),
                        pltpu.VMEM((DSA_KV_HEADS, rows, LANE), F32)],
        compiler_params=_cp(2),
        name="dsa",
    )(pm3, pm3, onehot, ikt, pm3, dkt, dv_ext, pz3)


def _dsa_layout_kernel(ik_ref, dk_ref, dv_ref, ikt_ref, kt_ref, vx_ref):
    ikt_ref[0, 0] = ik_ref[0].astype(F32).T[:IDX_DH, :].astype(BF16)
    ones = jnp.ones((ATT_TK, DSA_DH), BF16)
    for g in range(DSA_KV_HEADS):
        cs = slice(g * DSA_DH, (g + 1) * DSA_DH)
        kt_ref[0, g, 0] = dk_ref[0, :, cs].astype(F32).T.astype(BF16)
        vx_ref[0, g, 0, :, :DSA_DH] = dv_ref[0, :, cs]
        vx_ref[0, g, 0, :, DSA_DH:] = ones


def _dsa_layout(pm3):
    b, s, _ = pm3.shape
    tk = ATT_TK
    nk = s // tk
    kvw = DSA_KV_HEADS * DSA_DH
    ik_blk, dk_blk, dv_blk = _PM_OFF["ik"] // LANE, _PM_OFF["dk"] // kvw, _PM_OFF["dv"] // kvw
    return pl.pallas_call(
        _dsa_layout_kernel,
        grid=(b, nk),
        in_specs=[pl.BlockSpec((1, tk, LANE), lambda bi, j: (bi, j, ik_blk)),
                  pl.BlockSpec((1, tk, kvw), lambda bi, j: (bi, j, dk_blk)),
                  pl.BlockSpec((1, tk, kvw), lambda bi, j: (bi, j, dv_blk))],
        out_specs=[pl.BlockSpec((1, 1, IDX_DH, tk), lambda bi, j: (bi, j, 0, 0)),
                   pl.BlockSpec((1, DSA_KV_HEADS, 1, DSA_DH, tk), lambda bi, j: (bi, 0, j, 0, 0)),
                   pl.BlockSpec((1, DSA_KV_HEADS, 1, tk, 2 * DSA_DH), lambda bi, j: (bi, 0, j, 0, 0))],
        out_shape=[jax.ShapeDtypeStruct((b, nk, IDX_DH, tk), BF16),
                   jax.ShapeDtypeStruct((b, DSA_KV_HEADS, nk, DSA_DH, tk), BF16),
                   jax.ShapeDtypeStruct((b, DSA_KV_HEADS, nk, tk, 2 * DSA_DH), BF16)],
        compiler_params=_cp(2),
        name="dsa_layout",
    )(pm3, pm3, pm3)


def _lift_kernel(ym_ref, yg_ref, yd_ref, wm_ref, wg_ref, wd_ref, ga_ref, gb_ref, gc_ref, o_ref):
    a = jnp.dot(ym_ref[...], wm_ref[...], preferred_element_type=F32)
    b = jnp.dot(yg_ref[...], wg_ref[...], preferred_element_type=F32)
    c = jnp.dot(yd_ref[...], wd_ref[...], preferred_element_type=F32)
    o = ga_ref[...].astype(F32) * a + gb_ref[...].astype(F32) * b + gc_ref[...].astype(F32) * c
    o_ref[...] = o.astype(o_ref.dtype)


def _lift(y_mla, y_gla, y_dsa, w_bm, w_bg, w_bd, gates, *, tm=512, tn=1024):
    t = y_mla.shape[0]
    d = w_bm.shape[1]
    nj = d // tn
    row = lambda width: pl.BlockSpec((tm, width), lambda i, j: (i, 0))
    wcol = lambda depth: pl.BlockSpec((depth, tn), lambda i, j: (0, j))
    gate = lambda br: pl.BlockSpec((tm, tn), lambda i, j: (i, br * nj + j))
    return pl.pallas_call(
        _lift_kernel,
        grid=(t // tm, nj),
        in_specs=[row(MLA_WIDTH), row(GLA_WIDTH), row(DSA_WIDTH),
                  wcol(MLA_WIDTH), wcol(GLA_WIDTH), wcol(DSA_WIDTH),
                  gate(0), gate(1), gate(2)],
        out_specs=pl.BlockSpec((tm, tn), lambda i, j: (i, j)),
        out_shape=jax.ShapeDtypeStruct((t, d), BF16),
        compiler_params=_cp(2),
        name="lift_merge",
    )(y_mla, y_gla, y_dsa, w_bm, w_bg, w_bd, gates, gates, gates)


def _out_kernel(m_ref, w_ref, x_ref, gate_ref, o_ref):
    r = jnp.dot(m_ref[0], w_ref[...], preferred_element_type=F32)
    o_ref[0] = x_ref[0] + gate_ref[0] * r


def _out_proj(merged3, w_o, x, gate, *, tm=1024, tn=1024):
    b, s, d = x.shape
    tm = min(tm, s)
    return pl.pallas_call(
        _out_kernel,
        grid=(b, s // tm, d // tn),
        in_specs=[pl.BlockSpec((1, tm, d), lambda bi, i, j: (bi, i, 0)),
                  pl.BlockSpec((d, tn), lambda bi, i, j: (0, j)),
                  pl.BlockSpec((1, tm, tn), lambda bi, i, j: (bi, i, j)),
                  pl.BlockSpec((1, 1, tn), lambda bi, i, j: (bi, 0, j))],
        out_specs=pl.BlockSpec((1, tm, tn), lambda bi, i, j: (bi, i, j)),
        out_shape=jax.ShapeDtypeStruct((b, s, d), F32),
        compiler_params=_cp(3),
        name="out_proj",
    )(merged3, w_o, x, gate)


def _win_layout_kernel(w_ref, om_ref, oz_ref):
    half = MLA_ROPE // 2
    for name, width in _PM_LAYOUT:
        dst = _PM_OFF[name]
        if name == "pad":
            om_ref[:, dst:dst + width] = jnp.zeros((om_ref.shape[0], width), BF16)
        elif name in ("kr", "krr", "glow", "ik", "iw"):
            src_name = {"kr": "krope", "krr": "krope"}.get(name, name)
            src, w = _IN_OFF[src_name], _IN_W[src_name]
            om_ref[:, dst:dst + width] = jnp.zeros((om_ref.shape[0], width), BF16)
            if name == "krr":
                om_ref[:, dst:dst + half] = (-w_ref[:, src + half:src + w]).astype(BF16)
                om_ref[:, dst + half:dst + w] = w_ref[:, src:src + half].astype(BF16)
            else:
                om_ref[:, dst:dst + w] = w_ref[:, src:src + w].astype(BF16)
        else:
            src = _IN_OFF[name]
            om_ref[:, dst:dst + width] = w_ref[:, src:src + width].astype(BF16)
    oz_ref[...] = w_ref[:, _IN_OFF["z_mla"]:].astype(BF16)


def _win_layout(w_in, layer, *, tr=128):
    _, d, n = w_in.shape
    nz = n - _IN_OFF["z_mla"]
    return pl.pallas_call(
        _win_layout_kernel,
        grid=(d // tr,),
        in_specs=[pl.BlockSpec((None, tr, n), lambda i: (layer, i, 0))],
        out_specs=[pl.BlockSpec((tr, PM_WIDTH), lambda i: (i, 0)),
                   pl.BlockSpec((tr, nz), lambda i: (i, 0))],
        out_shape=[jax.ShapeDtypeStruct((d, PM_WIDTH), BF16), jax.ShapeDtypeStruct((d, nz), BF16)],
        compiler_params=_cp(1),
        name="win_layout",
    )(w_in)


def _mla_q_weights(w_uq):
    r = w_uq.shape[0]
    w = w_uq.reshape(r, MLA_HEADS, MLA_NOPE + MLA_ROPE)
    nope, rope = w[..., :MLA_NOPE], w[..., MLA_NOPE:]
    half = MLA_ROPE // 2
    zeros = jnp.zeros((r, MLA_HEADS, LANE - MLA_ROPE), w.dtype)
    w1 = jnp.concatenate([nope, rope, zeros], axis=-1).reshape(r, MLA_HEADS * MLA_QK_PAD)
    rot = jnp.concatenate([-rope[..., half:], rope[..., :half]], axis=-1)
    w2 = jnp.concatenate([rot, zeros], axis=-1).reshape(r, MLA_HEADS * LANE)
    return w1.astype(BF16), w2.astype(BF16)


def _layer(layer, x, c_pad, cos128, sin128, norm_g, w_ada, b_ada, w_in, mla_gq, mla_wuq, mla_gkv, mla_wukv,
           gla_wg2, gla_bg, gla_gout, w_mg, b_mg, w_bm, w_bg, w_bd, w_o):
    b, s, d = x.shape
    t = b * s
    mod = _mm(c_pad, w_ada, b_ada[None, :], tm=c_pad.shape[0], tn=512, out_dtype=F32,
              a_act="silu", layer=layer, name="ada")[:b]
    shift, scale, gate = (mod[:, None, k * d:(k + 1) * d] for k in range(3))
    h = _norm_mod(x, norm_g[None, :], scale, shift)
    h2 = h.reshape(t, d)

    mm_tm = min(1024, t)
    w_main, w_z = _win_layout(w_in, layer)
    pm = _mm(h2, w_main, None, tm=mm_tm, tn=1536, out_dtype=BF16, name="proj_main")
    pz = _mm(h2, w_z, None, tm=mm_tm, tn=1024, out_dtype=BF16, act="silu", name="proj_gate_paths")
    gates = _mm(h2, w_mg, b_mg[None, :], tm=mm_tm, tn=512, out_dtype=BF16,
                act="sigmoid", layer=layer, name="merge_gates")
    pm3 = pm.reshape(b, s, PM_WIDTH)
    pz3 = pz.reshape(b, s, d)

    w1, w2 = _mla_q_weights(mla_wuq)
    q = _mla_q(pm3, mla_gq[None, :], w1, w2, cos128, sin128)
    kcat, vt = _mla_kv(pm3, mla_gkv[None, :], mla_wukv.astype(BF16), cos128, sin128)
    y_mla = _mla_attn(q, kcat, vt, pz3)

    w2p = jnp.concatenate([gla_wg2, jnp.zeros((LANE - GLA_GATE_RANK, gla_wg2.shape[1]), gla_wg2.dtype)],
                          axis=0).astype(BF16)
    y_gla = _gla(pm3, pz3, w2p, gla_bg[None, :], gla_gout[None, :])

    ikt, dkt, dv_ext = _dsa_layout(pm3)
    y_dsa = _dsa(pm3, pz3, ikt, dkt, dv_ext, min(IDX_TOPK, s // 4))

    merged = _lift(y_mla.reshape(t, MLA_WIDTH), y_gla.reshape(t, GLA_WIDTH), y_dsa.reshape(t, DSA_WIDTH),
                   w_bm.astype(BF16), w_bg.astype(BF16), w_bd.astype(BF16), gates)
    return _out_proj(merged.reshape(b, s, d), w_o.astype(BF16), x, gate)


def kernel(x, c, positions, norm_g, w_ada, b_ada, w_in, mla_gq, mla_wuq, mla_gkv, mla_wukv, gla_wg2,
           gla_bg, gla_gout, w_mg, b_mg, w_bm, w_bg, w_bd, w_o, final_g):
    b = x.shape[0]
    cos128, sin128 = _rope_tables(positions)
    c_pad = jnp.concatenate([c, jnp.zeros((8 - b, c.shape[1]), c.dtype)], axis=0)
    for l in range(DEPTH):
        x = _layer(l, x, c_pad, cos128, sin128, norm_g[l], w_ada, b_ada[l], w_in, mla_gq[l],
                   mla_wuq[l], mla_gkv[l], mla_wukv[l], gla_wg2[l], gla_bg[l], gla_gout[l],
                   w_mg, b_mg[l], w_bm[l], w_bg[l], w_bd[l], w_o[l])
    return _final_norm(x, final_g[None, :])
```

```python
import functools

import jax
import jax.numpy as jnp
import numpy as np
from jax import lax
from jax.experimental import pallas as pl
from jax.experimental.pallas import tpu as pltpu

D_MODEL = 4096
DEPTH = 2
MLA_HEADS = 16
MLA_Q_RANK = 768
MLA_KV_RANK = 512
MLA_NOPE = 128
MLA_ROPE = 64
MLA_V = 128
ROPE_THETA = 10000.0
GLA_HEADS = 4
GLA_DK = 128
GLA_DV = 256
GLA_GATE_RANK = 16
GLA_GATE_NORM = 16.0
GLA_CHUNK = 64
GLA_CHUNKS_PER_STEP = 2
DSA_HEADS = 8
DSA_KV_HEADS = 2
DSA_DH = 128
IDX_HEADS = 32
IDX_DH = 64
IDX_TOPK = 256
NORM_EPS = 1e-6
NEG = -1e30

MLA_WIDTH = MLA_HEADS * MLA_V
GLA_WIDTH = GLA_HEADS * GLA_DV
DSA_WIDTH = DSA_HEADS * DSA_DH
DSA_REP = DSA_HEADS // DSA_KV_HEADS

IN_SPLITS = (
    MLA_Q_RANK, MLA_KV_RANK, MLA_ROPE,
    GLA_HEADS * GLA_DK, GLA_HEADS * GLA_DK, GLA_WIDTH, GLA_GATE_RANK,
    DSA_WIDTH, DSA_KV_HEADS * DSA_DH, DSA_KV_HEADS * DSA_DH,
    IDX_HEADS * IDX_DH, IDX_DH, IDX_HEADS,
    MLA_WIDTH, GLA_WIDTH, DSA_WIDTH,
)
_IN_NAMES = ("cq", "ckv", "krope", "gq", "gk", "gv", "glow", "dq", "dk", "dv",
             "iq", "ik", "iw", "z_mla", "z_gla", "z_dsa")
_IN_OFF = dict(zip(_IN_NAMES, np.concatenate([[0], np.cumsum(IN_SPLITS)[:-1]]).tolist()))
_IN_W = dict(zip(_IN_NAMES, IN_SPLITS))

LANE = 128
MLA_QK_PAD = 256
ATT_TQ = 2048
ATT_TK = 512
MLA_HEADS_PER_STEP = 4
LOG2E = 1.4426950408889634
DSA_TQ = 128
MM_CHUNK = 256
VMEM_LIMIT = 56 * 1024 * 1024

_PM_LAYOUT = (("dq", 1024), ("gv", 1024), ("iq", 2048), ("gq", 512), ("cq", 768), ("dk", 256),
              ("ckv", 512), ("gk", 512), ("dv", 256), ("kr", 128), ("krr", 128), ("glow", 128),
              ("ik", 128), ("iw", 128), ("pad", 128))
_PM_OFF = {}
_o = 0
for _n, _w in _PM_LAYOUT:
    assert _o % _w == 0
    _PM_OFF[_n] = _o
    _o += _w
PM_WIDTH = _o

BF16 = jnp.bfloat16
F32 = jnp.float32
INT_MIN = -2 ** 31


def _cp(n_axes):
    return pltpu.CompilerParams(dimension_semantics=("arbitrary",) * n_axes,
                                vmem_limit_bytes=VMEM_LIMIT)


def _nt(a, b):
    return lax.dot_general(a, b, (((1,), (1,)), ((), ())), preferred_element_type=F32)


def _rms(x, g):
    return x * lax.rsqrt(jnp.mean(x * x, axis=-1, keepdims=True) + NORM_EPS) * g


def _mm_kernel(*refs, a_act, act, has_bias):
    if has_bias:
        a_ref, w_ref, b_ref, o_ref = refs
    else:
        a_ref, w_ref, o_ref = refs
    a = a_ref[...]
    if a_act == "silu":
        a = a.astype(F32)
        a = a * jax.nn.sigmoid(a)
    a = a.astype(BF16)
    for c0 in range(0, o_ref.shape[1], MM_CHUNK):
        cs = slice(c0, c0 + MM_CHUNK)
        r = jnp.dot(a, w_ref[:, cs].astype(BF16), preferred_element_type=F32)
        if has_bias:
            r = r + b_ref[:, cs]
        if act == "sigmoid":
            r = jax.nn.sigmoid(r)
        elif act == "silu":
            r = r * jax.nn.sigmoid(r)
        o_ref[:, cs] = r.astype(o_ref.dtype)


def _mm(a, w, bias, *, tm, tn, out_dtype, a_act=None, act=None, layer=None, name):
    m, k = a.shape
    n = w.shape[-1]
    if layer is None:
        w_spec = pl.BlockSpec((k, tn), lambda i, j: (0, j))
    else:
        w_spec = pl.BlockSpec((None, k, tn), lambda i, j: (layer, 0, j))
    in_specs = [pl.BlockSpec((tm, k), lambda i, j: (i, 0)), w_spec]
    args = [a, w]
    if bias is not None:
        in_specs.append(pl.BlockSpec((1, tn), lambda i, j: (0, j)))
        args.append(bias)
    return pl.pallas_call(
        functools.partial(_mm_kernel, a_act=a_act, act=act, has_bias=bias is not None),
        grid=(m // tm, n // tn),
        in_specs=in_specs,
        out_specs=pl.BlockSpec((tm, tn), lambda i, j: (i, j)),
        out_shape=jax.ShapeDtypeStruct((m, n), out_dtype),
        compiler_params=_cp(2),
        name=name,
    )(*args)


def _norm_mod_kernel(x_ref, g_ref, sc_ref, sh_ref, o_ref):
    y = _rms(x_ref[0], g_ref[...])
    o_ref[0] = (y * (1.0 + sc_ref[0]) + sh_ref[0]).astype(o_ref.dtype)


def _norm_mod(x, g, scale, shift, *, tm=512):
    b, s, d = x.shape
    return pl.pallas_call(
        _norm_mod_kernel,
        grid=(b, s // tm),
        in_specs=[pl.BlockSpec((1, tm, d), lambda bi, i: (bi, i, 0)),
                  pl.BlockSpec((1, d), lambda bi, i: (0, 0)),
                  pl.BlockSpec((1, 1, d), lambda bi, i: (bi, 0, 0)),
                  pl.BlockSpec((1, 1, d), lambda bi, i: (bi, 0, 0))],
        out_specs=pl.BlockSpec((1, tm, d), lambda bi, i: (bi, i, 0)),
        out_shape=jax.ShapeDtypeStruct((b, s, d), BF16),
        compiler_params=_cp(2),
        name="norm_mod",
    )(x, g, scale, shift)


def _final_norm_kernel(x_ref, g_ref, o_ref):
    o_ref[0] = _rms(x_ref[0], g_ref[...])


def _final_norm(x, g, *, tm=512):
    b, s, d = x.shape
    return pl.pallas_call(
        _final_norm_kernel,
        grid=(b, s // tm),
        in_specs=[pl.BlockSpec((1, tm, d), lambda bi, i: (bi, i, 0)),
                  pl.BlockSpec((1, d), lambda bi, i: (0, 0))],
        out_specs=pl.BlockSpec((1, tm, d), lambda bi, i: (bi, i, 0)),
        out_shape=jax.ShapeDtypeStruct((b, s, d), F32),
        compiler_params=_cp(2),
        name="final_norm",
    )(x, g)


def _rope_kernel(pos_ref, inv_ref, cos_ref, sin_ref):
    ang = pos_ref[0] * inv_ref[...]
    live = lax.broadcasted_iota(jnp.int32, ang.shape, 1) < MLA_ROPE
    cos_ref[0] = jnp.where(live, jnp.cos(ang), 0.0)
    sin_ref[0] = jnp.where(live, jnp.sin(ang), 0.0)


def _rope_tables(positions, *, tm=512):
    b, s = positions.shape
    inv = ROPE_THETA ** (-jnp.arange(0, MLA_ROPE, 2, dtype=F32) / MLA_ROPE)
    inv128 = jnp.concatenate([inv, inv, jnp.zeros((LANE - MLA_ROPE,), F32)])[None, :]
    pos128 = jnp.broadcast_to(positions.astype(F32)[:, :, None], (b, s, LANE))
    spec = pl.BlockSpec((1, tm, LANE), lambda bi, i: (bi, i, 0))
    return pl.pallas_call(
        _rope_kernel,
        grid=(b, s // tm),
        in_specs=[spec, pl.BlockSpec((1, LANE), lambda bi, i: (0, 0))],
        out_specs=[spec, spec],
        out_shape=[jax.ShapeDtypeStruct((b, s, LANE), F32)] * 2,
        compiler_params=_cp(2),
        name="rope_tables",
    )(pos128, inv128)


def _mla_q_kernel(cq_ref, g_ref, w1_ref, w2_ref, cos_ref, sin_ref, o_ref, an_ref, *, scale):
    @pl.when(pl.program_id(2) == 0)
    def _():
        an_ref[...] = _rms(cq_ref[0].astype(F32), g_ref[...]).astype(BF16)

    a = an_ref[...]
    a1 = jnp.dot(a, w1_ref[...], preferred_element_type=F32)
    a2 = jnp.dot(a, w2_ref[...], preferred_element_type=F32)
    cos, sin = cos_ref[0], sin_ref[0]
    for u in range(MLA_HEADS_PER_STEP):
        c0 = u * MLA_QK_PAD
        rope = a1[:, c0 + MLA_NOPE:c0 + MLA_QK_PAD] * cos + a2[:, u * LANE:(u + 1) * LANE] * sin
        o_ref[0, :, c0:c0 + MLA_NOPE] = (a1[:, c0:c0 + MLA_NOPE] * scale).astype(BF16)
        o_ref[0, :, c0 + MLA_NOPE:c0 + MLA_QK_PAD] = (rope * scale).astype(BF16)


def _mla_q(pm3, g_q, w1, w2, cos128, sin128, *, tm=1024):
    b, s, _ = pm3.shape
    hp = MLA_HEADS_PER_STEP
    cq_blk = _PM_OFF["cq"] // MLA_Q_RANK
    return pl.pallas_call(
        functools.partial(_mla_q_kernel, scale=(MLA_NOPE + MLA_ROPE) ** -0.5 * LOG2E),
        grid=(b, s // tm, MLA_HEADS // hp),
        in_specs=[pl.BlockSpec((1, tm, MLA_Q_RANK), lambda bi, i, h: (bi, i, cq_blk)),
                  pl.BlockSpec((1, MLA_Q_RANK), lambda bi, i, h: (0, 0)),
                  pl.BlockSpec((MLA_Q_RANK, hp * MLA_QK_PAD), lambda bi, i, h: (0, h)),
                  pl.BlockSpec((MLA_Q_RANK, hp * LANE), lambda bi, i, h: (0, h)),
                  pl.BlockSpec((1, tm, LANE), lambda bi, i, h: (bi, i, 0)),
                  pl.BlockSpec((1, tm, LANE), lambda bi, i, h: (bi, i, 0))],
        out_specs=pl.BlockSpec((1, tm, hp * MLA_QK_PAD), lambda bi, i, h: (bi, i, h)),
        out_shape=jax.ShapeDtypeStruct((b, s, MLA_HEADS * MLA_QK_PAD), BF16),
        scratch_shapes=[pltpu.VMEM((tm, MLA_Q_RANK), BF16)],
        compiler_params=_cp(3),
        name="mla_q",
    )(pm3, g_q, w1, w2, cos128, sin128)


def _mla_kv_kernel(ckv_ref, g_ref, w_ref, kr_ref, krr_ref, cos_ref, sin_ref, k_ref, v_ref, an_ref, krt_ref):
    @pl.when(pl.program_id(2) == 0)
    def _():
        an_ref[...] = _rms(ckv_ref[0].astype(F32), g_ref[...]).astype(BF16)
        kr = kr_ref[0].astype(F32) * cos_ref[0] + krr_ref[0].astype(F32) * sin_ref[0]
        krt_ref[...] = kr.T.astype(BF16)

    acc = jnp.dot(an_ref[...], w_ref[...], preferred_element_type=F32)
    ones = jnp.ones((acc.shape[0], MLA_V), BF16)
    for u in range(MLA_HEADS_PER_STEP):
        c0 = u * (MLA_NOPE + MLA_V)
        k_ref[0, u, 0, :MLA_NOPE, :] = acc[:, c0:c0 + MLA_NOPE].T.astype(BF16)
        k_ref[0, u, 0, MLA_NOPE:, :] = krt_ref[...]
        v_ref[0, u, 0, :, :MLA_V] = acc[:, c0 + MLA_NOPE:c0 + MLA_NOPE + MLA_V].astype(BF16)
        v_ref[0, u, 0, :, MLA_V:] = ones


def _mla_kv(pm3, g_kv, w_ukv, cos128, sin128):
    b, s, _ = pm3.shape
    tm = ATT_TK
    hp = MLA_HEADS_PER_STEP
    ckv_blk = _PM_OFF["ckv"] // MLA_KV_RANK
    kr_blk = _PM_OFF["kr"] // LANE
    krr_blk = _PM_OFF["krr"] // LANE
    return pl.pallas_call(
        _mla_kv_kernel,
        grid=(b, s // tm, MLA_HEADS // hp),
        in_specs=[pl.BlockSpec((1, tm, MLA_KV_RANK), lambda bi, i, h: (bi, i, ckv_blk)),
                  pl.BlockSpec((1, MLA_KV_RANK), lambda bi, i, h: (0, 0)),
                  pl.BlockSpec((MLA_KV_RANK, hp * (MLA_NOPE + MLA_V)), lambda bi, i, h: (0, h)),
                  pl.BlockSpec((1, tm, LANE), lambda bi, i, h: (bi, i, kr_blk)),
                  pl.BlockSpec((1, tm, LANE), lambda bi, i, h: (bi, i, krr_blk)),
                  pl.BlockSpec((1, tm, LANE), lambda bi, i, h: (bi, i, 0)),
                  pl.BlockSpec((1, tm, LANE), lambda bi, i, h: (bi, i, 0))],
        out_specs=[pl.BlockSpec((1, hp, 1, MLA_QK_PAD, tm), lambda bi, i, h: (bi, h, i, 0, 0)),
                   pl.BlockSpec((1, hp, 1, tm, 2 * MLA_V), lambda bi, i, h: (bi, h, i, 0, 0))],
        out_shape=[jax.ShapeDtypeStruct((b, MLA_HEADS, s // tm, MLA_QK_PAD, tm), BF16),
                   jax.ShapeDtypeStruct((b, MLA_HEADS, s // tm, tm, 2 * MLA_V), BF16)],
        scratch_shapes=[pltpu.VMEM((tm, MLA_KV_RANK), BF16), pltpu.VMEM((LANE, tm), BF16)],
        compiler_params=_cp(3),
        name="mla_kv",
    )(pm3, g_kv, w_ukv, pm3, pm3, cos128, sin128)


def _sm_init(acc_ref, m_ref):
    m_ref[...] = jnp.full(m_ref.shape, NEG, F32)
    acc_ref[...] = jnp.zeros(acc_ref.shape, F32)


def _sm_step(s, v_ext, acc_ref, m_ref):
    m_old = m_ref[...]
    m_new = jnp.maximum(m_old, jnp.max(s, axis=-1, keepdims=True))
    alpha = jnp.exp2(m_old - m_new)
    p = jnp.exp2(s - jnp.tile(m_new, (1, s.shape[1] // LANE)))
    pv = jnp.dot(p.astype(BF16), v_ext, preferred_element_type=F32)
    acc_ref[...] = jnp.tile(alpha, (1, acc_ref.shape[1] // LANE)) * acc_ref[...] + pv
    m_ref[...] = m_new


def _sm_result(acc_ref, d):
    acc = acc_ref[...]
    return acc[:, :d] / acc[:, d:]


def _mla_attn_kernel(q_ref, kt_ref, v_ref, z_ref, o_ref, s_ref, acc_ref, m_ref):
    i = pl.program_id(2)
    t = ATT_TK
    n_chains = ATT_TQ // t
    assert n_chains % 2 == 0
    chains = range(n_chains)
    qpos = lax.broadcasted_iota(jnp.int32, (t, t), 0)
    kpos = lax.broadcasted_iota(jnp.int32, (t, t), 1)

    def scores(u, j):
        return jnp.dot(q_ref[0, u * t:(u + 1) * t, :], kt_ref[0, 0, j], preferred_element_type=F32)

    def update(u, s, j, diagonal):
        if diagonal:
            s = jnp.where(kpos <= qpos, s, NEG)
        _sm_step(s, v_ref[0, 0, j], acc_ref.at[u], m_ref.at[u])

    def consume(u, buf, j, diagonal):
        update(u, s_ref[buf, u], j, diagonal)

    for u in chains:
        _sm_init(acc_ref.at[u], m_ref.at[u])
        s_ref[0, u] = scores(u, 0)

    def block_pair(p, carry):
        j = 2 * p
        for u in chains:
            consume(u, 0, j, False)
            s_ref[1, u] = scores(u, j + 1)
        for u in chains:
            consume(u, 1, j + 1, False)
            s_ref[0, u] = scores(u, j + 2)
        return carry

    first = n_chains * i
    lax.fori_loop(0, first // 2, block_pair, 0)
    for u in chains:
        consume(u, 0, first, u == 0)
        for d in range(1, u + 1):
            update(u, scores(u, first + d), first + d, d == u)
    for u in chains:
        rows = slice(u * t, (u + 1) * t)
        o_ref[0, rows, :] = (_sm_result(acc_ref.at[u], MLA_V) * z_ref[0, rows, :].astype(F32)).astype(o_ref.dtype)


def _mla_attn(q, kt, v_ext, pz3):
    b, s, _ = q.shape
    nk = s // ATT_TK
    return pl.pallas_call(
        _mla_attn_kernel,
        grid=(b, MLA_HEADS, s // ATT_TQ),
        in_specs=[pl.BlockSpec((1, ATT_TQ, MLA_QK_PAD), lambda bi, h, i: (bi, i, h)),
                  pl.BlockSpec((1, 1, nk, MLA_QK_PAD, ATT_TK), lambda bi, h, i: (bi, h, 0, 0, 0)),
                  pl.BlockSpec((1, 1, nk, ATT_TK, 2 * MLA_V), lambda bi, h, i: (bi, h, 0, 0, 0)),
                  pl.BlockSpec((1, ATT_TQ, MLA_V), lambda bi, h, i: (bi, i, h))],
        out_specs=pl.BlockSpec((1, ATT_TQ, MLA_V), lambda bi, h, i: (bi, i, h)),
        out_shape=jax.ShapeDtypeStruct((b, s, MLA_WIDTH), BF16),
        scratch_shapes=[pltpu.VMEM((2, ATT_TQ // ATT_TK, ATT_TK, ATT_TK), F32),
                        pltpu.VMEM((ATT_TQ // ATT_TK, ATT_TK, 2 * MLA_V), F32),
                        pltpu.VMEM((ATT_TQ // ATT_TK, ATT_TK, LANE), F32)],
        compiler_params=_cp(3),
        name="mla_attn",
    )(q, kt, v_ext, pz3)


def _log_sigmoid(x):
    return jnp.minimum(x, 0.0) - jnp.log(1.0 + jnp.exp(-jnp.abs(x)))


def _gla_kernel(q_ref, k_ref, v_ref, gl_ref, z_ref, w2_ref, bg_ref, go_ref, o_ref, st_ref, *, nb):
    @pl.when(pl.program_id(0) == 0)
    def _():
        st_ref[...] = jnp.zeros(st_ref.shape, F32)

    c = GLA_CHUNK
    row = lax.broadcasted_iota(jnp.int32, (c, c), 0)
    col = lax.broadcasted_iota(jnp.int32, (c, c), 1)
    tril = row >= col
    tril_b = jnp.where(tril, 1.0, 0.0).astype(BF16)
    qw = GLA_HEADS * GLA_DK
    for b, r0 in [(b, n * c) for b in range(nb) for n in range(GLA_CHUNKS_PER_STEP)]:
        rs = slice(r0, r0 + c)
        pre = jnp.dot(gl_ref[b, rs, :], w2_ref[...], preferred_element_type=F32) + bg_ref[...]
        glog = _log_sigmoid(pre) / GLA_GATE_NORM
        g1 = glog.astype(BF16)
        r1 = glog - g1.astype(F32)
        g2 = r1.astype(BF16)
        g3 = (r1 - g2.astype(F32)).astype(BF16)
        cs = jnp.dot(tril_b, jnp.concatenate([g1, g2, g3], axis=1), preferred_element_type=F32)
        bc_all = cs[:, :qw] + cs[:, qw:2 * qw] + cs[:, 2 * qw:]
        for h in range(GLA_HEADS):
            ks = slice(h * GLA_DK, (h + 1) * GLA_DK)
            vs = slice(h * GLA_DV, (h + 1) * GLA_DV)
            q = q_ref[b, rs, ks].astype(F32)
            k = k_ref[b, rs, ks].astype(F32)
            v = v_ref[b, rs, vs]
            bc = bc_all[:, ks]
            b_last = bc[c - 1:c, :]
            q_dec = (q * GLA_DK ** -0.5 * jnp.exp(bc)).astype(BF16)
            k_inv = (k * jnp.exp(-bc)).astype(BF16)
            k_end = (k * jnp.exp(b_last - bc)).astype(BF16)
            decay = jnp.exp(b_last)
            attn = jnp.where(tril, _nt(q_dec, k_inv), 0.0).astype(BF16)
            st = st_ref[b * GLA_HEADS + h]
            o = jnp.dot(jnp.concatenate([q_dec, attn], axis=1),
                        jnp.concatenate([st.astype(BF16), v], axis=0), preferred_element_type=F32)
            upd = lax.dot_general(k_end, v, (((0,), (0,)), ((), ())), preferred_element_type=F32)
            decay_col = jnp.tile(jnp.broadcast_to(decay, (GLA_DK, GLA_DK)).T, (1, GLA_DV // GLA_DK))
            st_ref[b * GLA_HEADS + h] = st * decay_col + upd
            on = _rms(o, go_ref[...])
            o_ref[b, rs, vs] = (on * z_ref[b, rs, vs].astype(F32)).astype(o_ref.dtype)


def _gla(pm3, pz3, w2p, bg, gout):
    b, s, _ = pm3.shape
    c = GLA_CHUNK * GLA_CHUNKS_PER_STEP
    qw = GLA_HEADS * GLA_DK
    q_blk, k_blk = _PM_OFF["gq"] // qw, _PM_OFF["gk"] // qw
    v_blk, gl_blk = _PM_OFF["gv"] // GLA_WIDTH, _PM_OFF["glow"] // LANE
    z_blk = MLA_WIDTH // GLA_WIDTH
    return pl.pallas_call(
        functools.partial(_gla_kernel, nb=b),
        grid=(s // c,),
        in_specs=[pl.BlockSpec((b, c, qw), lambda i: (0, i, q_blk)),
                  pl.BlockSpec((b, c, qw), lambda i: (0, i, k_blk)),
                  pl.BlockSpec((b, c, GLA_WIDTH), lambda i: (0, i, v_blk)),
                  pl.BlockSpec((b, c, LANE), lambda i: (0, i, gl_blk)),
                  pl.BlockSpec((b, c, GLA_WIDTH), lambda i: (0, i, z_blk)),
                  pl.BlockSpec((LANE, qw), lambda i: (0, 0)),
                  pl.BlockSpec((1, qw), lambda i: (0, 0)),
                  pl.BlockSpec((1, GLA_DV), lambda i: (0, 0))],
        out_specs=pl.BlockSpec((b, c, GLA_WIDTH), lambda i: (0, i, 0)),
        out_shape=jax.ShapeDtypeStruct((b, s, GLA_WIDTH), BF16),
        scratch_shapes=[pltpu.VMEM((b * GLA_HEADS, GLA_DK, GLA_DV), F32)],
        compiler_params=_cp(1),
        name="gla",
    )(pm3, pm3, pm3, pm3, pz3, w2p, bg, gout)


def _dsa_kernel(iq_ref, iw_ref, onehot_ref, ikt_ref, q_ref, kt_ref, v_ref, z_ref, o_ref,
                key_ref, keyt_ref, iqs_ref, wb_ref, qs_ref, s_ref, acc_ref, m_ref, *, top_k):
    tq, tk = DSA_TQ, ATT_TK
    nk = key_ref.shape[0]
    lanes = tk // LANE
    i = pl.program_id(1)
    nkb = (i * tq + tq - 1) // tk + 1
    qpos = i * tq + lax.broadcasted_iota(jnp.int32, (tq, tk), 0)
    heads_per_dot = 4

    for h in range(IDX_HEADS):
        iqs_ref[h] = iq_ref[0, :, h * IDX_DH:(h + 1) * IDX_DH]

    wb = jnp.dot(iw_ref[0], onehot_ref[...], preferred_element_type=F32)
    for h in range(IDX_HEADS):
        wb_ref[h] = wb[:, h * LANE:(h + 1) * LANE] * (IDX_HEADS ** -0.5 * IDX_DH ** -0.5)

    def sortable(x):
        bits = lax.bitcast_convert_type(x, jnp.int32)
        return bits ^ ((bits >> 31) & 0x7FFFFFFF)

    qpos_t = i * tq + lax.broadcasted_iota(jnp.int32, (tk, tq), 1)

    def score_block(j, carry):
        ikt = ikt_ref[0, j]
        sc = jnp.zeros((tq, tk), F32)
        for hg in range(IDX_HEADS // heads_per_dot):
            iq = iqs_ref[hg * heads_per_dot:(hg + 1) * heads_per_dot].reshape(heads_per_dot * tq, IDX_DH)
            r = jnp.dot(iq, ikt, preferred_element_type=F32)
            for hh in range(heads_per_dot):
                wrow = jnp.tile(wb_ref[hg * heads_per_dot + hh], (1, lanes))
                sc = sc + jnp.maximum(r[hh * tq:(hh + 1) * tq, :], 0.0) * wrow
        kpos = j * tk + lax.broadcasted_iota(jnp.int32, (tq, tk), 1)
        key_ref[j] = jnp.where(kpos <= qpos, sortable(sc), INT_MIN)
        kpos_t = j * tk + lax.broadcasted_iota(jnp.int32, (tk, tq), 0)
        keyt_ref[j] = jnp.where(kpos_t <= qpos_t, sortable(sc.T), INT_MIN)
        return carry

    lax.fori_loop(0, nkb, score_block, 0)

    def count(pred):
        def body(j, acc):
            hit = jnp.where(pred(keyt_ref[j], j), 1, 0).astype(jnp.int32)
            return acc + hit.reshape(tk // 8, 8, tq).sum(axis=0)
        acc = lax.fori_loop(0, nkb, body, jnp.zeros((8, tq), jnp.int32))
        return acc.sum(axis=0, keepdims=True)

    def count_ge(cand):
        return count(lambda k, j: k >= cand)

    zero = jnp.zeros((1, tq), jnp.int32)
    c0 = count_ge(zero)
    thr = jnp.where(c0 >= top_k, zero, INT_MIN)
    n_ge = jnp.where(c0 >= top_k, c0, nkb * tk)

    def bit_step(it, carry):
        thr, n_ge = carry
        cand = thr | lax.shift_left(jnp.int32(1), 30 - it)
        c = count_ge(cand)
        return jnp.where(c >= top_k, cand, thr), jnp.where(c >= top_k, c, n_ge)

    thr, n_ge = lax.fori_loop(0, 31, bit_step, (thr, n_ge))

    overflow = (thr > INT_MIN) & (n_ge > top_k)

    @pl.when(jnp.max(jnp.where(overflow, 1, 0)) > 0)
    def _():
        n_take = top_k - count(lambda k, j: k > thr)
        rows = lax.broadcasted_iota(jnp.int32, (tk, tq), 0)
        bound = jnp.zeros((1, tq), jnp.int32)
        for bit in range((nk * tk - 1).bit_length() - 1, -1, -1):
            cand = bound | (1 << bit)
            below = count(lambda k, j: (k == thr) & (j * tk + rows < cand))
            bound = jnp.where(below < n_take, cand, bound)
        bound = jnp.where(overflow, bound, nk * tk)
        thr_c = jnp.tile(jnp.broadcast_to(thr, (tq, tq)).T, (1, lanes))
        bound_c = jnp.tile(jnp.broadcast_to(bound, (tq, tq)).T, (1, lanes))

        def demote(j, carry):
            k = key_ref[j]
            kpos = j * tk + lax.broadcasted_iota(jnp.int32, (tq, tk), 1)
            key_ref[j] = jnp.where((k == thr_c) & (kpos > bound_c), INT_MIN, k)
            return carry

        lax.fori_loop(0, nkb, demote, 0)

    thr = jnp.maximum(thr, INT_MIN + 1)
    thr_w = jnp.tile(jnp.broadcast_to(thr, (tq, tq)).T, (1, lanes))

    scale = DSA_DH ** -0.5 * LOG2E
    groups = range(DSA_KV_HEADS)

    def scores(g, j):
        return jnp.dot(qs_ref[g], kt_ref[0, g, j], preferred_element_type=F32)

    def consume(g, buf, j):
        sel = key_ref[j] >= thr_w
        s = s_ref[buf, g]
        s = jnp.concatenate([jnp.where(sel, s[r * tq:(r + 1) * tq, :], NEG) for r in range(DSA_REP)], axis=0)
        _sm_step(s, v_ref[0, g, j], acc_ref.at[g], m_ref.at[g])

    for g in groups:
        for r in range(DSA_REP):
            cs = slice((g * DSA_REP + r) * DSA_DH, (g * DSA_REP + r + 1) * DSA_DH)
            qs_ref[g, r * tq:(r + 1) * tq, :] = (q_ref[0, :, cs].astype(F32) * scale).astype(BF16)
        _sm_init(acc_ref.at[g], m_ref.at[g])
    for g in groups:
        s_ref[0, g] = scores(g, 0)

    def block_pair(p, carry):
        j = 2 * p
        for g in groups:
            consume(g, 0, j)
            s_ref[1, g] = scores(g, j + 1)
        for g in groups:
            consume(g, 1, j + 1)
            s_ref[0, g] = scores(g, jnp.minimum(j + 2, nk - 1))
        return carry

    lax.fori_loop(0, nkb // 2, block_pair, 0)

    @pl.when(nkb % 2 == 1)
    def _():
        for g in groups:
            consume(g, 0, nkb - 1)

    for g in groups:
        res = _sm_result(acc_ref.at[g], DSA_DH)
        for r in range(DSA_REP):
            cs = slice((g * DSA_REP + r) * DSA_DH, (g * DSA_REP + r + 1) * DSA_DH)
            o_ref[0, :, cs] = (res[r * tq:(r + 1) * tq, :] * z_ref[0, :, cs].astype(F32)).astype(o_ref.dtype)


def _dsa(pm3, pz3, ikt, dkt, dv_ext, top_k):
    b, s, _ = pm3.shape
    tq, tk = DSA_TQ, ATT_TK
    nk = s // tk
    q_blk = _PM_OFF["dq"] // DSA_WIDTH
    z_blk = (MLA_WIDTH + GLA_WIDTH) // DSA_WIDTH
    rows = DSA_REP * tq
    iq_blk = _PM_OFF["iq"] // (IDX_HEADS * IDX_DH)
    assert tq == LANE
    iw_blk = _PM_OFF["iw"] // LANE
    onehot = jnp.repeat(jnp.eye(LANE, IDX_HEADS, dtype=BF16), LANE, axis=1)
    return pl.pallas_call(
        functools.partial(_dsa_kernel, top_k=top_k),
        grid=(b, s // tq),
        in_specs=[pl.BlockSpec((1, tq, IDX_HEADS * IDX_DH), lambda bi, i: (bi, i, iq_blk)),
                  pl.BlockSpec((1, tq, LANE), lambda bi, i: (bi, i, iw_blk)),
                  pl.BlockSpec((LANE, IDX_HEADS * LANE), lambda bi, i: (0, 0)),
                  pl.BlockSpec((1, nk, IDX_DH, tk), lambda bi, i: (bi, 0, 0, 0)),
                  pl.BlockSpec((1, tq, DSA_WIDTH), lambda bi, i: (bi, i, q_blk)),
                  pl.BlockSpec((1, DSA_KV_HEADS, nk, DSA_DH, tk), lambda bi, i: (bi, 0, 0, 0, 0)),
                  pl.BlockSpec((1, DSA_KV_HEADS, nk, tk, 2 * DSA_DH), lambda bi, i: (bi, 0, 0, 0, 0)),
                  pl.BlockSpec((1, tq, DSA_WIDTH), lambda bi, i: (bi, i, z_blk))],
        out_specs=pl.BlockSpec((1, tq, DSA_WIDTH), lambda bi, i: (bi, i, 0)),
        out_shape=jax.ShapeDtypeStruct((b, s, DSA_WIDTH), BF16),
        scratch_shapes=[pltpu.VMEM((nk, tq, tk), jnp.int32),
                        pltpu.VMEM((nk, tk, tq), jnp.int32),
                        pltpu.VMEM((IDX_HEADS, tq, IDX_DH), BF16),
                        pltpu.VMEM((IDX_HEADS, tq, LANE), F32),
                        pltpu.VMEM((DSA_KV_HEADS, rows, DSA_DH), BF16),
                        pltpu.VMEM((2, DSA_KV_HEADS, rows, tk), F32),
                        pltpu.VMEM((DSA_KV_HEADS, rows, 2 * DSA_DH), F32),
                        pltpu.VMEM((DSA_KV_HEADS, rows, LANE), F32)],
        compiler_params=_cp(2),
        name="dsa",
    )(pm3, pm3, onehot, ikt, pm3, dkt, dv_ext, pz3)


def _dsa_layout_kernel(ik_ref, dk_ref, dv_ref, ikt_ref, kt_ref, vx_ref):
    ikt_ref[0, 0] = ik_ref[0].astype(F32).T[:IDX_DH, :].astype(BF16)
    ones = jnp.ones((ATT_TK, DSA_DH), BF16)
    for g in range(DSA_KV_HEADS):
        cs = slice(g * DSA_DH, (g + 1) * DSA_DH)
        kt_ref[0, g, 0] = dk_ref[0, :, cs].astype(F32).T.astype(BF16)
        vx_ref[0, g, 0, :, :DSA_DH] = dv_ref[0, :, cs]
        vx_ref[0, g, 0, :, DSA_DH:] = ones


def _dsa_layout(pm3):
    b, s, _ = pm3.shape
    tk = ATT_TK
    nk = s // tk
    kvw = DSA_KV_HEADS * DSA_DH
    ik_blk, dk_blk, dv_blk = _PM_OFF["ik"] // LANE, _PM_OFF["dk"] // kvw, _PM_OFF["dv"] // kvw
    return pl.pallas_call(
        _dsa_layout_kernel,
        grid=(b, nk),
        in_specs=[pl.BlockSpec((1, tk, LANE), lambda bi, j: (bi, j, ik_blk)),
                  pl.BlockSpec((1, tk, kvw), lambda bi, j: (bi, j, dk_blk)),
                  pl.BlockSpec((1, tk, kvw), lambda bi, j: (bi, j, dv_blk))],
        out_specs=[pl.BlockSpec((1, 1, IDX_DH, tk), lambda bi, j: (bi, j, 0, 0)),
                   pl.BlockSpec((1, DSA_KV_HEADS, 1, DSA_DH, tk), lambda bi, j: (bi, 0, j, 0, 0)),
                   pl.BlockSpec((1, DSA_KV_HEADS, 1, tk, 2 * DSA_DH), lambda bi, j: (bi, 0, j, 0, 0))],
        out_shape=[jax.ShapeDtypeStruct((b, nk, IDX_DH, tk), BF16),
                   jax.ShapeDtypeStruct((b, DSA_KV_HEADS, nk, DSA_DH, tk), BF16),
                   jax.ShapeDtypeStruct((b, DSA_KV_HEADS, nk, tk, 2 * DSA_DH), BF16)],
        compiler_params=_cp(2),
        name="dsa_layout",
    )(pm3, pm3, pm3)


def _lift_kernel(ym_ref, yg_ref, yd_ref, wm_ref, wg_ref, wd_ref, ga_ref, gb_ref, gc_ref, o_ref):
    a = jnp.dot(ym_ref[...], wm_ref[...], preferred_element_type=F32)
    b = jnp.dot(yg_ref[...], wg_ref[...], preferred_element_type=F32)
    c = jnp.dot(yd_ref[...], wd_ref[...], preferred_element_type=F32)
    o = ga_ref[...].astype(F32) * a + gb_ref[...].astype(F32) * b + gc_ref[...].astype(F32) * c
    o_ref[...] = o.astype(o_ref.dtype)


def _lift(y_mla, y_gla, y_dsa, w_bm, w_bg, w_bd, gates, *, tm=512, tn=1024):
    t = y_mla.shape[0]
    d = w_bm.shape[1]
    nj = d // tn
    row = lambda width: pl.BlockSpec((tm, width), lambda i, j: (i, 0))
    wcol = lambda depth: pl.BlockSpec((depth, tn), lambda i, j: (0, j))
    gate = lambda br: pl.BlockSpec((tm, tn), lambda i, j: (i, br * nj + j))
    return pl.pallas_call(
        _lift_kernel,
        grid=(t // tm, nj),
        in_specs=[row(MLA_WIDTH), row(GLA_WIDTH), row(DSA_WIDTH),
                  wcol(MLA_WIDTH), wcol(GLA_WIDTH), wcol(DSA_WIDTH),
                  gate(0), gate(1), gate(2)],
        out_specs=pl.BlockSpec((tm, tn), lambda i, j: (i, j)),
        out_shape=jax.ShapeDtypeStruct((t, d), BF16),
        compiler_params=_cp(2),
        name="lift_merge",
    )(y_mla, y_gla, y_dsa, w_bm, w_bg, w_bd, gates, gates, gates)


def _out_kernel(m_ref, w_ref, x_ref, gate_ref, o_ref):
    r = jnp.dot(m_ref[0], w_ref[...], preferred_element_type=F32)
    o_ref[0] = x_ref[0] + gate_ref[0] * r


def _out_proj(merged3, w_o, x, gate, *, tm=1024, tn=1024):
    b, s, d = x.shape
    tm = min(tm, s)
    return pl.pallas_call(
        _out_kernel,
        grid=(b, s // tm, d // tn),
        in_specs=[pl.BlockSpec((1, tm, d), lambda bi, i, j: (bi, i, 0)),
                  pl.BlockSpec((d, tn), lambda bi, i, j: (0, j)),
                  pl.BlockSpec((1, tm, tn), lambda bi, i, j: (bi, i, j)),
                  pl.BlockSpec((1, 1, tn), lambda bi, i, j: (bi, 0, j))],
        out_specs=pl.BlockSpec((1, tm, tn), lambda bi, i, j: (bi, i, j)),
        out_shape=jax.ShapeDtypeStruct((b, s, d), F32),
        compiler_params=_cp(3),
        name="out_proj",
    )(merged3, w_o, x, gate)


def _win_layout_kernel(w_ref, om_ref, oz_ref):
    half = MLA_ROPE // 2
    for name, width in _PM_LAYOUT:
        dst = _PM_OFF[name]
        if name == "pad":
            om_ref[:, dst:dst + width] = jnp.zeros((om_ref.shape[0], width), BF16)
        elif name in ("kr", "krr", "glow", "ik", "iw"):
            src_name = {"kr": "krope", "krr": "krope"}.get(name, name)
            src, w = _IN_OFF[src_name], _IN_W[src_name]
            om_ref[:, dst:dst + width] = jnp.zeros((om_ref.shape[0], width), BF16)
            if name == "krr":
                om_ref[:, dst:dst + half] = (-w_ref[:, src + half:src + w]).astype(BF16)
                om_ref[:, dst + half:dst + w] = w_ref[:, src:src + half].astype(BF16)
            else:
                om_ref[:, dst:dst + w] = w_ref[:, src:src + w].astype(BF16)
        else:
            src = _IN_OFF[name]
            om_ref[:, dst:dst + width] = w_ref[:, src:src + width].astype(BF16)
    oz_ref[...] = w_ref[:, _IN_OFF["z_mla"]:].astype(BF16)


def _win_layout(w_in, layer, *, tr=128):
    _, d, n = w_in.shape
    nz = n - _IN_OFF["z_mla"]
    return pl.pallas_call(
        _win_layout_kernel,
        grid=(d // tr,),
        in_specs=[pl.BlockSpec((None, tr, n), lambda i: (layer, i, 0))],
        out_specs=[pl.BlockSpec((tr, PM_WIDTH), lambda i: (i, 0)),
                   pl.BlockSpec((tr, nz), lambda i: (i, 0))],
        out_shape=[jax.ShapeDtypeStruct((d, PM_WIDTH), BF16), jax.ShapeDtypeStruct((d, nz), BF16)],
        compiler_params=_cp(1),
        name="win_layout",
    )(w_in)


def _mla_q_weights(w_uq):
    r = w_uq.shape[0]
    w = w_uq.reshape(r, MLA_HEADS, MLA_NOPE + MLA_ROPE)
    nope, rope = w[..., :MLA_NOPE], w[..., MLA_NOPE:]
    half = MLA_ROPE // 2
    zeros = jnp.zeros((r, MLA_HEADS, LANE - MLA_ROPE), w.dtype)
    w1 = jnp.concatenate([nope, rope, zeros], axis=-1).reshape(r, MLA_HEADS * MLA_QK_PAD)
    rot = jnp.concatenate([-rope[..., half:], rope[..., :half]], axis=-1)
    w2 = jnp.concatenate([rot, zeros], axis=-1).reshape(r, MLA_HEADS * LANE)
    return w1.astype(BF16), w2.astype(BF16)


def _layer(layer, x, c_pad, cos128, sin128, norm_g, w_ada, b_ada, w_in, mla_gq, mla_wuq, mla_gkv, mla_wukv,
           gla_wg2, gla_bg, gla_gout, w_mg, b_mg, w_bm, w_bg, w_bd, w_o):
    b, s, d = x.shape
    t = b * s
    mod = _mm(c_pad, w_ada, b_ada[None, :], tm=c_pad.shape[0], tn=512, out_dtype=F32,
              a_act="silu", layer=layer, name="ada")[:b]
    shift, scale, gate = (mod[:, None, k * d:(k + 1) * d] for k in range(3))
    h = _norm_mod(x, norm_g[None, :], scale, shift)
    h2 = h.reshape(t, d)

    mm_tm = min(1024, t)
    w_main, w_z = _win_layout(w_in, layer)
    pm = _mm(h2, w_main, None, tm=mm_tm, tn=1536, out_dtype=BF16, name="proj_main")
    pz = _mm(h2, w_z, None, tm=mm_tm, tn=1024, out_dtype=BF16, act="silu", name="proj_gate_paths")
    gates = _mm(h2, w_mg, b_mg[None, :], tm=mm_tm, tn=512, out_dtype=BF16,
                act="sigmoid", layer=layer, name="merge_gates")
    pm3 = pm.reshape(b, s, PM_WIDTH)
    pz3 = pz.reshape(b, s, d)

    w1, w2 = _mla_q_weights(mla_wuq)
    q = _mla_q(pm3, mla_gq[None, :], w1, w2, cos128, sin128)
    kcat, vt = _mla_kv(pm3, mla_gkv[None, :], mla_wukv.astype(BF16), cos128, sin128)
    y_mla = _mla_attn(q, kcat, vt, pz3)

    w2p = jnp.concatenate([gla_wg2, jnp.zeros((LANE - GLA_GATE_RANK, gla_wg2.shape[1]), gla_wg2.dtype)],
                          axis=0).astype(BF16)
    y_gla = _gla(pm3, pz3, w2p, gla_bg[None, :], gla_gout[None, :])

    ikt, dkt, dv_ext = _dsa_layout(pm3)
    y_dsa = _dsa(pm3, pz3, ikt, dkt, dv_ext, min(IDX_TOPK, s // 4))

    merged = _lift(y_mla.reshape(t, MLA_WIDTH), y_gla.reshape(t, GLA_WIDTH), y_dsa.reshape(t, DSA_WIDTH),
                   w_bm.astype(BF16), w_bg.astype(BF16), w_bd.astype(BF16), gates)
    return _out_proj(merged.reshape(b, s, d), w_o.astype(BF16), x, gate)


def kernel(x, c, positions, norm_g, w_ada, b_ada, w_in, mla_gq, mla_wuq, mla_gkv, mla_wukv, gla_wg2,
           gla_bg, gla_gout, w_mg, b_mg, w_bm, w_bg, w_bd, w_o, final_g):
    b = x.shape[0]
    cos128, sin128 = _rope_tables(positions)
    c_pad = jnp.concatenate([c, jnp.zeros((8 - b, c.shape[1]), c.dtype)], axis=0)
    for l in range(DEPTH):
        x = _layer(l, x, c_pad, cos128, sin128, norm_g[l], w_ada, b_ada[l], w_in, mla_gq[l],
                   mla_wuq[l], mla_gkv[l], mla_wukv[l], gla_wg2[l], gla_bg[l], gla_gout[l],
                   w_mg, b_mg[l], w_bm[l], w_bg[l], w_bd[l], w_o[l])
    return _final_norm(x, final_g[None, :])
```
